```python
import math
import jax, jax.numpy as jnp
from jax import lax
import numpy as np

D_MODEL = 2048
BATCH = 2
SEQ = 8192
DEPTH = 1
DEC_BATCH = 32
DEC_SEQ = 4
PAST_LEN = 16384
PAGE_SIZE = 128

MIX_WIDTH = D_MODEL
NSA_WIDTH = MIX_WIDTH // 2
CONV_CH = MIX_WIDTH - NSA_WIDTH
HEAD_DIM = 64
N_HEADS = NSA_WIDTH // HEAD_DIM
N_KV = 4
HPG = N_HEADS // N_KV
KV_WIDTH = N_KV * HEAD_DIM
CMP_BLOCK = 32
CMP_STRIDE = 16
CMP_HIDDEN = HEAD_DIM
SEL_BLOCK = 64
N_SEL = 16
WINDOW = 512
CONV_WIDTH = 31
N_BUCKETS = 32
MAX_DISTANCE = 128
Q_BLOCK = 128
FORCE_SCORE = 1e6
RMS_EPS = 1e-6
LN_EPS = 1e-5
N_IN = 2 * NSA_WIDTH + 6 * KV_WIDTH + 3 * N_HEADS + 3 * CONV_CH

kernel_name = 'hymba_nsa_conformer_step'


def _rmsnorm(x, g):
    x32 = x.astype(jnp.float32)
    y = x32 * lax.rsqrt(jnp.mean(x32 * x32, axis=-1, keepdims=True) + RMS_EPS)
    return (y * g.astype(jnp.float32)).astype(x.dtype)


def _layernorm(x, g, b):
    x32 = x.astype(jnp.float32)
    mu = jnp.mean(x32, axis=-1, keepdims=True)
    xc = x32 - mu
    y = xc * lax.rsqrt(jnp.mean(xc * xc, axis=-1, keepdims=True) + LN_EPS)
    return (y * g.astype(jnp.float32) + b.astype(jnp.float32)).astype(x.dtype)


def _t5_bucket(dist):
    dist = jnp.maximum(dist, 0)
    max_exact = N_BUCKETS // 2
    d32 = jnp.maximum(dist, 1).astype(jnp.float32)
    large = max_exact + (jnp.log(d32 / max_exact) / math.log(MAX_DISTANCE / max_exact)
                         * (N_BUCKETS - max_exact)).astype(jnp.int32)
    large = jnp.minimum(large, N_BUCKETS - 1)
    return jnp.where(dist < max_exact, dist, large)


def _masked_probs(logits, valid):
    logits = jnp.where(valid, logits, -jnp.inf)
    m = jnp.max(logits, axis=-1, keepdims=True)
    m = jnp.where(jnp.isfinite(m), m, 0.0)
    p = jnp.where(valid, jnp.exp(logits - m), 0.0)
    den = jnp.sum(p, axis=-1, keepdims=True)
    return p / jnp.where(den > 0, den, 1.0)


def _compress(k, w1, w2, pe):
    b, l = k.shape[:2]
    n_ch = l // CMP_STRIDE
    ch = k[:, :n_ch * CMP_STRIDE].reshape(b, n_ch, CMP_STRIDE, N_KV, HEAD_DIM)
    pre_a = jnp.einsum('bncgd,cdh->bngh', ch, w1[:CMP_STRIDE])
    pre_b = jnp.einsum('bncgd,cdh->bngh', ch, w1[CMP_STRIDE:])
    pe_bias = jnp.einsum('cd,cdh->h', pe, w1)
    hid = jax.nn.gelu(pre_a[:, :-1] + pre_b[:, 1:] + pe_bias)
    return jnp.einsum('bngh,hd->bngd', hid, w2)


def _select_positions(p_cmp, qpos, n_cmp, n_slc):
    c = jnp.arange(n_cmp)[:, None]
    j = jnp.arange(n_slc)[None, :]
    cover = ((c * CMP_STRIDE < (j + 1) * SEL_BLOCK)
             & (c * CMP_STRIDE + CMP_BLOCK > j * SEL_BLOCK)).astype(jnp.float32)
    imp = jnp.einsum('btgrc,cs->btgs', p_cmp, cover)
    jj = jnp.arange(n_slc)[None, :]
    cur = qpos[:, None] // SEL_BLOCK
    causal = (jj * SEL_BLOCK <= qpos[:, None])[:, None, :]
    forced = ((jj == 0) | (jj == cur) | (jj == cur - 1))[:, None, :]
    score = jnp.where(causal, imp, -jnp.inf)
    score = jnp.where(forced, FORCE_SCORE, score)
    k_sel = min(N_SEL, n_slc)
    _, idx = lax.top_k(score, k_sel)
    pos = idx[..., None] * SEL_BLOCK + jnp.arange(SEL_BLOCK)
    return pos.reshape(pos.shape[:3] + (k_sel * SEL_BLOCK,))


def _nsa_core(q, qpos, gates, kcb, vcb, fetch, n_slc, kw, vw, kwpos, rel_bias):
    scale = HEAD_DIM ** -0.5
    table = rel_bias.astype(jnp.float32).reshape(N_BUCKETS, N_KV, HPG)
    n_cmp = kcb.shape[1]
    cend = jnp.arange(n_cmp) * CMP_STRIDE + (CMP_BLOCK - 1)
    dist_c = qpos[:, None] - cend[None, :]
    bias_c = jnp.moveaxis(table[_t5_bucket(dist_c)], 1, -1)
    lc = jnp.einsum('btgrd,bcgd->btgrc', q, kcb).astype(jnp.float32) * scale + bias_c
    p_c = _masked_probs(lc, (dist_c >= 0)[:, None, None, :])
    o_c = jnp.einsum('btgrc,bcgd->btgrd', p_c.astype(vcb.dtype), vcb)
    pos = _select_positions(p_c, qpos, n_cmp, n_slc)
    ks, vs = fetch(pos)
    dist_s = qpos[None, :, None, None] - pos
    g_idx = jnp.arange(N_KV)[None, None, :, None]
    bias_s = jnp.moveaxis(table[_t5_bucket(dist_s), g_idx], -1, 3)
    ls = jnp.einsum('btgrd,btgkd->btgrk', q, ks).astype(jnp.float32) * scale + bias_s
    p_s = _masked_probs(ls, (dist_s >= 0)[:, :, :, None, :])
    o_s = jnp.einsum('btgrk,btgkd->btgrd', p_s.astype(vs.dtype), vs)
    dist_w = qpos[:, None] - kwpos[None, :]
    valid_w = (dist_w >= 0) & (dist_w < WINDOW) & (kwpos[None, :] >= 0)
    bias_w = jnp.moveaxis(table[_t5_bucket(dist_w)], 1, -1)
    lw = jnp.einsum('btgrd,bwgd->btgrw', q, kw).astype(jnp.float32) * scale + bias_w
    p_w = _masked_probs(lw, valid_w[:, None, None, :])
    o_w = jnp.einsum('btgrw,bwgd->btgrd', p_w.astype(vw.dtype), vw)
    return gates[..., 0:1] * o_c + gates[..., 1:2] * o_s + gates[..., 2:3] * o_w


def _nsa_prompt(q, gates, kc, vc, ks, vs, kw, vw, cmp_k, cmp_v, rel_bias):
    b, s = q.shape[:2]
    kcb = _compress(kc, *cmp_k)
    vcb = _compress(vc, *cmp_v)
    n_slc = -(-s // SEL_BLOCK)
    pad = ((0, 0), (WINDOW, 0), (0, 0), (0, 0))
    kw_pad = jnp.pad(kw, pad)
    vw_pad = jnp.pad(vw, pad)
    b_idx = jnp.arange(b)[:, None, None, None]
    g_idx = jnp.arange(N_KV)[None, None, :, None]

    def fetch(pos):
        p = jnp.clip(pos, 0, s - 1)
        return ks[b_idx, p, g_idx], vs[b_idx, p, g_idx]

    nqb = s // Q_BLOCK
    q_blocks = q.reshape(b, nqb, Q_BLOCK, N_KV, HPG, HEAD_DIM).swapaxes(0, 1)
    g_blocks = gates.reshape(b, nqb, Q_BLOCK, N_KV, HPG, 3).swapaxes(0, 1)

    def body(args):
        i, q_blk, g_blk = args
        start = i * Q_BLOCK
        qpos = start + jnp.arange(Q_BLOCK)
        kw_blk = lax.dynamic_slice_in_dim(kw_pad, start, WINDOW + Q_BLOCK, axis=1)
        vw_blk = lax.dynamic_slice_in_dim(vw_pad, start, WINDOW + Q_BLOCK, axis=1)
        kwpos = start - WINDOW + jnp.arange(WINDOW + Q_BLOCK)
        return _nsa_core(q_blk, qpos, g_blk, kcb, vcb, fetch, n_slc, kw_blk, vw_blk, kwpos, rel_bias)

    o = lax.map(body, (jnp.arange(nqb), q_blocks, g_blocks))
    return o.swapaxes(0, 1).reshape(b, s, NSA_WIDTH)


def _nsa_sample(layer, q, gates, kc, vc, ks, vs, kw, vw, cache_k_cmp, cache_v_cmp,
                cache_k_slc, cache_v_slc, k_win_buf, v_win_buf, page_table, cmp_k, cmp_v, rel_bias):
    db, t = q.shape[:2]
    n_pages = page_table.shape[1]
    past = n_pages * PAGE_SIZE
    kc_full = jnp.concatenate(
        [cache_k_cmp[layer, page_table].reshape(db, past, N_KV, HEAD_DIM).astype(kc.dtype), kc], axis=1)
    vc_full = jnp.concatenate(
        [cache_v_cmp[layer, page_table].reshape(db, past, N_KV, HEAD_DIM).astype(vc.dtype), vc], axis=1)
    kcb = _compress(kc_full, *cmp_k)
    vcb = _compress(vc_full, *cmp_v)
    n_slc = -(-(past + t) // SEL_BLOCK)
    b_idx = jnp.arange(db)[:, None, None, None]
    g_idx = jnp.arange(N_KV)[None, None, :, None]

    def fetch(pos):
        phys = page_table[b_idx, jnp.clip(pos // PAGE_SIZE, 0, n_pages - 1)]
        off = pos % PAGE_SIZE
        new_i = jnp.clip(pos - past, 0, t - 1)
        in_past = (pos < past)[..., None]
        k_sel = jnp.where(in_past, cache_k_slc[layer, phys, off, g_idx].astype(ks.dtype), ks[b_idx, new_i, g_idx])
        v_sel = jnp.where(in_past, cache_v_slc[layer, phys, off, g_idx].astype(vs.dtype), vs[b_idx, new_i, g_idx])
        return k_sel, v_sel

    kw_all = jnp.concatenate([k_win_buf.astype(kw.dtype), kw], axis=1)
    vw_all = jnp.concatenate([v_win_buf.astype(vw.dtype), vw], axis=1)
    n_buf = k_win_buf.shape[1]
    kwpos = past - n_buf + jnp.arange(n_buf + t)
    qpos = past + jnp.arange(t)
    o = _nsa_core(q, qpos, gates, kcb, vcb, fetch, n_slc, kw_all, vw_all, kwpos, rel_bias)
    n_keep = min(WINDOW, past + t)
    return o.reshape(db, t, NSA_WIDTH), kw_all[:, -n_keep:], vw_all[:, -n_keep:]


def _conformer_conv(c_in, buf, conv_w, conv_b, ln_g, ln_b, w_pw, b_pw):
    full = jnp.concatenate([buf.astype(c_in.dtype), c_in], axis=1)
    y = lax.conv_general_dilated(full, conv_w[:, None, :].astype(full.dtype), window_strides=(1,),
                                 padding='VALID', dimension_numbers=('NWC', 'WIO', 'NWC'),
                                 feature_group_count=CONV_CH) + conv_b
    y = jax.nn.silu(_layernorm(y, ln_g, ln_b))
    y = y @ w_pw + b_pw
    return y, full[:, -(CONV_WIDTH - 1):]


def _split_input(u):
    sizes = [NSA_WIDTH] + [KV_WIDTH] * 6 + [3 * N_HEADS, NSA_WIDTH, 2 * CONV_CH, CONV_CH]
    cuts = np.cumsum(sizes)[:-1].tolist()
    return jnp.split(u, cuts, axis=-1)


def _layer_inputs(x, g_pre, w_in):
    b, s = x.shape[:2]
    u = _rmsnorm(x, g_pre) @ w_in
    q, kc, vc, ks, vs, kw, vw, g_br, z_a, glu, z_b = _split_input(u)
    q = q.reshape(b, s, N_KV, HPG, HEAD_DIM)
    kc, vc, ks, vs, kw, vw = [a.reshape(b, s, N_KV, HEAD_DIM) for a in (kc, vc, ks, vs, kw, vw)]
    gates = jax.nn.sigmoid(g_br.reshape(b, s, N_KV, HPG, 3))
    a_half, g_half = jnp.split(glu, 2, axis=-1)
    c_in = a_half * jax.nn.sigmoid(g_half)
    return q, kc, vc, ks, vs, kw, vw, gates, z_a, c_in, z_b


def _layer_output(x, o_a, z_a, o_b, z_b, w_out, g_post):
    mixed = jnp.concatenate([o_a * jax.nn.silu(z_a), o_b * jax.nn.silu(z_b)], axis=-1)
    return x + _rmsnorm(mixed @ w_out, g_post)


def setup_inputs(seed: int = 0) -> dict:
    key = jax.random.key(seed)
    ks = jax.random.split(key, 32)
    f32 = jnp.float32
    n_pages = PAST_LEN // PAGE_SIZE
    n_used = DEC_BATCH * n_pages
    n_pool = n_used + n_used // 4
    win_buf = min(WINDOW, PAST_LEN)

    def nrm(k, shape, scale):
        return jax.random.normal(k, shape, f32) * scale

    page_table = jax.random.permutation(ks[0], n_pool)[:n_used].reshape(DEC_BATCH, n_pages).astype(jnp.int32)
    pool_shape = (DEPTH, n_pool, PAGE_SIZE, N_KV, HEAD_DIM)
    return {
        'x_prompt': nrm(ks[1], (BATCH, SEQ, D_MODEL), 1.0),
        'x_sample': nrm(ks[2], (DEC_BATCH, DEC_SEQ, D_MODEL), 1.0),
        'cache_k_cmp': nrm(ks[3], pool_shape, 1.0),
        'cache_v_cmp': nrm(ks[4], pool_shape, 1.0),
        'cache_k_slc': nrm(ks[5], pool_shape, 1.0),
        'cache_v_slc': nrm(ks[6], pool_shape, 1.0),
        'cache_k_win': nrm(ks[7], (DEPTH, DEC_BATCH, win_buf, N_KV, HEAD_DIM), 1.0),
        'cache_v_win': nrm(ks[8], (DEPTH, DEC_BATCH, win_buf, N_KV, HEAD_DIM), 1.0),
        'state_conv': nrm(ks[9], (DEPTH, DEC_BATCH, CONV_WIDTH - 1, CONV_CH), 0.5),
        'page_table': page_table,
        'g_pre': 1.0 + nrm(ks[10], (DEPTH, D_MODEL), 0.02),
        'w_in': nrm(ks[11], (DEPTH, D_MODEL, N_IN), D_MODEL ** -0.5),
        'cmp_w1_k': nrm(ks[12], (DEPTH, CMP_BLOCK, HEAD_DIM, CMP_HIDDEN), (CMP_BLOCK * HEAD_DIM) ** -0.5),
        'cmp_w2_k': nrm(ks[13], (DEPTH, CMP_HIDDEN, HEAD_DIM), CMP_HIDDEN ** -0.5),
        'cmp_pe_k': nrm(ks[14], (DEPTH, CMP_BLOCK, HEAD_DIM), 0.1),
        'cmp_w1_v': nrm(ks[15], (DEPTH, CMP_BLOCK, HEAD_DIM, CMP_HIDDEN), (CMP_BLOCK * HEAD_DIM) ** -0.5),
        'cmp_w2_v': nrm(ks[16], (DEPTH, CMP_HIDDEN, HEAD_DIM), CMP_HIDDEN ** -0.5),
        'cmp_pe_v': nrm(ks[17], (DEPTH, CMP_BLOCK, HEAD_DIM), 0.1),
        'rel_bias': nrm(ks[18], (N_BUCKETS, N_HEADS), 0.5),
        'conv_w': nrm(ks[19], (DEPTH, CONV_WIDTH, CONV_CH), CONV_WIDTH ** -0.5),
        'conv_b': nrm(ks[20], (DEPTH, CONV_CH), 0.01),
        'ln_g': 1.0 + nrm(ks[21], (DEPTH, CONV_CH), 0.02),
        'ln_b': nrm(ks[22], (DEPTH, CONV_CH), 0.01),
        'w_pw': nrm(ks[23], (DEPTH, CONV_CH, CONV_CH), CONV_CH ** -0.5),
        'b_pw': nrm(ks[24], (DEPTH, CONV_CH), 0.01),
        'w_out': nrm(ks[25], (DEPTH, MIX_WIDTH, D_MODEL), MIX_WIDTH ** -0.5),
        'g_post': 1.0 + nrm(ks[26], (DEPTH, D_MODEL), 0.02),
    }


def reference(x_prompt, x_sample, cache_k_cmp, cache_v_cmp, cache_k_slc, cache_v_slc,
              cache_k_win, cache_v_win, state_conv, page_table, g_pre, w_in,
              cmp_w1_k, cmp_w2_k, cmp_pe_k, cmp_w1_v, cmp_w2_v, cmp_pe_v, rel_bias,
              conv_w, conv_b, ln_g, ln_b, w_pw, b_pw, w_out, g_post):
    hp, hs = x_prompt, x_sample
    b, s = x_prompt.shape[:2]
    names = ('k_cmp_p', 'v_cmp_p', 'k_slc_p', 'v_slc_p', 'k_win_p', 'v_win_p', 'conv_p',
             'k_cmp_s', 'v_cmp_s', 'k_slc_s', 'v_slc_s', 'k_win_s', 'v_win_s', 'conv_s')
    acc = {n: [] for n in names}
    for layer in range(DEPTH):
        cmp_k = (cmp_w1_k[layer], cmp_w2_k[layer], cmp_pe_k[layer])
        cmp_v = (cmp_w1_v[layer], cmp_w2_v[layer], cmp_pe_v[layer])
        conv_params = (conv_w[layer], conv_b[layer], ln_g[layer], ln_b[layer], w_pw[layer], b_pw[layer])
        q, kc, vc, ks, vs, kw, vw, gates, z_a, c_in, z_b = _layer_inputs(hp, g_pre[layer], w_in[layer])
        o_a = _nsa_prompt(q, gates, kc, vc, ks, vs, kw, vw, cmp_k, cmp_v, rel_bias)
        zero_buf = jnp.zeros((b, CONV_WIDTH - 1, CONV_CH), c_in.dtype)
        o_b, conv_new = _conformer_conv(c_in, zero_buf, *conv_params)
        hp = _layer_output(hp, o_a, z_a, o_b, z_b, w_out[layer], g_post[layer])
        n_keep = min(WINDOW, s)
        for n, v in zip(names[:7], (kc, vc, ks, vs, kw[:, -n_keep:], vw[:, -n_keep:], conv_new)):
            acc[n].append(v)
        q, kc, vc, ks, vs, kw, vw, gates, z_a, c_in, z_b = _layer_inputs(hs, g_pre[layer], w_in[layer])
        o_a, kw_new, vw_new = _nsa_sample(layer, q, gates, kc, vc, ks, vs, kw, vw,
                                          cache_k_cmp, cache_v_cmp, cache_k_slc, cache_v_slc,
                                          cache_k_win[layer], cache_v_win[layer], page_table,
                                          cmp_k, cmp_v, rel_bias)
        o_b, conv_new = _conformer_conv(c_in, state_conv[layer], *conv_params)
        hs = _layer_output(hs, o_a, z_a, o_b, z_b, w_out[layer], g_post[layer])
        for n, v in zip(names[7:], (kc, vc, ks, vs, kw_new, vw_new, conv_new)):
            acc[n].append(v)
    y_prompt, y_sample = hp, hs
    k_cmp_p = jnp.stack(acc['k_cmp_p'])
    v_cmp_p = jnp.stack(acc['v_cmp_p'])
    k_slc_p = jnp.stack(acc['k_slc_p'])
    v_slc_p = jnp.stack(acc['v_slc_p'])
    k_win_p = jnp.stack(acc['k_win_p'])
    v_win_p = jnp.stack(acc['v_win_p'])
    conv_p = jnp.stack(acc['conv_p'])
    k_cmp_s = jnp.stack(acc['k_cmp_s'])
    v_cmp_s = jnp.stack(acc['v_cmp_s'])
    k_slc_s = jnp.stack(acc['k_slc_s'])
    v_slc_s = jnp.stack(acc['v_slc_s'])
    k_win_s = jnp.stack(acc['k_win_s'])
    v_win_s = jnp.stack(acc['v_win_s'])
    conv_s = jnp.stack(acc['conv_s'])
    return (y_prompt, y_sample, k_cmp_p, v_cmp_p, k_slc_p, v_slc_p, k_win_p, v_win_p, conv_p,
            k_cmp_s, v_cmp_s, k_slc_s, v_slc_s, k_win_s, v_win_s, conv_s)
```

```python
import functools
import math

import numpy as np
import jax
import jax.numpy as jnp
from jax import lax
from jax.experimental import pallas as pl
from jax.experimental.pallas import tpu as pltpu

F32 = jnp.float32
BF16 = jnp.bfloat16

HEAD_DIM = 64
N_KV = 4
HPG = 4
N_HEADS = N_KV * HPG
KV_WIDTH = N_KV * HEAD_DIM
NSA_WIDTH = N_HEADS * HEAD_DIM
CMP_BLOCK = 32
CMP_STRIDE = 16
SEL_BLOCK = 64
N_SEL = 16
WINDOW = 512
CONV_WIDTH = 31
N_BUCKETS = 32
MAX_DISTANCE = 128
FORCE_SCORE = 1e6
RMS_EPS = 1e-6
LN_EPS = 1e-5
PAGE_SIZE = 128
SCALE = HEAD_DIM ** -0.5

NEG = -1e30
PICKED = -3e38
LANES = 128
VMEM_LIMIT = 56 * 1024 * 1024

TQ = 256
CONV_HALO = 32
SAMPLE_PAGES = 16


def _cparams(sem):
    return pltpu.CompilerParams(dimension_semantics=sem, vmem_limit_bytes=VMEM_LIMIT)


def _dot(a, b):
    return jnp.dot(a, b, preferred_element_type=F32)


def _dot_nt(a, b):
    return lax.dot_general(a, b, (((1,), (1,)), ((), ())), preferred_element_type=F32)


def _sigmoid(x):
    return 1.0 / (1.0 + jnp.exp(-x))


def _split_hi_lo(x):
    hi = x.astype(BF16)
    lo = (x - hi.astype(F32)).astype(BF16)
    return hi, lo


def _normed(x_ref, g_ref):
    x = x_ref[0]
    ms = jnp.mean(x * x, axis=-1, keepdims=True)
    return (x * lax.rsqrt(ms + RMS_EPS) * g_ref[...]).astype(BF16)


def _proj_qkv_kernel(x_ref, g_ref, w_ref, q_ref, kc_ref, vc_ref, ks_ref, vs_ref, kw_ref, vw_ref,
                     *hm_refs, head_major):
    h = _normed(x_ref, g_ref)
    for g in range(N_KV):
        res = _dot(h, w_ref[:, g * KV_WIDTH:(g + 1) * KV_WIDTH]) * SCALE
        if head_major:
            for r in range(HPG):
                q_ref[0, g, r] = res[:, r * HEAD_DIM:(r + 1) * HEAD_DIM].astype(BF16)
        else:
            q_ref[0, :, g * KV_WIDTH:(g + 1) * KV_WIDTH] = res.astype(BF16)
    for j, o_ref in enumerate((kc_ref, vc_ref, ks_ref, vs_ref, kw_ref, vw_ref)):
        c0 = NSA_WIDTH + j * KV_WIDTH
        res = _dot(h, w_ref[:, c0:c0 + KV_WIDTH])
        o_ref[0] = res
        if head_major and j >= 2:
            for g in range(N_KV):
                hm_refs[j - 2][0, g] = res[:, g * HEAD_DIM:(g + 1) * HEAD_DIM].astype(BF16)


def _proj_gate_kernel(x_ref, g_ref, w_ref, sa_ref, sb_ref, gt_ref):
    h = _normed(x_ref, g_ref)
    for o_ref, base in ((sa_ref, 0), (sb_ref, NSA_WIDTH)):
        for c in range(NSA_WIDTH // 256):
            z = _dot(h, w_ref[:, base + c * 256: base + (c + 1) * 256])
            o_ref[0, :, c * 256:(c + 1) * 256] = z * _sigmoid(z)
    for c in range(2):
        z = _dot(h, w_ref[:, 2 * NSA_WIDTH + c * 256: 2 * NSA_WIDTH + (c + 1) * 256])
        gt_ref[0, :, c * 256:(c + 1) * 256] = _sigmoid(z)


def _proj_glu_kernel(x_ref, g_ref, w_ref, c_ref):
    h = _normed(x_ref, g_ref)
    n = c_ref.shape[-1]
    for c in range(n // 256):
        a = _dot(h, w_ref[:, c * 256:(c + 1) * 256])
        gg = _dot(h, w_ref[:, n + c * 256: n + (c + 1) * 256])
        c_ref[0, :, c * 256:(c + 1) * 256] = a * _sigmoid(gg)


def _projections(x, g_pre, w_qkv, w_gate, w_glu, *, tm, head_major):
    b, s, d = x.shape
    grid = (b, s // tm)
    x_spec = pl.BlockSpec((1, tm, d), lambda bi, i: (bi, i, 0))
    g_spec = pl.BlockSpec((1, d), lambda bi, i: (0, 0))

    def w_spec(w):
        return pl.BlockSpec(w.shape, lambda bi, i: (0, 0))

    def row_spec(width):
        return pl.BlockSpec((1, tm, width), lambda bi, i: (bi, i, 0))

    kv_shape = jax.ShapeDtypeStruct((b, s, KV_WIDTH), F32)
    if head_major:
        q_shape = jax.ShapeDtypeStruct((b, N_KV, HPG, s, HEAD_DIM), BF16)
        q_spec = pl.BlockSpec((1, N_KV, HPG, tm, HEAD_DIM), lambda bi, i: (bi, 0, 0, i, 0))
        hm_shape = [jax.ShapeDtypeStruct((b, N_KV, s, HEAD_DIM), BF16)] * 4
        hm_spec = [pl.BlockSpec((1, N_KV, tm, HEAD_DIM), lambda bi, i: (bi, 0, i, 0))] * 4
    else:
        q_shape = jax.ShapeDtypeStruct((b, s, NSA_WIDTH), BF16)
        q_spec = row_spec(NSA_WIDTH)
        hm_shape, hm_spec = [], []
    g2 = g_pre.reshape(1, d)
    qkv = pl.pallas_call(
        functools.partial(_proj_qkv_kernel, head_major=head_major),
        grid=grid,
        in_specs=[x_spec, g_spec, w_spec(w_qkv)],
        out_specs=[q_spec] + [row_spec(KV_WIDTH)] * 6 + hm_spec,
        out_shape=[q_shape] + [kv_shape] * 6 + hm_shape,
        compiler_params=_cparams(("parallel", "parallel")),
        name="proj_qkv",
    )(x, g2, w_qkv)
    sa, sb, gt = pl.pallas_call(
        _proj_gate_kernel,
        grid=grid,
        in_specs=[x_spec, g_spec, w_spec(w_gate)],
        out_specs=[row_spec(NSA_WIDTH), row_spec(NSA_WIDTH), row_spec(N_KV * LANES)],
        out_shape=[jax.ShapeDtypeStruct((b, s, NSA_WIDTH), F32)] * 2
        + [jax.ShapeDtypeStruct((b, s, N_KV * LANES), F32)],
        compiler_params=_cparams(("parallel", "parallel")),
        name="proj_gate",
    )(x, g2, w_gate)
    c_in = pl.pallas_call(
        _proj_glu_kernel,
        grid=grid,
        in_specs=[x_spec, g_spec, w_spec(w_glu)],
        out_specs=row_spec(w_glu.shape[1] // 2),
        out_shape=jax.ShapeDtypeStruct((b, s, w_glu.shape[1] // 2), F32),
        compiler_params=_cparams(("parallel", "parallel")),
        name="proj_glu",
    )(x, g2, w_glu)
    return qkv, sa, sb, gt, c_in


def _cmp1_kernel(*refs, n_in, has_pt):
    if has_pt:
        refs = refs[1:]
    x_refs, (pe_ref, w_ref, pre_ref, pepre_ref) = refs[:n_in], refs[n_in:]
    xs = [r[0] for r in x_refs]
    x = xs[0] if n_in == 1 else jnp.concatenate(xs, axis=0)
    pre_ref[0] = _dot(x.astype(BF16), w_ref[...])
    pepre_ref[...] = _dot(pe_ref[...], w_ref[...])


def _cmp2_kernel(pre_ref, pepre_ref, w2_ref, o_ref, *, head_major):
    pre = pre_ref[0]
    n_ch = pre.shape[0]
    a = pre[:, :KV_WIDTH]
    b_next = pltpu.roll(pre[:, KV_WIDTH:], n_ch - 1, axis=0)
    pe_bias = pepre_ref[0:1, :KV_WIDTH] + pepre_ref[1:2, KV_WIDTH:]
    z = a + b_next + pe_bias
    hid = 0.5 * z * (1.0 + jnp.tanh(math.sqrt(2.0 / math.pi) * (z + 0.044715 * (z * z * z))))
    out = _dot(hid.astype(BF16), w2_ref[...]).astype(BF16)
    if head_major:
        for g in range(N_KV):
            o_ref[0, g] = out[:, g * HEAD_DIM:(g + 1) * HEAD_DIM]
    else:
        o_ref[0] = out


def _cmp_weights(w1, w2, pe):
    eye = jnp.eye(N_KV, dtype=F32)
    halves = w1.reshape(2, CMP_STRIDE, HEAD_DIM, -1)
    hdim = halves.shape[-1]
    wbig = jnp.einsum('acdh,gk->cgdakh', halves, eye)
    wbig = wbig.reshape(CMP_STRIDE * KV_WIDTH, 2 * N_KV * hdim).astype(BF16)
    w2big = jnp.einsum('hd,gk->ghkd', w2, eye).reshape(N_KV * hdim, KV_WIDTH).astype(BF16)
    pe_rows = jnp.broadcast_to(pe.reshape(2, CMP_STRIDE, 1, HEAD_DIM), (2, CMP_STRIDE, N_KV, HEAD_DIM))
    pe8 = jnp.zeros((8, CMP_STRIDE * KV_WIDTH), F32).at[:2].set(pe_rows.reshape(2, -1)).astype(BF16)
    return wbig, w2big, pe8


def _compress(x_view, wts, *, page_table=None, head_major):
    wbig, w2big, pe8 = wts
    kdim, ncol = wbig.shape
    if page_table is None:
        b, n_ch, _ = x_view.shape
        rows = min(n_ch, 256)
        grid = (b, n_ch // rows)
        n_in = 1
        x_specs = [pl.BlockSpec((1, rows, kdim), lambda bi, i: (bi, i, 0))]
        const = lambda bi, i: (0, 0)
        out_map = lambda bi, i: (bi, i, 0)
        operands = [x_view]
        mk_grid = lambda **kw: dict(grid=grid, **kw)
    else:
        b, n_pages = page_table.shape
        rpp = x_view.shape[1]
        n_in = SAMPLE_PAGES
        rows = n_in * rpp
        n_ch = n_pages * rpp
        grid = (b, n_pages // n_in)
        x_specs = [pl.BlockSpec((1, rpp, kdim), functools.partial(
            lambda bi, i, pt, p: (pt[bi, i * SAMPLE_PAGES + p], 0, 0), p=p)) for p in range(n_in)]
        const = lambda bi, i, pt: (0, 0)
        out_map = lambda bi, i, pt: (bi, i, 0)
        operands = [page_table] + [x_view] * n_in
    in_specs = x_specs + [pl.BlockSpec(pe8.shape, const), pl.BlockSpec(wbig.shape, const)]
    out_specs = [pl.BlockSpec((1, rows, ncol), out_map), pl.BlockSpec((8, ncol), const)]
    out_shape = [jax.ShapeDtypeStruct((b, n_ch, ncol), F32), jax.ShapeDtypeStruct((8, ncol), F32)]
    kern = functools.partial(_cmp1_kernel, n_in=n_in, has_pt=page_table is not None)
    if page_table is None:
        call = pl.pallas_call(kern, grid=grid, in_specs=in_specs, out_specs=out_specs, out_shape=out_shape,
                              compiler_params=_cparams(("arbitrary", "arbitrary")), name="cmp_stage1")
    else:
        call = pl.pallas_call(
            kern,
            grid_spec=pltpu.PrefetchScalarGridSpec(num_scalar_prefetch=1, grid=grid, in_specs=in_specs,
                                                   out_specs=out_specs),
            out_shape=out_shape, compiler_params=_cparams(("arbitrary", "arbitrary")), name="cmp_stage1_paged")
    pre, pepre = call(*operands, pe8, wbig)
    if head_major:
        o_shape = jax.ShapeDtypeStruct((b, N_KV, n_ch, HEAD_DIM), BF16)
        o_spec = pl.BlockSpec((1, N_KV, n_ch, HEAD_DIM), lambda bi: (bi, 0, 0, 0))
    else:
        o_shape = jax.ShapeDtypeStruct((b, n_ch, KV_WIDTH), BF16)
        o_spec = pl.BlockSpec((1, n_ch, KV_WIDTH), lambda bi: (bi, 0, 0))
    return pl.pallas_call(
        functools.partial(_cmp2_kernel, head_major=head_major),
        grid=(b,),
        in_specs=[pl.BlockSpec((1, n_ch, ncol), lambda bi: (bi, 0, 0)),
                  pl.BlockSpec((8, ncol), lambda bi: (0, 0)),
                  pl.BlockSpec(w2big.shape, lambda bi: (0, 0))],
        out_specs=o_spec, out_shape=o_shape,
        compiler_params=_cparams(("parallel",)), name="cmp_stage2",
    )(pre, pepre, w2big)


def _t5_bucket(dist):
    dist = jnp.maximum(dist, 0)
    max_exact = N_BUCKETS // 2
    d32 = jnp.maximum(dist, 1).astype(F32)
    large = max_exact + (jnp.log(d32 / max_exact) / math.log(MAX_DISTANCE / max_exact)
                         * (N_BUCKETS - max_exact)).astype(jnp.int32)
    large = jnp.minimum(large, N_BUCKETS - 1)
    return jnp.where(dist < max_exact, dist, large)


def _bias_of_dist(rel_bias, dist):
    return jnp.moveaxis(rel_bias.astype(F32)[_t5_bucket(dist)], -1, 0)


def _select_top_blocks(score, jrow, k_sel):
    chosen = jnp.zeros(score.shape, F32)
    work = score
    for _ in range(k_sel):
        m = jnp.max(work, axis=0, keepdims=True)
        idx = jnp.min(jnp.where(work == m, jrow, 1 << 20), axis=0, keepdims=True)
        pick = jrow == idx
        chosen = jnp.where(pick, 1.0, chosen)
        work = jnp.where(pick, PICKED, work)
    return chosen


def _nsa_prompt_kernel(q_ref, gt_ref, sa_ref, kcb_ref, vcb_ref, ks_ref, vs_ref, kw_ref, vw_ref,
                       pc_ref, tz0_ref, tz1_ref, covt_ref, e3_ref, o_ref, m_ref, l_ref, acc_ref, *, n_slc):
    i = pl.program_id(2)
    rt = HPG * TQ
    ncp = kcb_ref.shape[2]
    cpt = TQ // CMP_STRIDE
    q = q_ref[0, 0].reshape(rt, HEAD_DIM)
    row_t = lax.broadcasted_iota(jnp.int32, (rt, 1), 0) & (TQ - 1)
    qpos_rt = i * TQ + row_t

    lc = _dot_nt(q, kcb_ref[0, 0])
    lc = lc + pltpu.roll(pc_ref[0], lax.rem(cpt * i - cpt + ncp, ncp), axis=1)
    cend = lax.broadcasted_iota(jnp.int32, (1, ncp), 1) * CMP_STRIDE + (CMP_BLOCK - 1)
    valid_c = cend <= qpos_rt
    lm = jnp.where(valid_c, lc, NEG)
    mc = jnp.max(lm, axis=-1, keepdims=True)
    ec = jnp.where(valid_c, jnp.exp(lm - mc), 0.0)
    den = jnp.sum(ec, axis=-1, keepdims=True)
    p_c = ec / jnp.where(den > 0, den, 1.0)
    o_c = _dot(p_c.astype(BF16), vcb_ref[0, 0])

    psum = p_c[0:TQ]
    for r in range(1, HPG):
        psum = psum + p_c[r * TQ:(r + 1) * TQ]
    hi, lo = _split_hi_lo(psum)
    imp_t = _dot_nt(covt_ref[...], hi) + _dot_nt(covt_ref[...], lo)
    jrow = lax.broadcasted_iota(jnp.int32, (LANES, TQ), 0)
    qpos_t = i * TQ + lax.broadcasted_iota(jnp.int32, (1, TQ), 1)
    cur = qpos_t // SEL_BLOCK
    causal = (jrow * SEL_BLOCK <= qpos_t) & (jrow < n_slc)
    forced = (jrow == 0) | (jrow == cur) | (jrow == cur - 1)
    score = jnp.where(forced, FORCE_SCORE, jnp.where(causal, imp_t, NEG))
    chosen = _select_top_blocks(score, jrow, min(N_SEL, n_slc))
    sel = jnp.where(causal, chosen, 0.0).T.astype(BF16)

    tt = lax.broadcasted_iota(jnp.int32, (TQ, TQ), 0)
    uu = lax.broadcasted_iota(jnp.int32, (TQ, TQ), 1)
    tz0 = tz0_ref[0].reshape(HPG, TQ, TQ)
    tz1 = tz1_ref[0].reshape(HPG, TQ, TQ)
    far = tz1[:, :, 0:1]

    def init_state():
        m_ref[...] = jnp.full(m_ref.shape, NEG, F32)
        l_ref[...] = jnp.zeros(l_ref.shape, F32)
        acc_ref[...] = jnp.zeros(acc_ref.shape, F32)

    def flash_tile(k_ref, v_ref, kt, bias, mask):
        rows = pl.ds(pl.multiple_of(kt * TQ, TQ), TQ)
        s = _dot_nt(q, k_ref[0, 0, rows, :]).reshape(HPG, TQ, TQ) + bias
        if mask is not None:
            s = jnp.where(mask[None], s, NEG)
        s = s.reshape(rt, TQ)
        m_old = m_ref[...]
        m_new = jnp.maximum(m_old, jnp.max(s, axis=-1, keepdims=True))
        alpha = jnp.exp(m_old - m_new)
        p = jnp.where(s > 0.5 * NEG, jnp.exp(s - m_new), 0.0)
        l_ref[...] = alpha * l_ref[...] + jnp.sum(p, axis=-1, keepdims=True)
        acc_ref[...] = alpha * acc_ref[...] + _dot(p.astype(BF16), v_ref[0, 0, rows, :])
        m_ref[...] = m_new

    def finish():
        l = l_ref[...]
        return acc_ref[...] / jnp.where(l > 0, l, 1.0)

    def sel_mask(kt):
        return _dot(sel, e3_ref[kt]) > 0.5

    init_state()

    def far_body(kt, carry):
        flash_tile(ks_ref, vs_ref, kt, far, sel_mask(kt))
        return carry

    lax.fori_loop(0, jnp.maximum(i - 1, 0), far_body, 0)

    @pl.when(i >= 1)
    def _():
        flash_tile(ks_ref, vs_ref, i - 1, tz1, sel_mask(i - 1))

    flash_tile(ks_ref, vs_ref, i, tz0, sel_mask(i) & (uu <= tt))
    o_s = finish()

    init_state()

    @pl.when(i >= 2)
    def _():
        flash_tile(kw_ref, vw_ref, i - 2, far, uu > tt)

    @pl.when(i >= 1)
    def _():
        flash_tile(kw_ref, vw_ref, i - 1, tz1, None)

    flash_tile(kw_ref, vw_ref, i, tz0, uu <= tt)
    o_w = finish()

    gt = gt_ref[0]
    sa = sa_ref[0]
    outs = []
    for r in range(HPG):
        rs = slice(r * TQ, (r + 1) * TQ)
        o = (gt[:, r:r + 1] * o_c[rs] + gt[:, HPG + r:HPG + r + 1] * o_s[rs]
             + gt[:, 2 * HPG + r:2 * HPG + r + 1] * o_w[rs])
        outs.append(o * sa[:, r * HEAD_DIM:(r + 1) * HEAD_DIM])
    o_ref[0] = jnp.concatenate(outs, axis=1).astype(BF16)


def _nsa_prompt(q_hm, gt, sa, kcb, vcb, ks_hm, vs_hm, kw_hm, vw_hm, rel_bias):
    b, _, _, s, _ = q_hm.shape
    assert WINDOW == 2 * TQ and s % TQ == 0
    nq = s // TQ
    ncp = s // CMP_STRIDE
    n_cmp = ncp - 1
    n_slc = -(-s // SEL_BLOCK)
    assert n_slc <= LANES and ncp >= 2 * (TQ // CMP_STRIDE) and ncp % LANES == 0
    rt = HPG * TQ
    cpt = TQ // CMP_STRIDE
    tt = jnp.arange(TQ)[:, None]
    uu = jnp.arange(TQ)[None, :]
    tz0 = _bias_of_dist(rel_bias, tt - uu).reshape(N_KV, rt, TQ)
    tz1 = _bias_of_dist(rel_bias, TQ + tt - uu).reshape(N_KV, rt, TQ)
    jj = jnp.arange(ncp)[None, :]
    dist_c = jnp.where(jj < 2 * cpt, tt - CMP_STRIDE * (jj - cpt) - (CMP_BLOCK - 1), MAX_DISTANCE)
    pc = _bias_of_dist(rel_bias, dist_c).reshape(N_KV, rt, ncp)
    c = np.arange(ncp)[None, :]
    j = np.arange(LANES)[:, None]
    cov_t = ((c * CMP_STRIDE < (j + 1) * SEL_BLOCK) & (c * CMP_STRIDE + CMP_BLOCK > j * SEL_BLOCK)
             & (c < n_cmp) & (j < n_slc))
    cov_t = jnp.asarray(cov_t, BF16)
    kt = np.arange(nq)[:, None, None]
    e3 = (np.arange(LANES)[None, :, None] == (kt * TQ + np.arange(TQ)[None, None, :]) // SEL_BLOCK)
    e3 = jnp.asarray(e3, BF16)

    full = lambda shape: pl.BlockSpec(shape, lambda bi, g, i: (0,) * len(shape))
    per_bg = lambda n: pl.BlockSpec((1, 1, n, HEAD_DIM), lambda bi, g, i: (bi, g, 0, 0))
    per_g = lambda n: pl.BlockSpec((1, rt, n), lambda bi, g, i: (g, 0, 0))
    return pl.pallas_call(
        functools.partial(_nsa_prompt_kernel, n_slc=n_slc),
        grid=(b, N_KV, nq),
        in_specs=[
            pl.BlockSpec((1, 1, HPG, TQ, HEAD_DIM), lambda bi, g, i: (bi, g, 0, i, 0)),
            pl.BlockSpec((1, TQ, LANES), lambda bi, g, i: (bi, i, g)),
            pl.BlockSpec((1, TQ, KV_WIDTH), lambda bi, g, i: (bi, i, g)),
            per_bg(ncp), per_bg(ncp), per_bg(s), per_bg(s), per_bg(s), per_bg(s),
            per_g(ncp), per_g(TQ), per_g(TQ), full(cov_t.shape), full(e3.shape),
        ],
        out_specs=pl.BlockSpec((1, TQ, KV_WIDTH), lambda bi, g, i: (bi, i, g)),
        out_shape=jax.ShapeDtypeStruct((b, s, NSA_WIDTH), BF16),
        scratch_shapes=[pltpu.VMEM((rt, 1), F32), pltpu.VMEM((rt, 1), F32), pltpu.VMEM((rt, HEAD_DIM), F32)],
        compiler_params=_cparams(("parallel", "parallel", "arbitrary")),
        name="nsa_prompt",
    )(q_hm, gt, sa, kcb, vcb, ks_hm, vs_hm, kw_hm, vw_hm, pc, tz0, tz1, cov_t, e3)


def _nsa_sample_kernel(*refs, n_pg, n_chunks, n_slc, past, t_new):
    pt_ref = refs[0]
    (qbd_ref, kcb_ref, vcb_ref, bc_ref, covt_ref, rmat_ref) = refs[1:7]
    kpg = refs[7:7 + n_pg]
    vpg = refs[7 + n_pg:7 + 2 * n_pg]
    (bsl_ref, ksn_ref, vsn_ref, bsn_ref, kwc_ref, vwc_ref, kwn_ref, vwn_ref, bw_ref, gt_ref,
     o_ref, sel_ref, m_ref, l_ref, acc_ref, oc_ref) = refs[7 + 2 * n_pg:]
    del pt_ref
    j = pl.program_id(1)
    qbd = qbd_ref[0]
    nsp = sel_ref.shape[0]

    def softmax_rows(s_t):
        m = jnp.max(s_t, axis=-1, keepdims=True)
        e = jnp.where(s_t > 0.5 * NEG, jnp.exp(s_t - m), 0.0)
        den = jnp.sum(e, axis=-1, keepdims=True)
        return e / jnp.where(den > 0, den, 1.0)

    def flash_step(s, v):
        s_t = s.T
        m_old = m_ref[...]
        m_new = jnp.maximum(m_old, jnp.max(s_t, axis=-1, keepdims=True))
        alpha = jnp.exp(m_old - m_new)
        p = jnp.where(s_t > 0.5 * NEG, jnp.exp(s_t - m_new), 0.0)
        l_ref[...] = alpha * l_ref[...] + jnp.sum(p, axis=-1, keepdims=True)
        acc_ref[...] = alpha * acc_ref[...] + _dot(p.astype(BF16), v)
        m_ref[...] = m_new

    @pl.when(j == 0)
    def _():
        lc = _dot(kcb_ref[0], qbd) + bc_ref[...]
        p_t = softmax_rows(lc.T)
        oc_ref[...] = _dot(p_t.astype(BF16), vcb_ref[0])
        hi, lo = _split_hi_lo(p_t.T)
        psum = _dot(hi, rmat_ref[...]) + _dot(lo, rmat_ref[...])
        hi, lo = _split_hi_lo(psum)
        imp_t = _dot(covt_ref[...], hi) + _dot(covt_ref[...], lo)
        jrow = lax.broadcasted_iota(jnp.int32, (nsp, LANES), 0)
        lane = lax.broadcasted_iota(jnp.int32, (1, LANES), 1)
        qpos = past + lane % t_new
        cur = qpos // SEL_BLOCK
        causal = (jrow * SEL_BLOCK <= qpos) & (jrow < n_slc)
        forced = (jrow == 0) | (jrow == cur) | (jrow == cur - 1)
        score = jnp.where(forced, FORCE_SCORE, jnp.where(causal, imp_t, NEG))
        chosen = _select_top_blocks(score, jrow, min(N_SEL, n_slc))
        sel_ref[...] = jnp.where(causal & (lane < N_HEADS * t_new), chosen, 0.0)
        m_ref[...] = jnp.full(m_ref.shape, NEG, F32)
        l_ref[...] = jnp.zeros(l_ref.shape, F32)
        acc_ref[...] = jnp.zeros(acc_ref.shape, F32)

    half = PAGE_SIZE // SEL_BLOCK
    kch = jnp.concatenate([r[0] for r in kpg], axis=0).astype(BF16)
    vch = jnp.concatenate([r[0] for r in vpg], axis=0).astype(BF16)
    s = _dot(kch, qbd)
    far = bsl_ref[PAGE_SIZE:PAGE_SIZE + 1, :]
    parts = []
    for p in range(n_pg):
        sp = s[p * PAGE_SIZE:(p + 1) * PAGE_SIZE]
        if p == n_pg - 1:
            sp = sp + jnp.where(j == n_chunks - 1, bsl_ref[0:PAGE_SIZE, :], far)
        else:
            sp = sp + far
        blk = (j * n_pg + p) * half
        rows = [jnp.broadcast_to(sel_ref[pl.ds(blk + h, 1), :], (SEL_BLOCK, LANES)) for h in range(half)]
        parts.append(jnp.where(jnp.concatenate(rows, axis=0) > 0.5, sp, NEG))
    flash_step(jnp.concatenate(parts, axis=0), vch)

    @pl.when(j == n_chunks - 1)
    def _():
        sn = _dot(ksn_ref[0].astype(BF16), qbd) + bsn_ref[...]
        seln = jnp.broadcast_to(sel_ref[n_slc - 1:n_slc, :], sn.shape)
        flash_step(jnp.where(seln > 0.5, sn, NEG), vsn_ref[0].astype(BF16))
        l = l_ref[...]
        o_s = acc_ref[...] / jnp.where(l > 0, l, 1.0)
        kwin = jnp.concatenate([kwc_ref[0], kwn_ref[0]], axis=0).astype(BF16)
        vwin = jnp.concatenate([vwc_ref[0], vwn_ref[0]], axis=0).astype(BF16)
        p_w = softmax_rows((_dot(kwin, qbd) + bw_ref[...]).T)
        o_w = _dot(p_w.astype(BF16), vwin)
        gt = gt_ref[0]
        o_ref[0] = gt[:, 0:1] * oc_ref[...] + gt[:, 1:2] * o_s + gt[:, 2:3] * o_w


def _nsa_sample(q, gates, kcb, vcb, cache_k_slc, cache_v_slc, page_table, ks_new, vs_new,
                kw_cache, vw_cache, kw_new, vw_new, rel_bias, *, past):
    db, t_new, _ = q.shape
    n_pages = page_table.shape[1]
    assert past == n_pages * PAGE_SIZE and past % SEL_BLOCK == 0 and t_new <= SEL_BLOCK
    n_cp = kcb.shape[1]
    n_slc = -(-(past + t_new) // SEL_BLOCK)
    nsp = -(-n_slc // 8) * 8
    n_pg = SAMPLE_PAGES
    n_chunks = n_pages // n_pg
    n_win = kw_cache.shape[1]
    nl = N_HEADS * t_new
    assert nl <= LANES
    lane = np.arange(LANES)
    lane_ok = lane < nl
    lane_h = np.where(lane_ok, lane // t_new, 0)
    lane_g = lane_h // HPG
    lane_t = lane % t_new
    q5 = q.reshape(db, t_new, N_KV, HPG, HEAD_DIM)
    qbd = jnp.einsum('btgrd,gk->bkdgrt', q5.astype(F32), jnp.eye(N_KV, dtype=F32))
    qbd = qbd.reshape(db, KV_WIDTH, nl)
    qbd = jnp.pad(qbd, ((0, 0), (0, 0), (0, LANES - nl))).astype(BF16)

    table = rel_bias.astype(F32)

    def bias_tab(dist, valid):
        vals = table[_t5_bucket(jnp.asarray(dist)), lane_h[None, :]]
        return jnp.where(jnp.asarray(valid & lane_ok[None, :]), vals, NEG).astype(F32)

    qpos = past + lane_t[None, :]
    cend = (np.arange(n_cp) * CMP_STRIDE + CMP_BLOCK - 1)[:, None]
    bc = bias_tab(qpos - cend, (qpos - cend >= 0) & (np.arange(n_cp)[:, None] < n_cp - 1))
    kpos = (past - PAGE_SIZE + np.arange(PAGE_SIZE + 8))[:, None]
    bsl = bias_tab(np.where(np.arange(PAGE_SIZE + 8)[:, None] < PAGE_SIZE, qpos - kpos, MAX_DISTANCE),
                   np.ones((PAGE_SIZE + 8, LANES), bool))
    u = np.arange(PAGE_SIZE)[:, None]
    new_ok = (u <= lane_t[None, :]) & (u < t_new)
    bsn = bias_tab(lane_t[None, :] - u, new_ok)
    w = np.arange(n_win)[:, None]
    dist_w = qpos - (past - n_win + w)
    bw = jnp.concatenate([bias_tab(dist_w, (dist_w >= 0) & (dist_w < WINDOW) & (past - n_win + w >= 0)),
                          bias_tab(lane_t[None, :] - u, new_ok & (lane_t[None, :] - u < WINDOW))], axis=0)
    c = np.arange(n_cp)[None, :]
    jb = np.arange(nsp)[:, None]
    cov_t = ((c * CMP_STRIDE < (jb + 1) * SEL_BLOCK) & (c * CMP_STRIDE + CMP_BLOCK > jb * SEL_BLOCK)
             & (c < n_cp - 1) & (jb < n_slc))
    cov_t = jnp.asarray(cov_t, BF16)
    rmat = (lane_g[:, None] == lane_g[None, :]) & (lane_t[:, None] == lane_t[None, :])
    rmat = jnp.asarray(rmat & lane_ok[:, None] & lane_ok[None, :], BF16)
    g5 = gates.reshape(db, t_new, N_KV, LANES)[..., :3 * HPG].reshape(db, t_new, N_KV, 3, HPG)
    gcol = jnp.transpose(g5, (0, 2, 4, 1, 3)).reshape(db, nl, 3)
    gcol = jnp.pad(gcol, ((0, 0), (0, LANES - nl), (0, 5)))

    def pad_new(a):
        return jnp.pad(a, ((0, 0), (0, PAGE_SIZE - t_new), (0, 0)))

    ksn, vsn, kwn, vwn = (pad_new(a) for a in (ks_new, vs_new, kw_new, vw_new))

    per_b = lambda shape: pl.BlockSpec((1,) + shape, lambda b, jc, pt: (b, 0, 0))
    full = lambda a: pl.BlockSpec(a.shape, lambda b, jc, pt: (0, 0))
    page = lambda p: pl.BlockSpec((1, PAGE_SIZE, KV_WIDTH), lambda b, jc, pt: (pt[b, jc * SAMPLE_PAGES + p], 0, 0))
    in_specs = ([per_b((KV_WIDTH, LANES)), per_b((n_cp, KV_WIDTH)), per_b((n_cp, KV_WIDTH)),
                 full(bc), full(cov_t), full(rmat)]
                + [page(p) for p in range(n_pg)] * 2
                + [full(bsl), per_b((PAGE_SIZE, KV_WIDTH)), per_b((PAGE_SIZE, KV_WIDTH)), full(bsn),
                   per_b((n_win, KV_WIDTH)), per_b((n_win, KV_WIDTH)),
                   per_b((PAGE_SIZE, KV_WIDTH)), per_b((PAGE_SIZE, KV_WIDTH)), full(bw), per_b((LANES, 8))])
    o = pl.pallas_call(
        functools.partial(_nsa_sample_kernel, n_pg=n_pg, n_chunks=n_chunks, n_slc=n_slc, past=past, t_new=t_new),
        grid_spec=pltpu.PrefetchScalarGridSpec(
            num_scalar_prefetch=1, grid=(db, n_chunks), in_specs=in_specs,
            out_specs=pl.BlockSpec((1, LANES, KV_WIDTH), lambda b, jc, pt: (b, 0, 0)),
            scratch_shapes=[pltpu.VMEM((nsp, LANES), F32), pltpu.VMEM((LANES, 1), F32), pltpu.VMEM((LANES, 1), F32),
                            pltpu.VMEM((LANES, KV_WIDTH), F32), pltpu.VMEM((LANES, KV_WIDTH), F32)]),
        out_shape=jax.ShapeDtypeStruct((db, LANES, KV_WIDTH), F32),
        compiler_params=_cparams(("arbitrary", "arbitrary")),
        name="nsa_sample",
    )(page_table, qbd, kcb, vcb, bc, cov_t, rmat, *([cache_k_slc] * n_pg), *([cache_v_slc] * n_pg),
      bsl, ksn, vsn, bsn, kw_cache, vw_cache, kwn, vwn, bw, gcol)
    o6 = o[:, :nl].reshape(db, N_KV, HPG, t_new, N_KV, HEAD_DIM)
    o_diag = jnp.stack([o6[:, g, :, :, g] for g in range(N_KV)], axis=1)
    return jnp.transpose(o_diag, (0, 3, 1, 2, 4)).reshape(db, t_new, NSA_WIDTH)


def _conv_tail(y, cb_ref, lg_ref, lb_ref, wpw_ref, bpw_ref, sb):
    y = y + cb_ref[...]
    mu = jnp.mean(y, axis=-1, keepdims=True)
    yc = y - mu
    var = jnp.mean(yc * yc, axis=-1, keepdims=True)
    yn = yc * lax.rsqrt(var + LN_EPS) * lg_ref[...] + lb_ref[...]
    act = yn * _sigmoid(yn)
    return ((_dot(act.astype(BF16), wpw_ref[...]) + bpw_ref[...]) * sb).astype(BF16)


def _conv_prompt_kernel(c_ref, halo_ref, init_ref, cw_ref, cb_ref, lg_ref, lb_ref, wpw_ref, bpw_ref, sb_ref,
                        o_ref, full_ref, y_ref, *, ts):
    j = pl.program_id(1)
    full_ref[CONV_HALO:CONV_HALO + ts, :] = c_ref[0]

    @pl.when(j == 0)
    def _():
        full_ref[0:CONV_HALO, :] = init_ref[0]

    @pl.when(j > 0)
    def _():
        full_ref[0:CONV_HALO, :] = halo_ref[0]

    first = CONV_HALO - (CONV_WIDTH - 1)
    rb = 64
    ch = full_ref.shape[1]
    for c0 in range(0, ch, LANES):
        for r0 in range(0, ts, rb):
            acc = jnp.zeros((rb, LANES), F32)
            for w in range(CONV_WIDTH):
                acc = acc + full_ref[r0 + first + w:r0 + first + w + rb, c0:c0 + LANES] * cw_ref[w:w + 1, c0:c0 + LANES]
            y_ref[r0:r0 + rb, c0:c0 + LANES] = acc
    o_ref[0] = _conv_tail(y_ref[...], cb_ref, lg_ref, lb_ref, wpw_ref, bpw_ref, sb_ref[0])


def _conv_sample_kernel(c_ref, st_ref, cw_ref, cb_ref, lg_ref, lb_ref, wpw_ref, bpw_ref, sb_ref,
                        o_ref, full_ref, y_ref):
    nb, t_new, _ = c_ref.shape
    n_st = st_ref.shape[1]
    full_ref[:, 0:n_st, :] = st_ref[...]
    full_ref[:, n_st:n_st + t_new, :] = c_ref[...]
    first = n_st - (CONV_WIDTH - 1)
    for b in range(nb):
        acc = jnp.zeros((t_new, full_ref.shape[2]), F32)
        for w in range(CONV_WIDTH):
            acc = acc + full_ref[b, first + w:first + w + t_new, :] * cw_ref[w:w + 1, :]
        y_ref[b * t_new:(b + 1) * t_new, :] = acc
    o_ref[...] = _conv_tail(y_ref[...], cb_ref, lg_ref, lb_ref, wpw_ref, bpw_ref, sb_ref[...])


def _conv_params(conv_w, conv_b, ln_g, ln_b, w_pw, b_pw):
    ch = conv_w.shape[1]
    cw = jnp.pad(conv_w, ((0, 32 - CONV_WIDTH), (0, 0)))
    return (cw, conv_b.reshape(1, ch), ln_g.reshape(1, ch), ln_b.reshape(1, ch), w_pw.astype(BF16),
            b_pw.reshape(1, ch))


def _conv_prompt(c_in, init, params, sb, *, ts):
    b, s, ch = c_in.shape
    hb = ts // CONV_HALO
    const = lambda a: pl.BlockSpec(a.shape, lambda bi, j: (0,) * a.ndim)
    return pl.pallas_call(
        functools.partial(_conv_prompt_kernel, ts=ts),
        grid=(b, s // ts),
        in_specs=[pl.BlockSpec((1, ts, ch), lambda bi, j: (bi, j, 0)),
                  pl.BlockSpec((1, CONV_HALO, ch), lambda bi, j: (bi, jnp.maximum(j * hb - 1, 0), 0)),
                  pl.BlockSpec((1, CONV_HALO, ch), lambda bi, j: (bi, 0, 0))]
        + [const(a) for a in params]
        + [pl.BlockSpec((1, ts, ch), lambda bi, j: (bi, j, 0))],
        out_specs=pl.BlockSpec((1, ts, ch), lambda bi, j: (bi, j, 0)),
        out_shape=jax.ShapeDtypeStruct((b, s, ch), BF16),
        scratch_shapes=[pltpu.VMEM((CONV_HALO + ts, ch), F32), pltpu.VMEM((ts, ch), F32)],
        compiler_params=_cparams(("parallel", "arbitrary")),
        name="conv_prompt",
    )(c_in, c_in, init, *params, sb)


def _conv_sample(c_in, state, params, sb):
    db, t_new, ch = c_in.shape
    n_st = state.shape[1]
    rows_pad = -(-(n_st + t_new) // 8) * 8
    return pl.pallas_call(
        _conv_sample_kernel,
        out_shape=jax.ShapeDtypeStruct((db * t_new, ch), BF16),
        scratch_shapes=[pltpu.VMEM((db, rows_pad, ch), F32), pltpu.VMEM((db * t_new, ch), F32)],
        compiler_params=pltpu.CompilerParams(vmem_limit_bytes=VMEM_LIMIT),
        name="conv_sample",
    )(c_in, state, *params, sb)


def _out_kernel(x_ref, ma_ref, mb_ref, *rest, gated):
    if gated:
        sa_ref, wa_ref, wb_ref, gp_ref, y_ref = rest
        ma = (ma_ref[0] * sa_ref[0]).astype(BF16)
    else:
        wa_ref, wb_ref, gp_ref, y_ref = rest
        ma = ma_ref[0]
    z = _dot(ma, wa_ref[...]) + _dot(mb_ref[0], wb_ref[...])
    ms = jnp.mean(z * z, axis=-1, keepdims=True)
    y_ref[0] = x_ref[0] + z * lax.rsqrt(ms + RMS_EPS) * gp_ref[...]


def _out_proj(x, ma, mb, sa, w_out, g_post, *, tm):
    b, s, d = x.shape
    na = ma.shape[-1]
    wa = w_out[:na].astype(BF16)
    wb = w_out[na:].astype(BF16)
    row = lambda width: pl.BlockSpec((1, tm, width), lambda bi, i: (bi, i, 0))
    const = lambda a: pl.BlockSpec(a.shape, lambda bi, i: (0, 0))
    gp = g_post.reshape(1, d)
    gate_in, gate_spec = ([sa], [row(na)]) if sa is not None else ([], [])
    return pl.pallas_call(
        functools.partial(_out_kernel, gated=sa is not None),
        grid=(b, s // tm),
        in_specs=[row(d), row(na), row(mb.shape[-1])] + gate_spec + [const(wa), const(wb), const(gp)],
        out_specs=row(d),
        out_shape=jax.ShapeDtypeStruct((b, s, d), F32),
        compiler_params=_cparams(("parallel", "parallel")),
        name="out_proj",
    )(x, ma, mb, *gate_in, wa, wb, gp)


def _split_w_in(w_in):
    d = w_in.shape[0]
    c0 = NSA_WIDTH + 6 * KV_WIDTH
    n_gate = 3 * N_HEADS
    conv_ch = (w_in.shape[1] - c0 - n_gate - NSA_WIDTH) // 3
    w_qkv = w_in[:, :c0].astype(BF16)
    wg = w_in[:, c0:c0 + n_gate].reshape(d, N_KV, HPG, 3)
    wg = jnp.transpose(wg, (0, 1, 3, 2)).reshape(d, N_KV, 3 * HPG)
    wg = jnp.pad(wg, ((0, 0), (0, 0), (0, LANES - 3 * HPG))).reshape(d, N_KV * LANES)
    z_a = w_in[:, c0 + n_gate:c0 + n_gate + NSA_WIDTH]
    glu0 = c0 + n_gate + NSA_WIDTH
    w_glu = w_in[:, glu0:glu0 + 2 * conv_ch].astype(BF16)
    z_b = w_in[:, glu0 + 2 * conv_ch:]
    assert conv_ch == NSA_WIDTH
    w_gate = jnp.concatenate([z_a, z_b, wg], axis=1).astype(BF16)
    return w_qkv, w_gate, w_glu


def kernel(x_prompt, x_sample, cache_k_cmp, cache_v_cmp, cache_k_slc, cache_v_slc, cache_k_win, cache_v_win,
           state_conv, page_table, g_pre, w_in, cmp_w1_k, cmp_w2_k, cmp_pe_k, cmp_w1_v, cmp_w2_v, cmp_pe_v,
           rel_bias, conv_w, conv_b, ln_g, ln_b, w_pw, b_pw, w_out, g_post):
    depth = g_pre.shape[0]
    assert depth == 1
    layer = 0
    b, s, d = x_prompt.shape
    db, t_new, _ = x_sample.shape
    n_pool = cache_k_cmp.shape[1]
    past = page_table.shape[1] * PAGE_SIZE
    conv_ch = conv_w.shape[-1]

    w_qkv, w_gate, w_glu = _split_w_in(w_in[layer])
    wts_k = _cmp_weights(cmp_w1_k[layer], cmp_w2_k[layer], cmp_pe_k[layer])
    wts_v = _cmp_weights(cmp_w1_v[layer], cmp_w2_v[layer], cmp_pe_v[layer])
    cparams = _conv_params(conv_w[layer], conv_b[layer], ln_g[layer], ln_b[layer], w_pw[layer], b_pw[layer])
    chunk_w = CMP_STRIDE * KV_WIDTH

    (q_hm, kc, vc, ks, vs, kw, vw, ks_hm, vs_hm, kw_hm, vw_hm), sa, sb, gt, c_in = _projections(
        x_prompt, g_pre[layer], w_qkv, w_gate, w_glu, tm=512, head_major=True)
    n_ch = s // CMP_STRIDE
    kcb = _compress(kc[:, :n_ch * CMP_STRIDE].reshape(b, n_ch, chunk_w), wts_k, head_major=True)
    vcb = _compress(vc[:, :n_ch * CMP_STRIDE].reshape(b, n_ch, chunk_w), wts_v, head_major=True)
    ma = _nsa_prompt(q_hm, gt, sa, kcb, vcb, ks_hm, vs_hm, kw_hm, vw_hm, rel_bias)
    mb = _conv_prompt(c_in, jnp.zeros((b, CONV_HALO, conv_ch), F32), cparams, sb, ts=256)
    y_prompt = _out_proj(x_prompt, ma, mb, None, w_out[layer], g_post[layer], tm=512)
    n_keep = min(WINDOW, s)
    kv5 = lambda a: a.reshape(1, a.shape[0], a.shape[1], N_KV, HEAD_DIM)
    outs_p = (kv5(kc), kv5(vc), kv5(ks), kv5(vs), kv5(kw[:, -n_keep:]), kv5(vw[:, -n_keep:]),
              c_in[None, :, -(CONV_WIDTH - 1):])

    xs = x_sample.reshape(1, db * t_new, d)
    (q_s, kc_s, vc_s, ks_s, vs_s, kw_s, vw_s), sa_s, sb_s, gt_s, c_s = _projections(
        xs, g_pre[layer], w_qkv, w_gate, w_glu, tm=db * t_new, head_major=False)
    tok = lambda a: a.reshape(db, t_new, a.shape[-1])
    page_rows = PAGE_SIZE // CMP_STRIDE
    kcb_s = _compress(cache_k_cmp[layer].reshape(n_pool, page_rows, chunk_w), wts_k, page_table=page_table,
                      head_major=False)
    vcb_s = _compress(cache_v_cmp[layer].reshape(n_pool, page_rows, chunk_w), wts_v, page_table=page_table,
                      head_major=False)
    n_buf = cache_k_win.shape[2]
    o_a = _nsa_sample(tok(q_s), tok(gt_s), kcb_s, vcb_s,
                      cache_k_slc[layer].reshape(n_pool, PAGE_SIZE, KV_WIDTH),
                      cache_v_slc[layer].reshape(n_pool, PAGE_SIZE, KV_WIDTH), page_table,
                      tok(ks_s), tok(vs_s), cache_k_win[layer].reshape(db, n_buf, KV_WIDTH),
                      cache_v_win[layer].reshape(db, n_buf, KV_WIDTH), tok(kw_s), tok(vw_s), rel_bias, past=past)
    mb_s = _conv_sample(tok(c_s), state_conv[layer], cparams, sb_s[0])
    y_sample = _out_proj(xs, o_a.reshape(1, db * t_new, NSA_WIDTH), mb_s[None], sa_s, w_out[layer],
                         g_post[layer], tm=db * t_new).reshape(db, t_new, d)
    n_keep_s = min(WINDOW, past + t_new)
    kv5s = lambda a: a.reshape(1, db, t_new, N_KV, HEAD_DIM)
    win = lambda cache, new: jnp.concatenate(
        [cache[layer], new.reshape(db, t_new, N_KV, HEAD_DIM)], axis=1)[None, :, -n_keep_s:]
    conv_s = jnp.concatenate([state_conv[layer], tok(c_s)], axis=1)[None, :, -(CONV_WIDTH - 1):]
    outs_s = (kv5s(kc_s), kv5s(vc_s), kv5s(ks_s), kv5s(vs_s), win(cache_k_win, kw_s), win(cache_v_win, vw_s), conv_s)
    return (y_prompt, y_sample) + outs_p + outs_s
```

```python
import functools
import math

import numpy as np
import jax
import jax.numpy as jnp
from jax import lax
from jax.experimental import pallas as pl
from jax.experimental.pallas import tpu as pltpu

F32 = jnp.float32
BF16 = jnp.bfloat16

HEAD_DIM = 64
N_KV = 4
HPG = 4
N_HEADS = N_KV * HPG
KV_WIDTH = N_KV * HEAD_DIM
NSA_WIDTH = N_HEADS * HEAD_DIM
CMP_BLOCK = 32
CMP_STRIDE = 16
SEL_BLOCK = 64
N_SEL = 16
WINDOW = 512
CONV_WIDTH = 31
N_BUCKETS = 32
MAX_DISTANCE = 128
FORCE_SCORE = 1e6
RMS_EPS = 1e-6
LN_EPS = 1e-5
PAGE_SIZE = 128
SCALE = HEAD_DIM ** -0.5

NEG = -1e30
PICKED = -3e38
LANES = 128
VMEM_LIMIT = 56 * 1024 * 1024

TQ = 256
CPT = TQ // CMP_STRIDE
CONV_HALO = 32
SAMPLE_PAGES = 16


def _cparams(sem):
    return pltpu.CompilerParams(dimension_semantics=sem, vmem_limit_bytes=VMEM_LIMIT)


def _dot(a, b):
    return jnp.dot(a, b, preferred_element_type=F32)


def _dot_nt(a, b):
    return lax.dot_general(a, b, (((1,), (1,)), ((), ())), preferred_element_type=F32)


def _sigmoid(x):
    return 1.0 / (1.0 + jnp.exp(-x))


def _split_hi_lo(x):
    hi = x.astype(BF16)
    lo = (x - hi.astype(F32)).astype(BF16)
    return hi, lo


def _normed(x_ref, g_ref):
    x = x_ref[0]
    ms = jnp.mean(x * x, axis=-1, keepdims=True)
    return (x * lax.rsqrt(ms + RMS_EPS) * g_ref[...]).astype(BF16)


def _proj_qkv_kernel(x_ref, g_ref, w_ref, q_ref, kc_ref, vc_ref, ks_ref, vs_ref, kw_ref, vw_ref,
                     *attn_refs, attn_layouts):
    h = _normed(x_ref, g_ref)
    tm = h.shape[0]
    for g in range(N_KV):
        res = _dot(h, w_ref[:, g * KV_WIDTH:(g + 1) * KV_WIDTH]) * SCALE
        if attn_layouts:
            res_t = res.T
            for r in range(HPG):
                q_ref[0, g, r] = res_t[r * HEAD_DIM:(r + 1) * HEAD_DIM, :].astype(BF16)
        else:
            q_ref[0, :, g * KV_WIDTH:(g + 1) * KV_WIDTH] = res.astype(BF16)
    for j, o_ref in enumerate((kc_ref, vc_ref, ks_ref, vs_ref, kw_ref, vw_ref)):
        c0 = NSA_WIDTH + j * KV_WIDTH
        res = _dot(h, w_ref[:, c0:c0 + KV_WIDTH])
        o_ref[0] = res
        if attn_layouts and j in (2, 4):
            for g in range(N_KV):
                attn_refs[j // 2 - 1][0, g] = res[:, g * HEAD_DIM:(g + 1) * HEAD_DIM].astype(BF16)
        if attn_layouts and j in (3, 5):
            res_t = res.T.astype(BF16)
            for g in range(N_KV):
                for kt in range(tm // TQ):
                    attn_refs[2 + j // 2 - 1][0, g, kt] = res_t[g * HEAD_DIM:(g + 1) * HEAD_DIM, kt * TQ:(kt + 1) * TQ]


def _proj_gate_kernel(x_ref, g_ref, w_ref, sa_ref, sb_ref, gt_ref):
    h = _normed(x_ref, g_ref)
    for o_ref, base in ((sa_ref, 0), (sb_ref, NSA_WIDTH)):
        for c in range(NSA_WIDTH // 256):
            z = _dot(h, w_ref[:, base + c * 256: base + (c + 1) * 256])
            o_ref[0, :, c * 256:(c + 1) * 256] = z * _sigmoid(z)
    for c in range(2):
        z = _dot(h, w_ref[:, 2 * NSA_WIDTH + c * 256: 2 * NSA_WIDTH + (c + 1) * 256])
        gt_ref[0, :, c * 256:(c + 1) * 256] = _sigmoid(z)


def _proj_glu_kernel(x_ref, g_ref, w_ref, c_ref):
    h = _normed(x_ref, g_ref)
    n = c_ref.shape[-1]
    for c in range(n // 256):
        a = _dot(h, w_ref[:, c * 256:(c + 1) * 256])
        gg = _dot(h, w_ref[:, n + c * 256: n + (c + 1) * 256])
        c_ref[0, :, c * 256:(c + 1) * 256] = a * _sigmoid(gg)


def _projections(x, g_pre, w_qkv, w_gate, w_glu, *, tm, attn_layouts):
    b, s, d = x.shape
    grid = (b, s // tm)
    x_spec = pl.BlockSpec((1, tm, d), lambda bi, i: (bi, i, 0))
    g_spec = pl.BlockSpec((1, d), lambda bi, i: (0, 0))

    def w_spec(w):
        return pl.BlockSpec(w.shape, lambda bi, i: (0, 0))

    def row_spec(width):
        return pl.BlockSpec((1, tm, width), lambda bi, i: (bi, i, 0))

    kv_shape = jax.ShapeDtypeStruct((b, s, KV_WIDTH), F32)
    if attn_layouts:
        assert tm % TQ == 0
        q_shape = jax.ShapeDtypeStruct((b, N_KV, HPG, HEAD_DIM, s), BF16)
        q_spec = pl.BlockSpec((1, N_KV, HPG, HEAD_DIM, tm), lambda bi, i: (bi, 0, 0, 0, i))
        k_shape = jax.ShapeDtypeStruct((b, N_KV, s, HEAD_DIM), BF16)
        k_spec = pl.BlockSpec((1, N_KV, tm, HEAD_DIM), lambda bi, i: (bi, 0, i, 0))
        v_shape = jax.ShapeDtypeStruct((b, N_KV, s // TQ, HEAD_DIM, TQ), BF16)
        v_spec = pl.BlockSpec((1, N_KV, tm // TQ, HEAD_DIM, TQ), lambda bi, i: (bi, 0, i, 0, 0))
        extra_shape, extra_spec = [k_shape, k_shape, v_shape, v_shape], [k_spec, k_spec, v_spec, v_spec]
    else:
        q_shape = jax.ShapeDtypeStruct((b, s, NSA_WIDTH), BF16)
        q_spec = row_spec(NSA_WIDTH)
        extra_shape, extra_spec = [], []
    g2 = g_pre.reshape(1, d)
    qkv = pl.pallas_call(
        functools.partial(_proj_qkv_kernel, attn_layouts=attn_layouts),
        grid=grid,
        in_specs=[x_spec, g_spec, w_spec(w_qkv)],
        out_specs=[q_spec] + [row_spec(KV_WIDTH)] * 6 + extra_spec,
        out_shape=[q_shape] + [kv_shape] * 6 + extra_shape,
        compiler_params=_cparams(("parallel", "parallel")),
        name="proj_qkv",
    )(x, g2, w_qkv)
    sa, sb, gt = pl.pallas_call(
        _proj_gate_kernel,
        grid=grid,
        in_specs=[x_spec, g_spec, w_spec(w_gate)],
        out_specs=[row_spec(NSA_WIDTH), row_spec(NSA_WIDTH), row_spec(N_KV * LANES)],
        out_shape=[jax.ShapeDtypeStruct((b, s, NSA_WIDTH), F32)] * 2
        + [jax.ShapeDtypeStruct((b, s, N_KV * LANES), F32)],
        compiler_params=_cparams(("parallel", "parallel")),
        name="proj_gate",
    )(x, g2, w_gate)
    c_in = pl.pallas_call(
        _proj_glu_kernel,
        grid=grid,
        in_specs=[x_spec, g_spec, w_spec(w_glu)],
        out_specs=row_spec(w_glu.shape[1] // 2),
        out_shape=jax.ShapeDtypeStruct((b, s, w_glu.shape[1] // 2), F32),
        compiler_params=_cparams(("parallel", "parallel")),
        name="proj_glu",
    )(x, g2, w_glu)
    return qkv, sa, sb, gt, c_in


def _cmp1_rows_kernel(x_ref, pe_ref, w_ref, pre_ref, pepre_ref):
    w = w_ref[...].reshape(-1, w_ref.shape[-1])
    pre_ref[0] = _dot(x_ref[0].astype(BF16), w)
    pepre_ref[...] = _dot(pe_ref[...], w)


def _cmp1_paged_kernel(*refs, n_in):
    x_refs, (pe_ref, w_ref, pre_ref, pepre_ref, xs_ref) = refs[1:1 + n_in], refs[1 + n_in:]
    n_half = KV_WIDTH // LANES
    for p, r in enumerate(x_refs):
        x_t = r[0].T
        for hh in range(n_half):
            xs_ref[hh, p * PAGE_SIZE:(p + 1) * PAGE_SIZE, :] = x_t[:, hh * LANES:(hh + 1) * LANES]
    n_rows = n_in * PAGE_SIZE // CMP_STRIDE
    acc = jnp.zeros((n_rows, w_ref.shape[-1]), F32)
    for c in range(CMP_STRIDE):
        xc = jnp.concatenate([xs_ref[hh, pl.ds(c, n_rows, stride=CMP_STRIDE), :] for hh in range(n_half)], axis=1)
        acc = acc + _dot(xc.astype(BF16), w_ref[c])
    pre_ref[0] = acc
    pepre_ref[...] = _dot(pe_ref[...], w_ref[...].reshape(-1, w_ref.shape[-1]))


def _cmp2_kernel(pre_ref, pepre_ref, w2_ref, o_ref, *, layout):
    pre = pre_ref[0]
    n_ch = pre.shape[0]
    a = pre[:, :KV_WIDTH]
    b_next = pltpu.roll(pre[:, KV_WIDTH:], n_ch - 1, axis=0)
    pe_bias = pepre_ref[0:1, :KV_WIDTH] + pepre_ref[1:2, KV_WIDTH:]
    z = a + b_next + pe_bias
    hid = 0.5 * z * (1.0 + jnp.tanh(math.sqrt(2.0 / math.pi) * (z + 0.044715 * (z * z * z))))
    out = _dot(hid.astype(BF16), w2_ref[...])
    if layout == "group_rows":
        for g in range(N_KV):
            o_ref[0, g] = out[:, g * HEAD_DIM:(g + 1) * HEAD_DIM].astype(BF16)
    elif layout == "group_cols":
        out_t = out.T.astype(BF16)
        for g in range(N_KV):
            o_ref[0, g] = out_t[g * HEAD_DIM:(g + 1) * HEAD_DIM, :]
    elif layout == "rows":
        o_ref[0] = out.astype(BF16)
    else:
        o_ref[0] = out.T.astype(BF16)


def _cmp_weights(w1, w2, pe):
    eye = jnp.eye(N_KV, dtype=F32)
    halves = w1.reshape(2, CMP_STRIDE, HEAD_DIM, -1)
    hdim = halves.shape[-1]
    wbig = jnp.einsum('acdh,gk->cgdakh', halves, eye)
    wbig = wbig.reshape(CMP_STRIDE, KV_WIDTH, 2 * N_KV * hdim).astype(BF16)
    w2big = jnp.einsum('hd,gk->ghkd', w2, eye).reshape(N_KV * hdim, KV_WIDTH).astype(BF16)
    pe_rows = jnp.broadcast_to(pe.reshape(2, CMP_STRIDE, 1, HEAD_DIM), (2, CMP_STRIDE, N_KV, HEAD_DIM))
    pe8 = jnp.zeros((8, CMP_STRIDE * KV_WIDTH), F32).at[:2].set(pe_rows.reshape(2, -1)).astype(BF16)
    return wbig, w2big, pe8


def _compress(x_view, wts, *, page_table=None, layout):
    wbig, w2big, pe8 = wts
    ncol = wbig.shape[-1]
    kdim = wbig.shape[0] * wbig.shape[1]
    if page_table is None:
        b, n_ch, _ = x_view.shape
        rows = min(n_ch, 256)
        const2 = lambda bi, i: (0, 0)
        pre, pepre = pl.pallas_call(
            _cmp1_rows_kernel, grid=(b, n_ch // rows),
            in_specs=[pl.BlockSpec((1, rows, kdim), lambda bi, i: (bi, i, 0)), pl.BlockSpec(pe8.shape, const2),
                      pl.BlockSpec(wbig.shape, lambda bi, i: (0, 0, 0))],
            out_specs=[pl.BlockSpec((1, rows, ncol), lambda bi, i: (bi, i, 0)), pl.BlockSpec((8, ncol), const2)],
            out_shape=[jax.ShapeDtypeStruct((b, n_ch, ncol), F32), jax.ShapeDtypeStruct((8, ncol), F32)],
            compiler_params=_cparams(("arbitrary", "arbitrary")), name="cmp_stage1")(x_view, pe8, wbig)
    else:
        b, n_pages = page_table.shape
        n_in = SAMPLE_PAGES
        rows = n_in * PAGE_SIZE // CMP_STRIDE
        n_ch = n_pages * PAGE_SIZE // CMP_STRIDE
        const2 = lambda bi, i, pt: (0, 0)
        x_specs = [pl.BlockSpec((1, KV_WIDTH, PAGE_SIZE), functools.partial(
            lambda bi, i, pt, p: (pt[bi, i * SAMPLE_PAGES + p], 0, 0), p=p)) for p in range(n_in)]
        pre, pepre = pl.pallas_call(
            functools.partial(_cmp1_paged_kernel, n_in=n_in),
            grid_spec=pltpu.PrefetchScalarGridSpec(
                num_scalar_prefetch=1, grid=(b, n_pages // n_in),
                in_specs=x_specs + [pl.BlockSpec(pe8.shape, const2),
                                    pl.BlockSpec(wbig.shape, lambda bi, i, pt: (0, 0, 0))],
                out_specs=[pl.BlockSpec((1, rows, ncol), lambda bi, i, pt: (bi, i, 0)),
                           pl.BlockSpec((8, ncol), const2)],
                scratch_shapes=[pltpu.VMEM((KV_WIDTH // LANES, n_in * PAGE_SIZE, LANES), F32)]),
            out_shape=[jax.ShapeDtypeStruct((b, n_ch, ncol), F32), jax.ShapeDtypeStruct((8, ncol), F32)],
            compiler_params=_cparams(("arbitrary", "arbitrary")), name="cmp_stage1_paged",
        )(page_table, *([x_view] * n_in), pe8, wbig)
    o_dims = {"group_rows": (N_KV, n_ch, HEAD_DIM), "group_cols": (N_KV, HEAD_DIM, n_ch),
              "rows": (n_ch, KV_WIDTH), "cols": (KV_WIDTH, n_ch)}[layout]
    return pl.pallas_call(
        functools.partial(_cmp2_kernel, layout=layout),
        grid=(b,),
        in_specs=[pl.BlockSpec((1, n_ch, ncol), lambda bi: (bi, 0, 0)),
                  pl.BlockSpec((8, ncol), lambda bi: (0, 0)),
                  pl.BlockSpec(w2big.shape, lambda bi: (0, 0))],
        out_specs=pl.BlockSpec((1,) + o_dims, lambda bi: (bi,) + (0,) * len(o_dims)),
        out_shape=jax.ShapeDtypeStruct((b,) + o_dims, BF16),
        compiler_params=_cparams(("parallel",)), name="cmp_stage2",
    )(pre, pepre, w2big)


def _t5_bucket(dist):
    dist = np.maximum(np.asarray(dist, np.int64), 0)
    max_exact = N_BUCKETS // 2
    d32 = np.maximum(dist, 1).astype(np.float32)
    large = max_exact + (np.log(d32 / np.float32(max_exact)) / np.float32(math.log(MAX_DISTANCE / max_exact))
                         * np.float32(N_BUCKETS - max_exact)).astype(np.int32)
    large = np.minimum(large, N_BUCKETS - 1)
    return np.where(dist < max_exact, dist, large).astype(np.int32)


def _bias_lookup(cols, dist):
    bucket = jnp.asarray(_t5_bucket(dist))
    out = jnp.zeros(jnp.broadcast_shapes(cols.shape[1:], bucket.shape), F32)
    for k in range(N_BUCKETS):
        out = jnp.where(bucket == k, cols[k], out)
    return out


def _select_top_blocks(score, jrow, k_sel):
    chosen = jnp.zeros(score.shape, F32)
    work = score
    for _ in range(k_sel):
        m = jnp.max(work, axis=0, keepdims=True)
        idx = jnp.min(jnp.where(work == m, jrow, 1 << 20), axis=0, keepdims=True)
        pick = jrow == idx
        chosen = jnp.where(pick, 1.0, chosen)
        work = jnp.where(pick, PICKED, work)
    return chosen


def _nsa_prompt_kernel(q_ref, gt_ref, sa_ref, kcb_ref, vcbt_ref, ks_ref, vst_ref, kw_ref, vwt_ref,
                       pcd_ref, far_ref, tz0_ref, tz1_ref, covt_ref, o_ref,
                       lc_ref, selx_ref, m_ref, l_ref, acc_ref, *, n_slc):
    i = pl.program_id(2)
    nl = HPG * TQ
    ncp = kcb_ref.shape[2]
    q_t = jnp.concatenate([q_ref[0, 0, r] for r in range(HPG)], axis=1)
    lane_t = lax.broadcasted_iota(jnp.int32, (1, nl), 1) & (TQ - 1)
    qpos = i * TQ + lane_t
    far = far_ref[0, 0:1, :]

    lc_ref[0:CPT, :] = jnp.zeros((CPT, nl), F32)
    lc_ref[CPT:CPT + ncp, :] = _dot(kcb_ref[0, 0], q_t) + far
    near = pl.ds(pl.multiple_of(i * CPT, CPT), 2 * CPT)
    lc_ref[near, :] = lc_ref[near, :] + pcd_ref[0]
    lc = lc_ref[CPT:CPT + ncp, :]
    cend = lax.broadcasted_iota(jnp.int32, (ncp, 1), 0) * CMP_STRIDE + (CMP_BLOCK - 1)
    valid_c = cend <= qpos
    lm = jnp.where(valid_c, lc, NEG)
    mc = jnp.max(lm, axis=0, keepdims=True)
    ec = jnp.where(valid_c, jnp.exp(lm - mc), 0.0)
    den = jnp.sum(ec, axis=0, keepdims=True)
    p_c = ec / jnp.where(den > 0, den, 1.0)
    o_c = _dot(vcbt_ref[0, 0], p_c.astype(BF16))

    psum = p_c[:, 0:TQ]
    for r in range(1, HPG):
        psum = psum + p_c[:, r * TQ:(r + 1) * TQ]
    hi, lo = _split_hi_lo(psum)
    imp_t = _dot(covt_ref[...], hi) + _dot(covt_ref[...], lo)
    jrow = lax.broadcasted_iota(jnp.int32, (LANES, TQ), 0)
    qpos_t = qpos[:, 0:TQ]
    cur = qpos_t // SEL_BLOCK
    causal = (jrow * SEL_BLOCK <= qpos_t) & (jrow < n_slc)
    forced = (jrow == 0) | (jrow == cur) | (jrow == cur - 1)
    score = jnp.where(forced, FORCE_SCORE, jnp.where(causal, imp_t, NEG))
    chosen = _select_top_blocks(score, jrow, min(N_SEL, n_slc))
    sel_t = jnp.where(causal, chosen, 0.0)
    selx_ref[...] = jnp.concatenate([sel_t] * HPG, axis=1)

    key_u = lax.broadcasted_iota(jnp.int32, (TQ, 1), 0)
    causal_diag = key_u <= lane_t

    def init_state():
        m_ref[...] = jnp.full(m_ref.shape, NEG, F32)
        l_ref[...] = jnp.zeros(l_ref.shape, F32)
        acc_ref[...] = jnp.zeros(acc_ref.shape, F32)

    def flash_tile(k_ref, vt_ref, kt, bias, mask):
        rows = pl.ds(pl.multiple_of(kt * TQ, TQ), TQ)
        s = _dot(k_ref[0, 0, rows, :], q_t) + bias
        if mask is not None:
            s = jnp.where(mask, s, NEG)
        m_old = m_ref[...]
        m_new = jnp.maximum(m_old, jnp.max(s, axis=0, keepdims=True))
        alpha = jnp.exp(m_old - m_new)
        p = jnp.where(s > 0.5 * NEG, jnp.exp(s - m_new), 0.0)
        l_ref[...] = alpha * l_ref[...] + jnp.sum(p, axis=0, keepdims=True)
        acc_ref[...] = alpha * acc_ref[...] + _dot(vt_ref[0, 0, kt], p.astype(BF16))
        m_ref[...] = m_new

    def finish():
        l = l_ref[...]
        return acc_ref[...] / jnp.where(l > 0, l, 1.0)

    def sel_mask(kt):
        nb = TQ // SEL_BLOCK
        rows = [jnp.broadcast_to(selx_ref[pl.ds(kt * nb + jb, 1), :], (SEL_BLOCK, nl)) for jb in range(nb)]
        return jnp.concatenate(rows, axis=0) > 0.5

    init_state()

    def far_body(kt, carry):
        flash_tile(ks_ref, vst_ref, kt, far, sel_mask(kt))
        return carry

    lax.fori_loop(0, jnp.maximum(i - 1, 0), far_body, 0)

    @pl.when(i >= 1)
    def _():
        flash_tile(ks_ref, vst_ref, i - 1, tz1_ref[0], sel_mask(i - 1))

    flash_tile(ks_ref, vst_ref, i, tz0_ref[0], sel_mask(i) & causal_diag)
    o_s = finish()

    init_state()

    @pl.when(i >= 2)
    def _():
        flash_tile(kw_ref, vwt_ref, i - 2, far, key_u > lane_t)

    @pl.when(i >= 1)
    def _():
        flash_tile(kw_ref, vwt_ref, i - 1, tz1_ref[0], None)

    flash_tile(kw_ref, vwt_ref, i, tz0_ref[0], causal_diag)
    o_w = finish()

    g_t = gt_ref[0].T

    def gate_row(branch):
        return jnp.concatenate([g_t[branch * HPG + r:branch * HPG + r + 1, :] for r in range(HPG)], axis=1)

    o_t = gate_row(0) * o_c + gate_row(1) * o_s + gate_row(2) * o_w
    o_rd = jnp.concatenate([o_t[:, r * TQ:(r + 1) * TQ] for r in range(HPG)], axis=0)
    o_ref[0] = (o_rd.T * sa_ref[0]).astype(BF16)


def _nsa_prompt(q_t, gt, sa, kcb, vcb_t, ks, vs_t, kw, vw_t, rel_bias):
    b, _, _, _, s = q_t.shape
    assert WINDOW == 2 * TQ and s % TQ == 0 and TQ >= MAX_DISTANCE
    nq = s // TQ
    ncp = s // CMP_STRIDE
    n_cmp = ncp - 1
    n_slc = -(-s // SEL_BLOCK)
    assert n_slc <= LANES and ncp >= 2 * CPT
    nl = HPG * TQ
    table = rel_bias.astype(F32)
    uu = np.arange(TQ)[:, None]
    tt = np.arange(TQ)[None, :]

    def per_group(tab):
        rows = tab.shape[1]
        return jnp.transpose(tab.reshape(N_KV, HPG, rows, TQ), (0, 2, 1, 3)).reshape(N_KV, rows, nl)

    cols = table[:, :, None, None]
    tz0 = per_group(_bias_lookup(cols, (tt - uu)[None]))
    tz1 = per_group(_bias_lookup(cols, (TQ + tt - uu)[None]))
    far_h = _bias_lookup(table, np.full((N_HEADS,), MAX_DISTANCE))
    far = per_group(jnp.broadcast_to(far_h[:, None, None], (N_HEADS, 8, TQ)))
    e = np.arange(2 * CPT)[:, None] - CPT
    pcd = per_group(_bias_lookup(cols, (tt - CMP_STRIDE * e - (CMP_BLOCK - 1))[None]) - far_h[:, None, None])
    c = np.arange(ncp)[None, :]
    j = np.arange(LANES)[:, None]
    cov_t = ((c * CMP_STRIDE < (j + 1) * SEL_BLOCK) & (c * CMP_STRIDE + CMP_BLOCK > j * SEL_BLOCK)
             & (c < n_cmp) & (j < n_slc))
    cov_t = jnp.asarray(cov_t, BF16)

    per_g = lambda rows: pl.BlockSpec((1, rows, nl), lambda bi, g, i: (g, 0, 0))
    return pl.pallas_call(
        functools.partial(_nsa_prompt_kernel, n_slc=n_slc),
        grid=(b, N_KV, nq),
        in_specs=[
            pl.BlockSpec((1, 1, HPG, HEAD_DIM, TQ), lambda bi, g, i: (bi, g, 0, 0, i)),
            pl.BlockSpec((1, TQ, LANES), lambda bi, g, i: (bi, i, g)),
            pl.BlockSpec((1, TQ, KV_WIDTH), lambda bi, g, i: (bi, i, g)),
            pl.BlockSpec((1, 1, ncp, HEAD_DIM), lambda bi, g, i: (bi, g, 0, 0)),
            pl.BlockSpec((1, 1, HEAD_DIM, ncp), lambda bi, g, i: (bi, g, 0, 0)),
            pl.BlockSpec((1, 1, s, HEAD_DIM), lambda bi, g, i: (bi, g, 0, 0)),
            pl.BlockSpec((1, 1, nq, HEAD_DIM, TQ), lambda bi, g, i: (bi, g, 0, 0, 0)),
            pl.BlockSpec((1, 1, s, HEAD_DIM), lambda bi, g, i: (bi, g, 0, 0)),
            pl.BlockSpec((1, 1, nq, HEAD_DIM, TQ), lambda bi, g, i: (bi, g, 0, 0, 0)),
            per_g(2 * CPT), per_g(8), per_g(TQ), per_g(TQ),
            pl.BlockSpec(cov_t.shape, lambda bi, g, i: (0, 0)),
        ],
        out_specs=pl.BlockSpec((1, TQ, KV_WIDTH), lambda bi, g, i: (bi, i, g)),
        out_shape=jax.ShapeDtypeStruct((b, s, NSA_WIDTH), BF16),
        scratch_shapes=[pltpu.VMEM((CPT + ncp, nl), F32), pltpu.VMEM((LANES, nl), F32),
                        pltpu.VMEM((1, nl), F32), pltpu.VMEM((1, nl), F32), pltpu.VMEM((HEAD_DIM, nl), F32)],
        compiler_params=_cparams(("parallel", "parallel", "arbitrary")),
        name="nsa_prompt",
    )(q_t, gt, sa, kcb, vcb_t, ks, vs_t, kw, vw_t, pcd, far, tz0, tz1, cov_t)


def _nsa_sample_kernel(*refs, n_pg, n_chunks, n_slc, past, t_new):
    (qbd_ref, kcbt_ref, vcb_ref, bc_ref, cov_ref, rmat_ref) = refs[1:7]
    kpg = refs[7:7 + n_pg]
    vpg = refs[7 + n_pg:7 + 2 * n_pg]
    (bsl_ref, ksn_ref, vsn_ref, bsn_ref, kwc_ref, vwc_ref, kwn_ref, vwn_ref, bw_ref, gt_ref, e4_ref,
     o_ref, sel_ref, m_ref, l_ref, acc_ref, oc_ref) = refs[7 + 2 * n_pg:]
    j = pl.program_id(1)
    qbd = qbd_ref[0]
    nsp = cov_ref.shape[1]
    n_win = kwc_ref.shape[2]

    def softmax_rows(s):
        m = jnp.max(s, axis=-1, keepdims=True)
        e = jnp.where(s > 0.5 * NEG, jnp.exp(s - m), 0.0)
        den = jnp.sum(e, axis=-1, keepdims=True)
        return e / jnp.where(den > 0, den, 1.0)

    def flash_step(s, v, v_is_transposed):
        m_old = m_ref[...]
        m_new = jnp.maximum(m_old, jnp.max(s, axis=-1, keepdims=True))
        alpha = jnp.exp(m_old - m_new)
        p = jnp.where(s > 0.5 * NEG, jnp.exp(s - m_new), 0.0)
        l_ref[...] = alpha * l_ref[...] + jnp.sum(p, axis=-1, keepdims=True)
        pv = _dot_nt(p.astype(BF16), v) if v_is_transposed else _dot(p.astype(BF16), v)
        acc_ref[...] = alpha * acc_ref[...] + pv
        m_ref[...] = m_new

    @pl.when(j == 0)
    def _():
        p_c = softmax_rows(_dot(qbd, kcbt_ref[0]) + bc_ref[...])
        oc_ref[...] = _dot(p_c.astype(BF16), vcb_ref[0])
        hi, lo = _split_hi_lo(p_c)
        psum = _dot(rmat_ref[...], hi) + _dot(rmat_ref[...], lo)
        hi, lo = _split_hi_lo(psum)
        imp_t = (_dot(hi, cov_ref[...]) + _dot(lo, cov_ref[...])).T
        jrow = lax.broadcasted_iota(jnp.int32, (nsp, LANES), 0)
        lane = lax.broadcasted_iota(jnp.int32, (1, LANES), 1)
        qpos = past + lane % t_new
        cur = qpos // SEL_BLOCK
        causal = (jrow * SEL_BLOCK <= qpos) & (jrow < n_slc)
        forced = (jrow == 0) | (jrow == cur) | (jrow == cur - 1)
        score = jnp.where(forced, FORCE_SCORE, jnp.where(causal, imp_t, NEG))
        chosen = _select_top_blocks(score, jrow, min(N_SEL, n_slc))
        sel = jnp.where(causal & (lane < N_HEADS * t_new), chosen, 0.0).T
        for k in range(nsp // LANES):
            sel_ref[k] = sel[:, k * LANES:(k + 1) * LANES]
        m_ref[...] = jnp.full(m_ref.shape, NEG, F32)
        l_ref[...] = jnp.zeros(l_ref.shape, F32)
        acc_ref[...] = jnp.zeros(acc_ref.shape, F32)

    k_t = jnp.concatenate([r[0] for r in kpg], axis=1).astype(BF16)
    v_t = jnp.concatenate([r[0] for r in vpg], axis=1).astype(BF16)
    far = bsl_ref[:, PAGE_SIZE:2 * PAGE_SIZE]
    last = jnp.where(j == n_chunks - 1, bsl_ref[:, 0:PAGE_SIZE], far)
    bias = jnp.concatenate([far] * (n_pg - 1) + [last], axis=1)
    chunks_per_tile = LANES // (n_pg * (PAGE_SIZE // SEL_BLOCK))
    mask = _dot(sel_ref[j // chunks_per_tile].astype(BF16), e4_ref[j % chunks_per_tile]) > 0.5
    flash_step(jnp.where(mask, _dot(qbd, k_t) + bias, NEG), v_t, True)

    @pl.when(j == n_chunks - 1)
    def _():
        sn = _dot_nt(qbd, ksn_ref[0].astype(BF16)) + bsn_ref[...]
        blk = n_slc - 1
        seln = sel_ref[blk // LANES][:, blk % LANES:blk % LANES + 1]
        flash_step(jnp.where(seln > 0.5, sn, NEG), vsn_ref[0].astype(BF16), False)
        l = l_ref[...]
        o_s = acc_ref[...] / jnp.where(l > 0, l, 1.0)
        sw = jnp.concatenate([_dot(qbd, kwc_ref[0].astype(BF16)), _dot_nt(qbd, kwn_ref[0].astype(BF16))], axis=1)
        p_w = softmax_rows(sw + bw_ref[...]).astype(BF16)
        o_w = _dot_nt(p_w[:, :n_win], vwc_ref[0].astype(BF16)) + _dot(p_w[:, n_win:], vwn_ref[0].astype(BF16))
        gt = gt_ref[0]
        o_ref[0] = gt[:, 0:1] * oc_ref[...] + gt[:, 1:2] * o_s + gt[:, 2:3] * o_w


def _nsa_sample(q, gates, kcb_t, vcb, cache_k_slc, cache_v_slc, page_table, ks_new, vs_new,
                kw_cache, vw_cache, kw_new, vw_new, rel_bias, *, past):
    db, t_new, _ = q.shape
    n_pages = page_table.shape[1]
    assert past == n_pages * PAGE_SIZE and past % SEL_BLOCK == 0 and t_new <= SEL_BLOCK
    assert PAGE_SIZE >= MAX_DISTANCE
    n_cp = kcb_t.shape[2]
    n_slc = -(-(past + t_new) // SEL_BLOCK)
    nsp = -(-n_slc // LANES) * LANES
    n_pg = SAMPLE_PAGES
    n_chunks = n_pages // n_pg
    blocks_per_chunk = n_pg * (PAGE_SIZE // SEL_BLOCK)
    assert LANES % blocks_per_chunk == 0
    n_win = kw_cache.shape[2]
    nl = N_HEADS * t_new
    assert nl <= LANES
    row = np.arange(LANES)
    row_ok = (row < nl)[:, None]
    row_g = np.where(row < nl, row // (HPG * t_new), 0)
    row_t = (row % t_new)[:, None]
    q5 = q.reshape(db, t_new, N_KV, HPG, HEAD_DIM)
    qbd = jnp.einsum('btgrd,gk->bgrtkd', q5.astype(F32), jnp.eye(N_KV, dtype=F32)).reshape(db, nl, KV_WIDTH)
    qbd = jnp.pad(qbd, ((0, 0), (0, LANES - nl), (0, 0))).astype(BF16)

    tab_rows = jnp.pad(jnp.repeat(rel_bias.astype(F32), t_new, axis=1), ((0, 0), (0, LANES - nl)))[:, :, None]

    def bias_tab(dist, valid):
        return jnp.where(jnp.asarray(valid & row_ok), _bias_lookup(tab_rows, dist), NEG)

    qpos = past + row_t
    cblk = np.arange(n_cp)[None, :]
    dist_c = qpos - (cblk * CMP_STRIDE + CMP_BLOCK - 1)
    bc = bias_tab(dist_c, (dist_c >= 0) & (cblk < n_cp - 1))
    kpos = past - PAGE_SIZE + np.arange(PAGE_SIZE)[None, :]
    bsl = jnp.concatenate([bias_tab(qpos - kpos, np.ones((LANES, PAGE_SIZE), bool)),
                           bias_tab(np.full((LANES, PAGE_SIZE), MAX_DISTANCE), np.ones((LANES, PAGE_SIZE), bool))],
                          axis=1)
    u = np.arange(PAGE_SIZE)[None, :]
    new_ok = (u <= row_t) & (u < t_new)
    bsn = bias_tab(row_t - u, new_ok)
    wpos = past - n_win + np.arange(n_win)[None, :]
    dist_w = qpos - wpos
    bw = jnp.concatenate([bias_tab(dist_w, (dist_w >= 0) & (dist_w < WINDOW) & (wpos >= 0)),
                          bias_tab(row_t - u, new_ok & (row_t - u < WINDOW))], axis=1)
    c = np.arange(n_cp)[:, None]
    jb = np.arange(nsp)[None, :]
    cov = ((c * CMP_STRIDE < (jb + 1) * SEL_BLOCK) & (c * CMP_STRIDE + CMP_BLOCK > jb * SEL_BLOCK)
           & (c < n_cp - 1) & (jb < n_slc))
    cov = jnp.asarray(cov, BF16)
    same = (row_g[:, None] == row_g[None, :]) & (row_t == row_t.T) & row_ok & row_ok.T
    rmat = jnp.asarray(same, BF16)
    kk = np.arange(n_pg * PAGE_SIZE)[None, None, :] // SEL_BLOCK
    e4 = np.arange(LANES)[None, :, None] == (np.arange(LANES // blocks_per_chunk)[:, None, None] * blocks_per_chunk + kk)
    e4 = jnp.asarray(e4, BF16)
    g5 = gates.reshape(db, t_new, N_KV, LANES)[..., :3 * HPG].reshape(db, t_new, N_KV, 3, HPG)
    gcol = jnp.transpose(g5, (0, 2, 4, 1, 3)).reshape(db, nl, 3)
    gcol = jnp.pad(gcol, ((0, 0), (0, LANES - nl), (0, 5)))

    def pad_new(a):
        return jnp.pad(a, ((0, 0), (0, PAGE_SIZE - t_new), (0, 0)))

    ksn, vsn, kwn, vwn = (pad_new(a) for a in (ks_new, vs_new, kw_new, vw_new))

    per_b = lambda shape: pl.BlockSpec((1,) + shape, lambda b, jc, pt: (b, 0, 0))
    full = lambda a: pl.BlockSpec(a.shape, lambda b, jc, pt: (0,) * a.ndim)
    page = lambda p: pl.BlockSpec((1, KV_WIDTH, PAGE_SIZE), lambda b, jc, pt: (pt[b, jc * SAMPLE_PAGES + p], 0, 0))
    in_specs = ([per_b((LANES, KV_WIDTH)), per_b((KV_WIDTH, n_cp)), per_b((n_cp, KV_WIDTH)),
                 full(bc), full(cov), full(rmat)]
                + [page(p) for p in range(n_pg)] * 2
                + [full(bsl), per_b((PAGE_SIZE, KV_WIDTH)), per_b((PAGE_SIZE, KV_WIDTH)), full(bsn),
                   per_b((KV_WIDTH, n_win)), per_b((KV_WIDTH, n_win)),
                   per_b((PAGE_SIZE, KV_WIDTH)), per_b((PAGE_SIZE, KV_WIDTH)), full(bw), per_b((LANES, 8)), full(e4)])
    o = pl.pallas_call(
        functools.partial(_nsa_sample_kernel, n_pg=n_pg, n_chunks=n_chunks, n_slc=n_slc, past=past, t_new=t_new),
        grid_spec=pltpu.PrefetchScalarGridSpec(
            num_scalar_prefetch=1, grid=(db, n_chunks), in_specs=in_specs,
            out_specs=pl.BlockSpec((1, LANES, KV_WIDTH), lambda b, jc, pt: (b, 0, 0)),
            scratch_shapes=[pltpu.VMEM((nsp // LANES, LANES, LANES), F32), pltpu.VMEM((LANES, 1), F32),
                            pltpu.VMEM((LANES, 1), F32), pltpu.VMEM((LANES, KV_WIDTH), F32),
                            pltpu.VMEM((LANES, KV_WIDTH), F32)]),
        out_shape=jax.ShapeDtypeStruct((db, LANES, KV_WIDTH), F32),
        compiler_params=_cparams(("arbitrary", "arbitrary")),
        name="nsa_sample",
    )(page_table, qbd, kcb_t, vcb, bc, cov, rmat, *([cache_k_slc] * n_pg), *([cache_v_slc] * n_pg),
      bsl, ksn, vsn, bsn, kw_cache, vw_cache, kwn, vwn, bw, gcol, e4)
    o6 = o[:, :nl].reshape(db, N_KV, HPG, t_new, N_KV, HEAD_DIM)
    o_diag = jnp.stack([o6[:, g, :, :, g] for g in range(N_KV)], axis=1)
    return jnp.transpose(o_diag, (0, 3, 1, 2, 4)).reshape(db, t_new, NSA_WIDTH)


def _conv_tail(y, cb_ref, lg_ref, lb_ref, wpw_ref, bpw_ref, sb):
    y = y + cb_ref[...]
    mu = jnp.mean(y, axis=-1, keepdims=True)
    yc = y - mu
    var = jnp.mean(yc * yc, axis=-1, keepdims=True)
    yn = yc * lax.rsqrt(var + LN_EPS) * lg_ref[...] + lb_ref[...]
    act = yn * _sigmoid(yn)
    return ((_dot(act.astype(BF16), wpw_ref[...]) + bpw_ref[...]) * sb).astype(BF16)


def _conv_prompt_kernel(c_ref, halo_ref, init_ref, cw_ref, cb_ref, lg_ref, lb_ref, wpw_ref, bpw_ref, sb_ref,
                        o_ref, full_ref, y_ref, *, ts):
    j = pl.program_id(1)
    full_ref[CONV_HALO:CONV_HALO + ts, :] = c_ref[0]

    @pl.when(j == 0)
    def _():
        full_ref[0:CONV_HALO, :] = init_ref[0]

    @pl.when(j > 0)
    def _():
        full_ref[0:CONV_HALO, :] = halo_ref[0]

    first = CONV_HALO - (CONV_WIDTH - 1)
    rb = 64
    ch = full_ref.shape[1]
    for c0 in range(0, ch, LANES):
        for r0 in range(0, ts, rb):
            acc = jnp.zeros((rb, LANES), F32)
            for w in range(CONV_WIDTH):
                acc = acc + full_ref[r0 + first + w:r0 + first + w + rb, c0:c0 + LANES] * cw_ref[w:w + 1, c0:c0 + LANES]
            y_ref[r0:r0 + rb, c0:c0 + LANES] = acc
    o_ref[0] = _conv_tail(y_ref[...], cb_ref, lg_ref, lb_ref, wpw_ref, bpw_ref, sb_ref[0])


def _conv_sample_kernel(c_ref, st_ref, cw_ref, cb_ref, lg_ref, lb_ref, wpw_ref, bpw_ref, sb_ref,
                        o_ref, full_ref, y_ref):
    nb, t_new, _ = c_ref.shape
    n_st = st_ref.shape[1]
    full_ref[:, 0:n_st, :] = st_ref[...]
    full_ref[:, n_st:n_st + t_new, :] = c_ref[...]
    first = n_st - (CONV_WIDTH - 1)
    for b in range(nb):
        acc = jnp.zeros((t_new, full_ref.shape[2]), F32)
        for w in range(CONV_WIDTH):
            acc = acc + full_ref[b, first + w:first + w + t_new, :] * cw_ref[w:w + 1, :]
        y_ref[b * t_new:(b + 1) * t_new, :] = acc
    o_ref[...] = _conv_tail(y_ref[...], cb_ref, lg_ref, lb_ref, wpw_ref, bpw_ref, sb_ref[...])


def _conv_params(conv_w, conv_b, ln_g, ln_b, w_pw, b_pw):
    ch = conv_w.shape[1]
    cw = jnp.pad(conv_w, ((0, 32 - CONV_WIDTH), (0, 0)))
    return (cw, conv_b.reshape(1, ch), ln_g.reshape(1, ch), ln_b.reshape(1, ch), w_pw.astype(BF16),
            b_pw.reshape(1, ch))


def _conv_prompt(c_in, init, params, sb, *, ts):
    b, s, ch = c_in.shape
    hb = ts // CONV_HALO
    const = lambda a: pl.BlockSpec(a.shape, lambda bi, j: (0,) * a.ndim)
    return pl.pallas_call(
        functools.partial(_conv_prompt_kernel, ts=ts),
        grid=(b, s // ts),
        in_specs=[pl.BlockSpec((1, ts, ch), lambda bi, j: (bi, j, 0)),
                  pl.BlockSpec((1, CONV_HALO, ch), lambda bi, j: (bi, jnp.maximum(j * hb - 1, 0), 0)),
                  pl.BlockSpec((1, CONV_HALO, ch), lambda bi, j: (bi, 0, 0))]
        + [const(a) for a in params]
        + [pl.BlockSpec((1, ts, ch), lambda bi, j: (bi, j, 0))],
        out_specs=pl.BlockSpec((1, ts, ch), lambda bi, j: (bi, j, 0)),
        out_shape=jax.ShapeDtypeStruct((b, s, ch), BF16),
        scratch_shapes=[pltpu.VMEM((CONV_HALO + ts, ch), F32), pltpu.VMEM((ts, ch), F32)],
        compiler_params=_cparams(("parallel", "arbitrary")),
        name="conv_prompt",
    )(c_in, c_in, init, *params, sb)


def _conv_sample(c_in, state, params, sb):
    db, t_new, ch = c_in.shape
    n_st = state.shape[1]
    rows_pad = -(-(n_st + t_new) // 8) * 8
    return pl.pallas_call(
        _conv_sample_kernel,
        out_shape=jax.ShapeDtypeStruct((db * t_new, ch), BF16),
        scratch_shapes=[pltpu.VMEM((db, rows_pad, ch), F32), pltpu.VMEM((db * t_new, ch), F32)],
        compiler_params=pltpu.CompilerParams(vmem_limit_bytes=VMEM_LIMIT),
        name="conv_sample",
    )(c_in, state, *params, sb)


def _out_kernel(x_ref, ma_ref, mb_ref, *rest, gated):
    if gated:
        sa_ref, wa_ref, wb_ref, gp_ref, y_ref = rest
        ma = (ma_ref[0] * sa_ref[0]).astype(BF16)
    else:
        wa_ref, wb_ref, gp_ref, y_ref = rest
        ma = ma_ref[0]
    z = _dot(ma, wa_ref[...]) + _dot(mb_ref[0], wb_ref[...])
    ms = jnp.mean(z * z, axis=-1, keepdims=True)
    y_ref[0] = x_ref[0] + z * lax.rsqrt(ms + RMS_EPS) * gp_ref[...]


def _out_proj(x, ma, mb, sa, w_out, g_post, *, tm):
    b, s, d = x.shape
    na = ma.shape[-1]
    wa = w_out[:na].astype(BF16)
    wb = w_out[na:].astype(BF16)
    row = lambda width: pl.BlockSpec((1, tm, width), lambda bi, i: (bi, i, 0))
    const = lambda a: pl.BlockSpec(a.shape, lambda bi, i: (0, 0))
    gp = g_post.reshape(1, d)
    gate_in, gate_spec = ([sa], [row(na)]) if sa is not None else ([], [])
    return pl.pallas_call(
        functools.partial(_out_kernel, gated=sa is not None),
        grid=(b, s // tm),
        in_specs=[row(d), row(na), row(mb.shape[-1])] + gate_spec + [const(wa), const(wb), const(gp)],
        out_specs=row(d),
        out_shape=jax.ShapeDtypeStruct((b, s, d), F32),
        compiler_params=_cparams(("parallel", "parallel")),
        name="out_proj",
    )(x, ma, mb, *gate_in, wa, wb, gp)


def _split_w_in(w_in):
    d = w_in.shape[0]
    c0 = NSA_WIDTH + 6 * KV_WIDTH
    n_gate = 3 * N_HEADS
    conv_ch = (w_in.shape[1] - c0 - n_gate - NSA_WIDTH) // 3
    w_qkv = w_in[:, :c0].astype(BF16)
    wg = w_in[:, c0:c0 + n_gate].reshape(d, N_KV, HPG, 3)
    wg = jnp.transpose(wg, (0, 1, 3, 2)).reshape(d, N_KV, 3 * HPG)
    wg = jnp.pad(wg, ((0, 0), (0, 0), (0, LANES - 3 * HPG))).reshape(d, N_KV * LANES)
    z_a = w_in[:, c0 + n_gate:c0 + n_gate + NSA_WIDTH]
    glu0 = c0 + n_gate + NSA_WIDTH
    w_glu = w_in[:, glu0:glu0 + 2 * conv_ch].astype(BF16)
    z_b = w_in[:, glu0 + 2 * conv_ch:]
    assert conv_ch == NSA_WIDTH
    w_gate = jnp.concatenate([z_a, z_b, wg], axis=1).astype(BF16)
    return w_qkv, w_gate, w_glu


def _token_minor(cache):
    n, tokens = cache.shape[:2]
    return jnp.transpose(cache, (0, 2, 3, 1)).reshape(n, KV_WIDTH, tokens)


def kernel(x_prompt, x_sample, cache_k_cmp, cache_v_cmp, cache_k_slc, cache_v_slc, cache_k_win, cache_v_win,
           state_conv, page_table, g_pre, w_in, cmp_w1_k, cmp_w2_k, cmp_pe_k, cmp_w1_v, cmp_w2_v, cmp_pe_v,
           rel_bias, conv_w, conv_b, ln_g, ln_b, w_pw, b_pw, w_out, g_post):
    depth = g_pre.shape[0]
    assert depth == 1
    layer = 0
    b, s, d = x_prompt.shape
    db, t_new, _ = x_sample.shape
    past = page_table.shape[1] * PAGE_SIZE
    conv_ch = conv_w.shape[-1]

    w_qkv, w_gate, w_glu = _split_w_in(w_in[layer])
    wts_k = _cmp_weights(cmp_w1_k[layer], cmp_w2_k[layer], cmp_pe_k[layer])
    wts_v = _cmp_weights(cmp_w1_v[layer], cmp_w2_v[layer], cmp_pe_v[layer])
    cparams = _conv_params(conv_w[layer], conv_b[layer], ln_g[layer], ln_b[layer], w_pw[layer], b_pw[layer])
    chunk_w = CMP_STRIDE * KV_WIDTH

    (q_t, kc, vc, ks, vs, kw, vw, ks_g, kw_g, vs_t, vw_t), sa, sb, gt, c_in = _projections(
        x_prompt, g_pre[layer], w_qkv, w_gate, w_glu, tm=512, attn_layouts=True)
    n_ch = s // CMP_STRIDE
    kcb = _compress(kc[:, :n_ch * CMP_STRIDE].reshape(b, n_ch, chunk_w), wts_k, layout="group_rows")
    vcb_t = _compress(vc[:, :n_ch * CMP_STRIDE].reshape(b, n_ch, chunk_w), wts_v, layout="group_cols")
    ma = _nsa_prompt(q_t, gt, sa, kcb, vcb_t, ks_g, vs_t, kw_g, vw_t, rel_bias)
    mb = _conv_prompt(c_in, jnp.zeros((b, CONV_HALO, conv_ch), F32), cparams, sb, ts=256)
    y_prompt = _out_proj(x_prompt, ma, mb, None, w_out[layer], g_post[layer], tm=512)
    n_keep = min(WINDOW, s)
    kv5 = lambda a: a.reshape(1, a.shape[0], a.shape[1], N_KV, HEAD_DIM)
    outs_p = (kv5(kc), kv5(vc), kv5(ks), kv5(vs), kv5(kw[:, -n_keep:]), kv5(vw[:, -n_keep:]),
              c_in[None, :, -(CONV_WIDTH - 1):])

    xs = x_sample.reshape(1, db * t_new, d)
    (q_s, kc_s, vc_s, ks_s, vs_s, kw_s, vw_s), sa_s, sb_s, gt_s, c_s = _projections(
        xs, g_pre[layer], w_qkv, w_gate, w_glu, tm=db * t_new, attn_layouts=False)
    tok = lambda a: a.reshape(db, t_new, a.shape[-1])
    kcb_s = _compress(_token_minor(cache_k_cmp[layer]), wts_k, page_table=page_table, layout="cols")
    vcb_s = _compress(_token_minor(cache_v_cmp[layer]), wts_v, page_table=page_table, layout="rows")
    o_a = _nsa_sample(tok(q_s), tok(gt_s), kcb_s, vcb_s,
                      _token_minor(cache_k_slc[layer]), _token_minor(cache_v_slc[layer]), page_table,
                      tok(ks_s), tok(vs_s), _token_minor(cache_k_win[layer]), _token_minor(cache_v_win[layer]),
                      tok(kw_s), tok(vw_s), rel_bias, past=past)
    mb_s = _conv_sample(tok(c_s), state_conv[layer], cparams, sb_s[0])
    y_sample = _out_proj(xs, o_a.reshape(1, db * t_new, NSA_WIDTH), mb_s[None], sa_s, w_out[layer],
                         g_post[layer], tm=db * t_new).reshape(db, t_new, d)
    n_keep_s = min(WINDOW, past + t_new)
    kv5s = lambda a: a.reshape(1, db, t_new, N_KV, HEAD_DIM)
    win = lambda cache, new: jnp.concatenate(
        [cache[layer], new.reshape(db, t_new, N_KV, HEAD_DIM)], axis=1)[None, :, -n_keep_s:]
    conv_s = jnp.concatenate([state_conv[layer], tok(c_s)], axis=1)[None, :, -(CONV_WIDTH - 1):]
    outs_s = (kv5s(kc_s), kv5s(vc_s), kv5s(ks_s), kv5s(vs_s), win(cache_k_win, kw_s), win(cache_v_win, vw_s), conv_s)
    return (y_prompt, y_sample) + outs_p + outs_s
```

```python
import functools
import math

import numpy as np
import jax
import jax.numpy as jnp
from jax import lax
from jax.experimental import pallas as pl
from jax.experimental.pallas import tpu as pltpu

F32 = jnp.float32
BF16 = jnp.bfloat16

HEAD_DIM = 64
N_KV = 4
HPG = 4
N_HEADS = N_KV * HPG
KV_WIDTH = N_KV * HEAD_DIM
NSA_WIDTH = N_HEADS * HEAD_DIM
CMP_BLOCK = 32
CMP_STRIDE = 16
SEL_BLOCK = 64
N_SEL = 16
WINDOW = 512
CONV_WIDTH = 31
N_BUCKETS = 32
MAX_DISTANCE = 128
FORCE_SCORE = 1e6
RMS_EPS = 1e-6
LN_EPS = 1e-5
PAGE_SIZE = 128
SCALE = HEAD_DIM ** -0.5

NEG = -1e30
PICKED = -3e38
LANES = 128
VMEM_LIMIT = 56 * 1024 * 1024

TQ = 256
CPT = TQ // CMP_STRIDE
CONV_HALO = 32
SAMPLE_PAGES = 32
CMP_PAGES = 32


def _cparams(sem):
    return pltpu.CompilerParams(dimension_semantics=sem, vmem_limit_bytes=VMEM_LIMIT)


def _dot(a, b):
    return jnp.dot(a, b, preferred_element_type=F32)


def _dot_nt(a, b):
    return lax.dot_general(a, b, (((1,), (1,)), ((), ())), preferred_element_type=F32)


def _sigmoid(x):
    return 1.0 / (1.0 + jnp.exp(-x))


def _split_hi_lo(x):
    hi = x.astype(BF16)
    lo = (x - hi.astype(F32)).astype(BF16)
    return hi, lo


def _normed(x_ref, g_ref):
    x = x_ref[0]
    ms = jnp.mean(x * x, axis=-1, keepdims=True)
    return (x * lax.rsqrt(ms + RMS_EPS) * g_ref[...]).astype(BF16)


def _proj_qkv_kernel(x_ref, g_ref, w_ref, q_ref, kc_ref, vc_ref, ks_ref, vs_ref, kw_ref, vw_ref,
                     *attn_refs, attn_layouts):
    h = _normed(x_ref, g_ref)
    tm = h.shape[0]
    for g in range(N_KV):
        res = _dot(h, w_ref[:, g * KV_WIDTH:(g + 1) * KV_WIDTH]) * SCALE
        if attn_layouts:
            res_t = res.T
            for r in range(HPG):
                q_ref[0, g, r] = res_t[r * HEAD_DIM:(r + 1) * HEAD_DIM, :].astype(BF16)
        else:
            q_ref[0, :, g * KV_WIDTH:(g + 1) * KV_WIDTH] = res.astype(BF16)
    for j, o_ref in enumerate((kc_ref, vc_ref, ks_ref, vs_ref, kw_ref, vw_ref)):
        c0 = NSA_WIDTH + j * KV_WIDTH
        res = _dot(h, w_ref[:, c0:c0 + KV_WIDTH])
        o_ref[0] = res
        if attn_layouts and j in (2, 4):
            for g in range(N_KV):
                attn_refs[j // 2 - 1][0, g] = res[:, g * HEAD_DIM:(g + 1) * HEAD_DIM].astype(BF16)
        if attn_layouts and j in (3, 5):
            res_t = res.T.astype(BF16)
            for g in range(N_KV):
                for kt in range(tm // TQ):
                    attn_refs[2 + j // 2 - 1][0, g, kt] = res_t[g * HEAD_DIM:(g + 1) * HEAD_DIM, kt * TQ:(kt + 1) * TQ]


def _proj_gate_kernel(x_ref, g_ref, w_ref, sa_ref, sb_ref, gt_ref):
    h = _normed(x_ref, g_ref)
    for o_ref, base in ((sa_ref, 0), (sb_ref, NSA_WIDTH)):
        for c in range(NSA_WIDTH // 256):
            z = _dot(h, w_ref[:, base + c * 256: base + (c + 1) * 256])
            o_ref[0, :, c * 256:(c + 1) * 256] = z * _sigmoid(z)
    for c in range(2):
        z = _dot(h, w_ref[:, 2 * NSA_WIDTH + c * 256: 2 * NSA_WIDTH + (c + 1) * 256])
        gt_ref[0, :, c * 256:(c + 1) * 256] = _sigmoid(z)


def _proj_glu_kernel(x_ref, g_ref, w_ref, c_ref):
    h = _normed(x_ref, g_ref)
    n = c_ref.shape[-1]
    for c in range(n // 256):
        a = _dot(h, w_ref[:, c * 256:(c + 1) * 256])
        gg = _dot(h, w_ref[:, n + c * 256: n + (c + 1) * 256])
        c_ref[0, :, c * 256:(c + 1) * 256] = a * _sigmoid(gg)


def _projections(x, g_pre, w_qkv, w_gate, w_glu, *, tm, attn_layouts):
    b, s, d = x.shape
    grid = (b, s // tm)
    x_spec = pl.BlockSpec((1, tm, d), lambda bi, i: (bi, i, 0))
    g_spec = pl.BlockSpec((1, d), lambda bi, i: (0, 0))

    def w_spec(w):
        return pl.BlockSpec(w.shape, lambda bi, i: (0, 0))

    def row_spec(width):
        return pl.BlockSpec((1, tm, width), lambda bi, i: (bi, i, 0))

    kv_shape = jax.ShapeDtypeStruct((b, s, KV_WIDTH), F32)
    if attn_layouts:
        assert tm % TQ == 0
        q_shape = jax.ShapeDtypeStruct((b, N_KV, HPG, HEAD_DIM, s), BF16)
        q_spec = pl.BlockSpec((1, N_KV, HPG, HEAD_DIM, tm), lambda bi, i: (bi, 0, 0, 0, i))
        k_shape = jax.ShapeDtypeStruct((b, N_KV, s, HEAD_DIM), BF16)
        k_spec = pl.BlockSpec((1, N_KV, tm, HEAD_DIM), lambda bi, i: (bi, 0, i, 0))
        v_shape = jax.ShapeDtypeStruct((b, N_KV, s // TQ, HEAD_DIM, TQ), BF16)
        v_spec = pl.BlockSpec((1, N_KV, tm // TQ, HEAD_DIM, TQ), lambda bi, i: (bi, 0, i, 0, 0))
        extra_shape, extra_spec = [k_shape, k_shape, v_shape, v_shape], [k_spec, k_spec, v_spec, v_spec]
    else:
        q_shape = jax.ShapeDtypeStruct((b, s, NSA_WIDTH), BF16)
        q_spec = row_spec(NSA_WIDTH)
        extra_shape, extra_spec = [], []
    g2 = g_pre.reshape(1, d)
    qkv = pl.pallas_call(
        functools.partial(_proj_qkv_kernel, attn_layouts=attn_layouts),
        grid=grid,
        in_specs=[x_spec, g_spec, w_spec(w_qkv)],
        out_specs=[q_spec] + [row_spec(KV_WIDTH)] * 6 + extra_spec,
        out_shape=[q_shape] + [kv_shape] * 6 + extra_shape,
        compiler_params=_cparams(("parallel", "parallel")),
        name="proj_qkv",
    )(x, g2, w_qkv)
    sa, sb, gt = pl.pallas_call(
        _proj_gate_kernel,
        grid=grid,
        in_specs=[x_spec, g_spec, w_spec(w_gate)],
        out_specs=[row_spec(NSA_WIDTH), row_spec(NSA_WIDTH), row_spec(N_KV * LANES)],
        out_shape=[jax.ShapeDtypeStruct((b, s, NSA_WIDTH), F32)] * 2
        + [jax.ShapeDtypeStruct((b, s, N_KV * LANES), F32)],
        compiler_params=_cparams(("parallel", "parallel")),
        name="proj_gate",
    )(x, g2, w_gate)
    c_in = pl.pallas_call(
        _proj_glu_kernel,
        grid=grid,
        in_specs=[x_spec, g_spec, w_spec(w_glu)],
        out_specs=row_spec(w_glu.shape[1] // 2),
        out_shape=jax.ShapeDtypeStruct((b, s, w_glu.shape[1] // 2), F32),
        compiler_params=_cparams(("parallel", "parallel")),
        name="proj_glu",
    )(x, g2, w_glu)
    return qkv, sa, sb, gt, c_in


def _cmp1_rows_kernel(x_ref, pe_ref, w_ref, pre_ref, pepre_ref):
    w = w_ref[...].reshape(-1, w_ref.shape[-1])
    pre_ref[0] = _dot(x_ref[0].astype(BF16), w)
    pepre_ref[...] = _dot(pe_ref[...], w)


def _cmp1_paged_kernel(*refs, n_in):
    x_refs, (perm_ref, pe_ref, w_ref, pre_ref, pepre_ref) = refs[1:1 + n_in], refs[1 + n_in:]
    perm = perm_ref[...]
    zs = [_dot_nt(perm, r[0].astype(BF16)) for r in x_refs]
    rpp = PAGE_SIZE // CMP_STRIDE
    acc = jnp.zeros((n_in * rpp, w_ref.shape[-1]), F32)
    for c in range(CMP_STRIDE):
        xc = jnp.concatenate([z[c * rpp:(c + 1) * rpp] for z in zs], axis=0)
        acc = acc + _dot(xc.astype(BF16), w_ref[c])
    pre_ref[0] = acc
    pepre_ref[...] = _dot(pe_ref[...], w_ref[...].reshape(-1, w_ref.shape[-1]))


def _cmp2_kernel(pre_ref, pepre_ref, w2_ref, o_ref, *, layout):
    pre = pre_ref[0]
    n_ch = pre.shape[0]
    a = pre[:, :KV_WIDTH]
    b_next = pltpu.roll(pre[:, KV_WIDTH:], n_ch - 1, axis=0)
    pe_bias = pepre_ref[0:1, :KV_WIDTH] + pepre_ref[1:2, KV_WIDTH:]
    z = a + b_next + pe_bias
    hid = 0.5 * z * (1.0 + jnp.tanh(math.sqrt(2.0 / math.pi) * (z + 0.044715 * (z * z * z))))
    out = _dot(hid.astype(BF16), w2_ref[...])
    if layout == "group_rows":
        for g in range(N_KV):
            o_ref[0, g] = out[:, g * HEAD_DIM:(g + 1) * HEAD_DIM].astype(BF16)
    elif layout == "group_cols":
        out_t = out.T.astype(BF16)
        for g in range(N_KV):
            o_ref[0, g] = out_t[g * HEAD_DIM:(g + 1) * HEAD_DIM, :]
    elif layout == "rows":
        o_ref[0] = out.astype(BF16)
    else:
        o_ref[0] = out.T.astype(BF16)


def _cmp_weights(w1, w2, pe):
    eye = jnp.eye(N_KV, dtype=F32)
    halves = w1.reshape(2, CMP_STRIDE, HEAD_DIM, -1)
    hdim = halves.shape[-1]
    wbig = jnp.einsum('acdh,gk->cgdakh', halves, eye)
    wbig = wbig.reshape(CMP_STRIDE, KV_WIDTH, 2 * N_KV * hdim).astype(BF16)
    w2big = jnp.einsum('hd,gk->ghkd', w2, eye).reshape(N_KV * hdim, KV_WIDTH).astype(BF16)
    pe_rows = jnp.broadcast_to(pe.reshape(2, CMP_STRIDE, 1, HEAD_DIM), (2, CMP_STRIDE, N_KV, HEAD_DIM))
    pe8 = jnp.zeros((8, CMP_STRIDE * KV_WIDTH), F32).at[:2].set(pe_rows.reshape(2, -1)).astype(BF16)
    return wbig, w2big, pe8


def _compress(x_view, wts, *, page_table=None, layout):
    wbig, w2big, pe8 = wts
    ncol = wbig.shape[-1]
    kdim = wbig.shape[0] * wbig.shape[1]
    if page_table is None:
        b, n_ch, _ = x_view.shape
        rows = min(n_ch, 256)
        const2 = lambda bi, i: (0, 0)
        pre, pepre = pl.pallas_call(
            _cmp1_rows_kernel, grid=(b, n_ch // rows),
            in_specs=[pl.BlockSpec((1, rows, kdim), lambda bi, i: (bi, i, 0)), pl.BlockSpec(pe8.shape, const2),
                      pl.BlockSpec(wbig.shape, lambda bi, i: (0, 0, 0))],
            out_specs=[pl.BlockSpec((1, rows, ncol), lambda bi, i: (bi, i, 0)), pl.BlockSpec((8, ncol), const2)],
            out_shape=[jax.ShapeDtypeStruct((b, n_ch, ncol), F32), jax.ShapeDtypeStruct((8, ncol), F32)],
            compiler_params=_cparams(("arbitrary", "arbitrary")), name="cmp_stage1")(x_view, pe8, wbig)
    else:
        b, n_pages = page_table.shape
        n_in = CMP_PAGES
        rpp = PAGE_SIZE // CMP_STRIDE
        rows = n_in * rpp
        n_ch = n_pages * rpp
        const2 = lambda bi, i, pt: (0, 0)
        x_specs = [pl.BlockSpec((1, KV_WIDTH, PAGE_SIZE), functools.partial(
            lambda bi, i, pt, p: (pt[bi, i * CMP_PAGES + p], 0, 0), p=p)) for p in range(n_in)]
        tok = np.arange(PAGE_SIZE)
        perm = (tok[None, :] == (tok[:, None] % rpp) * CMP_STRIDE + tok[:, None] // rpp)
        perm = jnp.asarray(perm, BF16)
        pre, pepre = pl.pallas_call(
            functools.partial(_cmp1_paged_kernel, n_in=n_in),
            grid_spec=pltpu.PrefetchScalarGridSpec(
                num_scalar_prefetch=1, grid=(b, n_pages // n_in),
                in_specs=x_specs + [pl.BlockSpec(perm.shape, const2), pl.BlockSpec(pe8.shape, const2),
                                    pl.BlockSpec(wbig.shape, lambda bi, i, pt: (0, 0, 0))],
                out_specs=[pl.BlockSpec((1, rows, ncol), lambda bi, i, pt: (bi, i, 0)),
                           pl.BlockSpec((8, ncol), const2)]),
            out_shape=[jax.ShapeDtypeStruct((b, n_ch, ncol), F32), jax.ShapeDtypeStruct((8, ncol), F32)],
            compiler_params=_cparams(("arbitrary", "arbitrary")), name="cmp_stage1_paged",
        )(page_table, *([x_view] * n_in), perm, pe8, wbig)
    o_dims = {"group_rows": (N_KV, n_ch, HEAD_DIM), "group_cols": (N_KV, HEAD_DIM, n_ch),
              "rows": (n_ch, KV_WIDTH), "cols": (KV_WIDTH, n_ch)}[layout]
    return pl.pallas_call(
        functools.partial(_cmp2_kernel, layout=layout),
        grid=(b,),
        in_specs=[pl.BlockSpec((1, n_ch, ncol), lambda bi: (bi, 0, 0)),
                  pl.BlockSpec((8, ncol), lambda bi: (0, 0)),
                  pl.BlockSpec(w2big.shape, lambda bi: (0, 0))],
        out_specs=pl.BlockSpec((1,) + o_dims, lambda bi: (bi,) + (0,) * len(o_dims)),
        out_shape=jax.ShapeDtypeStruct((b,) + o_dims, BF16),
        compiler_params=_cparams(("parallel",)), name="cmp_stage2",
    )(pre, pepre, w2big)


def _t5_bucket(dist):
    dist = np.maximum(np.asarray(dist, np.int64), 0)
    max_exact = N_BUCKETS // 2
    d32 = np.maximum(dist, 1).astype(np.float32)
    large = max_exact + (np.log(d32 / np.float32(max_exact)) / np.float32(math.log(MAX_DISTANCE / max_exact))
                         * np.float32(N_BUCKETS - max_exact)).astype(np.int32)
    large = np.minimum(large, N_BUCKETS - 1)
    return np.where(dist < max_exact, dist, large).astype(np.int32)


def _bias_lookup(cols, dist):
    bucket = jnp.asarray(_t5_bucket(dist))
    out = jnp.zeros(jnp.broadcast_shapes(cols.shape[1:], bucket.shape), F32)
    for k in range(N_BUCKETS):
        out = jnp.where(bucket == k, cols[k], out)
    return out


def _select_top_blocks(score, jrow, k_sel):
    chosen = jnp.zeros(score.shape, F32)
    work = score
    for _ in range(k_sel):
        m = jnp.max(work, axis=0, keepdims=True)
        idx = jnp.min(jnp.where(work == m, jrow, 1 << 20), axis=0, keepdims=True)
        pick = jrow == idx
        chosen = jnp.where(pick, 1.0, chosen)
        work = jnp.where(pick, PICKED, work)
    return chosen


def _nsa_prompt_kernel(q_ref, gt_ref, sa_ref, kcb_ref, vcbt_ref, ks_ref, vst_ref, kw_ref, vwt_ref,
                       pcd_ref, far_ref, tz0_ref, tz1_ref, covt_ref, o_ref,
                       lc_ref, selx_ref, m_ref, l_ref, acc_ref, *, n_slc):
    i = pl.program_id(2)
    nl = HPG * TQ
    ncp = kcb_ref.shape[2]
    q_t = jnp.concatenate([q_ref[0, 0, r] for r in range(HPG)], axis=1)
    lane_t = lax.broadcasted_iota(jnp.int32, (1, nl), 1) & (TQ - 1)
    qpos = i * TQ + lane_t
    far = far_ref[0, 0:1, :]

    lc_ref[0:CPT, :] = jnp.zeros((CPT, nl), F32)
    lc_ref[CPT:CPT + ncp, :] = _dot(kcb_ref[0, 0], q_t) + far
    near = pl.ds(pl.multiple_of(i * CPT, CPT), 2 * CPT)
    lc_ref[near, :] = lc_ref[near, :] + pcd_ref[0]
    lc = lc_ref[CPT:CPT + ncp, :]
    cend = lax.broadcasted_iota(jnp.int32, (ncp, 1), 0) * CMP_STRIDE + (CMP_BLOCK - 1)
    valid_c = cend <= qpos
    lm = jnp.where(valid_c, lc, NEG)
    mc = jnp.max(lm, axis=0, keepdims=True)
    ec = jnp.where(valid_c, jnp.exp(lm - mc), 0.0)
    den = jnp.sum(ec, axis=0, keepdims=True)
    p_c = ec / jnp.where(den > 0, den, 1.0)
    o_c = _dot(vcbt_ref[0, 0], p_c.astype(BF16))

    psum = p_c[:, 0:TQ]
    for r in range(1, HPG):
        psum = psum + p_c[:, r * TQ:(r + 1) * TQ]
    hi, lo = _split_hi_lo(psum)
    imp_t = _dot(covt_ref[...], hi) + _dot(covt_ref[...], lo)
    jrow = lax.broadcasted_iota(jnp.int32, (LANES, TQ), 0)
    qpos_t = qpos[:, 0:TQ]
    cur = qpos_t // SEL_BLOCK
    causal = (jrow * SEL_BLOCK <= qpos_t) & (jrow < n_slc)
    forced = (jrow == 0) | (jrow == cur) | (jrow == cur - 1)
    score = jnp.where(forced, FORCE_SCORE, jnp.where(causal, imp_t, NEG))
    chosen = _select_top_blocks(score, jrow, min(N_SEL, n_slc))
    sel_add = jnp.where(causal & (chosen > 0.5), 0.0, NEG)
    selx_ref[...] = jnp.concatenate([sel_add] * HPG, axis=1)

    key_u = lax.broadcasted_iota(jnp.int32, (TQ, 1), 0)
    causal_diag = key_u <= lane_t

    def attend(k_ref, vt_ref, tiles):
        s_list = []
        for kt, bias, mask in tiles:
            s = _dot(k_ref[0, 0, pl.ds(pl.multiple_of(kt * TQ, TQ), TQ), :], q_t) + bias
            s_list.append(s if mask is None else jnp.where(mask, s, NEG))
        m = functools.reduce(jnp.maximum, [jnp.max(s, axis=0, keepdims=True) for s in s_list])
        p_list = [jnp.exp(s - m) for s in s_list]
        l = functools.reduce(jnp.add, [jnp.sum(p, axis=0, keepdims=True) for p in p_list])
        acc = functools.reduce(jnp.add, [_dot(vt_ref[0, 0, kt], p.astype(BF16))
                                         for (kt, _, _), p in zip(tiles, p_list)])
        return acc, m, l

    def sel_rows(kt, extra):
        nb = TQ // SEL_BLOCK
        rows = [jnp.broadcast_to(selx_ref[pl.ds(kt * nb + jb, 1), :] + extra, (SEL_BLOCK, nl)) for jb in range(nb)]
        return jnp.concatenate(rows, axis=0)

    m_ref[...] = jnp.full(m_ref.shape, NEG, F32)
    l_ref[...] = jnp.zeros(l_ref.shape, F32)
    acc_ref[...] = jnp.zeros(acc_ref.shape, F32)

    def merge_far(kts):
        acc, m, l = attend(ks_ref, vst_ref, [(kt, sel_rows(kt, far), None) for kt in kts])
        m_old = m_ref[...]
        m_new = jnp.maximum(m_old, m)
        a_old = jnp.exp(m_old - m_new)
        a_new = jnp.exp(m - m_new)
        l_ref[...] = a_old * l_ref[...] + a_new * l
        acc_ref[...] = a_old * acc_ref[...] + a_new * acc
        m_ref[...] = m_new

    n_far = jnp.maximum(i - 1, 0)

    def far_pair(jp, carry):
        merge_far([2 * jp, 2 * jp + 1])
        return carry

    lax.fori_loop(0, n_far // 2, far_pair, 0)

    @pl.when(n_far % 2 == 1)
    def _():
        merge_far([n_far - 1])

    prev = jnp.maximum(i - 1, 0)
    has_prev = jnp.where(i >= 1, 0.0, NEG)
    acc_n, m_n, l_n = attend(ks_ref, vst_ref, [(prev, tz1_ref[0] + sel_rows(prev, has_prev), None),
                                               (i, tz0_ref[0] + sel_rows(i, 0.0), causal_diag)])
    m_old = m_ref[...]
    m_new = jnp.maximum(m_old, m_n)
    a_old = jnp.exp(m_old - m_new)
    a_new = jnp.exp(m_n - m_new)
    o_s = (a_old * acc_ref[...] + a_new * acc_n) / (a_old * l_ref[...] + a_new * l_n)

    prev2 = jnp.maximum(i - 2, 0)
    has_prev2 = jnp.where(i >= 2, 0.0, NEG)
    acc_w, _, l_w = attend(kw_ref, vwt_ref, [(prev2, far + has_prev2, key_u > lane_t),
                                             (prev, tz1_ref[0] + has_prev, None),
                                             (i, tz0_ref[0], causal_diag)])
    o_w = acc_w / l_w

    g_t = gt_ref[0].T

    def gate_row(branch):
        return jnp.concatenate([g_t[branch * HPG + r:branch * HPG + r + 1, :] for r in range(HPG)], axis=1)

    o_t = gate_row(0) * o_c + gate_row(1) * o_s + gate_row(2) * o_w
    o_rd = jnp.concatenate([o_t[:, r * TQ:(r + 1) * TQ] for r in range(HPG)], axis=0)
    o_ref[0] = (o_rd.T * sa_ref[0]).astype(BF16)


def _nsa_prompt(q_t, gt, sa, kcb, vcb_t, ks, vs_t, kw, vw_t, rel_bias):
    b, _, _, _, s = q_t.shape
    assert WINDOW == 2 * TQ and s % TQ == 0 and TQ >= MAX_DISTANCE
    nq = s // TQ
    ncp = s // CMP_STRIDE
    n_cmp = ncp - 1
    n_slc = -(-s // SEL_BLOCK)
    assert n_slc <= LANES and ncp >= 2 * CPT
    nl = HPG * TQ
    table = rel_bias.astype(F32)
    uu = np.arange(TQ)[:, None]
    tt = np.arange(TQ)[None, :]

    def per_group(tab):
        rows = tab.shape[1]
        return jnp.transpose(tab.reshape(N_KV, HPG, rows, TQ), (0, 2, 1, 3)).reshape(N_KV, rows, nl)

    cols = table[:, :, None, None]
    tz0 = per_group(_bias_lookup(cols, (tt - uu)[None]))
    tz1 = per_group(_bias_lookup(cols, (TQ + tt - uu)[None]))
    far_h = _bias_lookup(table, np.full((N_HEADS,), MAX_DISTANCE))
    far = per_group(jnp.broadcast_to(far_h[:, None, None], (N_HEADS, 8, TQ)))
    e = np.arange(2 * CPT)[:, None] - CPT
    pcd = per_group(_bias_lookup(cols, (tt - CMP_STRIDE * e - (CMP_BLOCK - 1))[None]) - far_h[:, None, None])
    c = np.arange(ncp)[None, :]
    j = np.arange(LANES)[:, None]
    cov_t = ((c * CMP_STRIDE < (j + 1) * SEL_BLOCK) & (c * CMP_STRIDE + CMP_BLOCK > j * SEL_BLOCK)
             & (c < n_cmp) & (j < n_slc))
    cov_t = jnp.asarray(cov_t, BF16)

    per_g = lambda rows: pl.BlockSpec((1, rows, nl), lambda bi, g, i: (g, 0, 0))
    return pl.pallas_call(
        functools.partial(_nsa_prompt_kernel, n_slc=n_slc),
        grid=(b, N_KV, nq),
        in_specs=[
            pl.BlockSpec((1, 1, HPG, HEAD_DIM, TQ), lambda bi, g, i: (bi, g, 0, 0, i)),
            pl.BlockSpec((1, TQ, LANES), lambda bi, g, i: (bi, i, g)),
            pl.BlockSpec((1, TQ, KV_WIDTH), lambda bi, g, i: (bi, i, g)),
            pl.BlockSpec((1, 1, ncp, HEAD_DIM), lambda bi, g, i: (bi, g, 0, 0)),
            pl.BlockSpec((1, 1, HEAD_DIM, ncp), lambda bi, g, i: (bi, g, 0, 0)),
            pl.BlockSpec((1, 1, s, HEAD_DIM), lambda bi, g, i: (bi, g, 0, 0)),
            pl.BlockSpec((1, 1, nq, HEAD_DIM, TQ), lambda bi, g, i: (bi, g, 0, 0, 0)),
            pl.BlockSpec((1, 1, s, HEAD_DIM), lambda bi, g, i: (bi, g, 0, 0)),
            pl.BlockSpec((1, 1, nq, HEAD_DIM, TQ), lambda bi, g, i: (bi, g, 0, 0, 0)),
            per_g(2 * CPT), per_g(8), per_g(TQ), per_g(TQ),
            pl.BlockSpec(cov_t.shape, lambda bi, g, i: (0, 0)),
        ],
        out_specs=pl.BlockSpec((1, TQ, KV_WIDTH), lambda bi, g, i: (bi, i, g)),
        out_shape=jax.ShapeDtypeStruct((b, s, NSA_WIDTH), BF16),
        scratch_shapes=[pltpu.VMEM((CPT + ncp, nl), F32), pltpu.VMEM((LANES, nl), F32),
                        pltpu.VMEM((1, nl), F32), pltpu.VMEM((1, nl), F32), pltpu.VMEM((HEAD_DIM, nl), F32)],
        compiler_params=_cparams(("parallel", "parallel", "arbitrary")),
        name="nsa_prompt",
    )(q_t, gt, sa, kcb, vcb_t, ks, vs_t, kw, vw_t, pcd, far, tz0, tz1, cov_t)


def _nsa_sample_kernel(*refs, n_pg, n_chunks, n_slc, past, t_new):
    (qbd_ref, kcbt_ref, vcb_ref, bc_ref, cov_ref, rmat_ref) = refs[1:7]
    kpg = refs[7:7 + n_pg]
    vpg = refs[7 + n_pg:7 + 2 * n_pg]
    (bsl_ref, ksn_ref, vsn_ref, bsn_ref, kwc_ref, vwc_ref, kwn_ref, vwn_ref, bw_ref, gt_ref, e4_ref,
     o_ref, sel_ref, m_ref, l_ref, acc_ref, oc_ref) = refs[7 + 2 * n_pg:]
    j = pl.program_id(1)
    qbd = qbd_ref[0]
    nsp = cov_ref.shape[1]
    n_win = kwc_ref.shape[2]

    def softmax_rows(s):
        m = jnp.max(s, axis=-1, keepdims=True)
        e = jnp.where(s > 0.5 * NEG, jnp.exp(s - m), 0.0)
        den = jnp.sum(e, axis=-1, keepdims=True)
        return e / jnp.where(den > 0, den, 1.0)

    def flash_step(s, v, v_is_transposed):
        m_old = m_ref[...]
        m_new = jnp.maximum(m_old, jnp.max(s, axis=-1, keepdims=True))
        alpha = jnp.exp(m_old - m_new)
        p = jnp.exp(s - m_new)
        l_ref[...] = alpha * l_ref[...] + jnp.sum(p, axis=-1, keepdims=True)
        pv = _dot_nt(p.astype(BF16), v) if v_is_transposed else _dot(p.astype(BF16), v)
        acc_ref[...] = alpha * acc_ref[...] + pv
        m_ref[...] = m_new

    @pl.when(j == 0)
    def _():
        p_c = softmax_rows(_dot(qbd, kcbt_ref[0]) + bc_ref[...])
        oc_ref[...] = _dot(p_c.astype(BF16), vcb_ref[0])
        hi, lo = _split_hi_lo(p_c)
        psum = _dot(rmat_ref[...], hi) + _dot(rmat_ref[...], lo)
        hi, lo = _split_hi_lo(psum)
        imp_t = (_dot(hi, cov_ref[...]) + _dot(lo, cov_ref[...])).T
        jrow = lax.broadcasted_iota(jnp.int32, (nsp, LANES), 0)
        lane = lax.broadcasted_iota(jnp.int32, (1, LANES), 1)
        qpos = past + lane % t_new
        cur = qpos // SEL_BLOCK
        causal = (jrow * SEL_BLOCK <= qpos) & (jrow < n_slc)
        forced = (jrow == 0) | (jrow == cur) | (jrow == cur - 1)
        score = jnp.where(forced, FORCE_SCORE, jnp.where(causal, imp_t, NEG))
        chosen = _select_top_blocks(score, jrow, min(N_SEL, n_slc))
        sel = jnp.where(causal & (lane < N_HEADS * t_new) & (chosen > 0.5), 0.0, -1.0).T
        for k in range(nsp // LANES):
            sel_ref[k] = sel[:, k * LANES:(k + 1) * LANES]
        m_ref[...] = jnp.full(m_ref.shape, NEG, F32)
        l_ref[...] = jnp.zeros(l_ref.shape, F32)
        acc_ref[...] = jnp.zeros(acc_ref.shape, F32)

    k_t = jnp.concatenate([r[0] for r in kpg], axis=1).astype(BF16)
    v_t = jnp.concatenate([r[0] for r in vpg], axis=1).astype(BF16)
    far = bsl_ref[:, PAGE_SIZE:2 * PAGE_SIZE]
    last = jnp.where(j == n_chunks - 1, bsl_ref[:, 0:PAGE_SIZE], far)
    bias = jnp.concatenate([far] * (n_pg - 1) + [last], axis=1)
    chunks_per_tile = LANES // (n_pg * (PAGE_SIZE // SEL_BLOCK))
    mask_add = _dot(sel_ref[j // chunks_per_tile].astype(BF16), e4_ref[j % chunks_per_tile])
    flash_step(_dot(qbd, k_t) + bias + mask_add, v_t, True)

    @pl.when(j == n_chunks - 1)
    def _():
        sn = _dot_nt(qbd, ksn_ref[0].astype(BF16)) + bsn_ref[...]
        blk = n_slc - 1
        seln = sel_ref[blk // LANES][:, blk % LANES:blk % LANES + 1]
        flash_step(sn + seln * (-NEG), vsn_ref[0].astype(BF16), False)
        l = l_ref[...]
        o_s = acc_ref[...] / jnp.where(l > 0, l, 1.0)
        sw = jnp.concatenate([_dot(qbd, kwc_ref[0].astype(BF16)), _dot_nt(qbd, kwn_ref[0].astype(BF16))], axis=1)
        p_w = softmax_rows(sw + bw_ref[...]).astype(BF16)
        o_w = _dot_nt(p_w[:, :n_win], vwc_ref[0].astype(BF16)) + _dot(p_w[:, n_win:], vwn_ref[0].astype(BF16))
        gt = gt_ref[0]
        o_ref[0] = gt[:, 0:1] * oc_ref[...] + gt[:, 1:2] * o_s + gt[:, 2:3] * o_w


def _nsa_sample(q, gates, kcb_t, vcb, cache_k_slc, cache_v_slc, page_table, ks_new, vs_new,
                kw_cache, vw_cache, kw_new, vw_new, rel_bias, *, past):
    db, t_new, _ = q.shape
    n_pages = page_table.shape[1]
    assert past == n_pages * PAGE_SIZE and past % SEL_BLOCK == 0 and t_new <= SEL_BLOCK
    assert PAGE_SIZE >= MAX_DISTANCE
    n_cp = kcb_t.shape[2]
    n_slc = -(-(past + t_new) // SEL_BLOCK)
    nsp = -(-n_slc // LANES) * LANES
    n_pg = SAMPLE_PAGES
    n_chunks = n_pages // n_pg
    blocks_per_chunk = n_pg * (PAGE_SIZE // SEL_BLOCK)
    assert LANES % blocks_per_chunk == 0
    n_win = kw_cache.shape[2]
    nl = N_HEADS * t_new
    assert nl <= LANES
    row = np.arange(LANES)
    row_ok = (row < nl)[:, None]
    row_g = np.where(row < nl, row // (HPG * t_new), 0)
    row_t = (row % t_new)[:, None]
    q5 = q.reshape(db, t_new, N_KV, HPG, HEAD_DIM)
    qbd = jnp.einsum('btgrd,gk->bgrtkd', q5.astype(F32), jnp.eye(N_KV, dtype=F32)).reshape(db, nl, KV_WIDTH)
    qbd = jnp.pad(qbd, ((0, 0), (0, LANES - nl), (0, 0))).astype(BF16)

    tab_rows = jnp.pad(jnp.repeat(rel_bias.astype(F32), t_new, axis=1), ((0, 0), (0, LANES - nl)))[:, :, None]

    def bias_tab(dist, valid):
        return jnp.where(jnp.asarray(valid & row_ok), _bias_lookup(tab_rows, dist), NEG)

    qpos = past + row_t
    cblk = np.arange(n_cp)[None, :]
    dist_c = qpos - (cblk * CMP_STRIDE + CMP_BLOCK - 1)
    bc = bias_tab(dist_c, (dist_c >= 0) & (cblk < n_cp - 1))
    kpos = past - PAGE_SIZE + np.arange(PAGE_SIZE)[None, :]
    bsl = jnp.concatenate([bias_tab(qpos - kpos, np.ones((LANES, PAGE_SIZE), bool)),
                           bias_tab(np.full((LANES, PAGE_SIZE), MAX_DISTANCE), np.ones((LANES, PAGE_SIZE), bool))],
                          axis=1)
    u = np.arange(PAGE_SIZE)[None, :]
    new_ok = (u <= row_t) & (u < t_new)
    bsn = bias_tab(row_t - u, new_ok)
    wpos = past - n_win + np.arange(n_win)[None, :]
    dist_w = qpos - wpos
    bw = jnp.concatenate([bias_tab(dist_w, (dist_w >= 0) & (dist_w < WINDOW) & (wpos >= 0)),
                          bias_tab(row_t - u, new_ok & (row_t - u < WINDOW))], axis=1)
    c = np.arange(n_cp)[:, None]
    jb = np.arange(nsp)[None, :]
    cov = ((c * CMP_STRIDE < (jb + 1) * SEL_BLOCK) & (c * CMP_STRIDE + CMP_BLOCK > jb * SEL_BLOCK)
           & (c < n_cp - 1) & (jb < n_slc))
    cov = jnp.asarray(cov, BF16)
    same = (row_g[:, None] == row_g[None, :]) & (row_t == row_t.T) & row_ok & row_ok.T
    rmat = jnp.asarray(same, BF16)
    kk = np.arange(n_pg * PAGE_SIZE)[None, None, :] // SEL_BLOCK
    e4 = np.arange(LANES)[None, :, None] == (np.arange(LANES // blocks_per_chunk)[:, None, None] * blocks_per_chunk + kk)
    e4 = jnp.asarray(np.where(e4, -NEG, 0.0), BF16)
    g5 = gates.reshape(db, t_new, N_KV, LANES)[..., :3 * HPG].reshape(db, t_new, N_KV, 3, HPG)
    gcol = jnp.transpose(g5, (0, 2, 4, 1, 3)).reshape(db, nl, 3)
    gcol = jnp.pad(gcol, ((0, 0), (0, LANES - nl), (0, 5)))

    def pad_new(a):
        return jnp.pad(a, ((0, 0), (0, PAGE_SIZE - t_new), (0, 0)))

    ksn, vsn, kwn, vwn = (pad_new(a) for a in (ks_new, vs_new, kw_new, vw_new))

    per_b = lambda shape: pl.BlockSpec((1,) + shape, lambda b, jc, pt: (b, 0, 0))
    full = lambda a: pl.BlockSpec(a.shape, lambda b, jc, pt: (0,) * a.ndim)
    page = lambda p: pl.BlockSpec((1, KV_WIDTH, PAGE_SIZE), lambda b, jc, pt: (pt[b, jc * SAMPLE_PAGES + p], 0, 0))
    in_specs = ([per_b((LANES, KV_WIDTH)), per_b((KV_WIDTH, n_cp)), per_b((n_cp, KV_WIDTH)),
                 full(bc), full(cov), full(rmat)]
                + [page(p) for p in range(n_pg)] * 2
                + [full(bsl), per_b((PAGE_SIZE, KV_WIDTH)), per_b((PAGE_SIZE, KV_WIDTH)), full(bsn),
                   per_b((KV_WIDTH, n_win)), per_b((KV_WIDTH, n_win)),
                   per_b((PAGE_SIZE, KV_WIDTH)), per_b((PAGE_SIZE, KV_WIDTH)), full(bw), per_b((LANES, 8)), full(e4)])
    o = pl.pallas_call(
        functools.partial(_nsa_sample_kernel, n_pg=n_pg, n_chunks=n_chunks, n_slc=n_slc, past=past, t_new=t_new),
        grid_spec=pltpu.PrefetchScalarGridSpec(
            num_scalar_prefetch=1, grid=(db, n_chunks), in_specs=in_specs,
            out_specs=pl.BlockSpec((1, LANES, KV_WIDTH), lambda b, jc, pt: (b, 0, 0)),
            scratch_shapes=[pltpu.VMEM((nsp // LANES, LANES, LANES), F32), pltpu.VMEM((LANES, 1), F32),
                            pltpu.VMEM((LANES, 1), F32), pltpu.VMEM((LANES, KV_WIDTH), F32),
                            pltpu.VMEM((LANES, KV_WIDTH), F32)]),
        out_shape=jax.ShapeDtypeStruct((db, LANES, KV_WIDTH), F32),
        compiler_params=_cparams(("arbitrary", "arbitrary")),
        name="nsa_sample",
    )(page_table, qbd, kcb_t, vcb, bc, cov, rmat, *([cache_k_slc] * n_pg), *([cache_v_slc] * n_pg),
      bsl, ksn, vsn, bsn, kw_cache, vw_cache, kwn, vwn, bw, gcol, e4)
    o6 = o[:, :nl].reshape(db, N_KV, HPG, t_new, N_KV, HEAD_DIM)
    o_diag = jnp.stack([o6[:, g, :, :, g] for g in range(N_KV)], axis=1)
    return jnp.transpose(o_diag, (0, 3, 1, 2, 4)).reshape(db, t_new, NSA_WIDTH)


def _conv_tail(y, cb_ref, lg_ref, lb_ref, wpw_ref, bpw_ref, sb):
    y = y + cb_ref[...]
    mu = jnp.mean(y, axis=-1, keepdims=True)
    yc = y - mu
    var = jnp.mean(yc * yc, axis=-1, keepdims=True)
    yn = yc * lax.rsqrt(var + LN_EPS) * lg_ref[...] + lb_ref[...]
    act = yn * _sigmoid(yn)
    return ((_dot(act.astype(BF16), wpw_ref[...]) + bpw_ref[...]) * sb).astype(BF16)


def _conv_prompt_kernel(c_ref, halo_ref, init_ref, cw_ref, cb_ref, lg_ref, lb_ref, wpw_ref, bpw_ref, sb_ref,
                        o_ref, full_ref, sh_ref, y_ref, *, ts):
    j = pl.program_id(1)
    full_ref[CONV_HALO:CONV_HALO + ts, :] = c_ref[0]

    @pl.when(j == 0)
    def _():
        full_ref[0:CONV_HALO, :] = init_ref[0]

    @pl.when(j > 0)
    def _():
        full_ref[0:CONV_HALO, :] = halo_ref[0]

    first = CONV_HALO - (CONV_WIDTH - 1)
    span = sh_ref.shape[1]
    for sft in range(1, 8):
        sh_ref[sft - 1] = full_ref[sft:sft + span, :]
    rb = 64
    ch = full_ref.shape[1]
    for c0 in range(0, ch, LANES):
        for r0 in range(0, ts, rb):
            acc = jnp.zeros((rb, LANES), F32)
            for w in range(CONV_WIDTH):
                sft = (first + w) % 8
                base = r0 + first + w - sft
                if sft == 0:
                    x = full_ref[base:base + rb, c0:c0 + LANES]
                else:
                    x = sh_ref[sft - 1, base:base + rb, c0:c0 + LANES]
                acc = acc + x * cw_ref[w:w + 1, c0:c0 + LANES]
            y_ref[r0:r0 + rb, c0:c0 + LANES] = acc
    o_ref[0] = _conv_tail(y_ref[...], cb_ref, lg_ref, lb_ref, wpw_ref, bpw_ref, sb_ref[0])


def _conv_sample_kernel(c_ref, st_ref, cw_ref, cb_ref, lg_ref, lb_ref, wpw_ref, bpw_ref, sb_ref,
                        o_ref, full_ref, y_ref):
    nb, t_new, _ = c_ref.shape
    n_st = st_ref.shape[1]
    full_ref[:, 0:n_st, :] = st_ref[...]
    full_ref[:, n_st:n_st + t_new, :] = c_ref[...]
    first = n_st - (CONV_WIDTH - 1)
    for b in range(nb):
        acc = jnp.zeros((t_new, full_ref.shape[2]), F32)
        for w in range(CONV_WIDTH):
            acc = acc + full_ref[b, first + w:first + w + t_new, :] * cw_ref[w:w + 1, :]
        y_ref[b * t_new:(b + 1) * t_new, :] = acc
    o_ref[...] = _conv_tail(y_ref[...], cb_ref, lg_ref, lb_ref, wpw_ref, bpw_ref, sb_ref[...])


def _conv_params(conv_w, conv_b, ln_g, ln_b, w_pw, b_pw):
    ch = conv_w.shape[1]
    cw = jnp.pad(conv_w, ((0, 32 - CONV_WIDTH), (0, 0)))
    return (cw, conv_b.reshape(1, ch), ln_g.reshape(1, ch), ln_b.reshape(1, ch), w_pw.astype(BF16),
            b_pw.reshape(1, ch))


def _conv_prompt(c_in, init, params, sb, *, ts):
    b, s, ch = c_in.shape
    hb = ts // CONV_HALO
    const = lambda a: pl.BlockSpec(a.shape, lambda bi, j: (0,) * a.ndim)
    return pl.pallas_call(
        functools.partial(_conv_prompt_kernel, ts=ts),
        grid=(b, s // ts),
        in_specs=[pl.BlockSpec((1, ts, ch), lambda bi, j: (bi, j, 0)),
                  pl.BlockSpec((1, CONV_HALO, ch), lambda bi, j: (bi, jnp.maximum(j * hb - 1, 0), 0)),
                  pl.BlockSpec((1, CONV_HALO, ch), lambda bi, j: (bi, 0, 0))]
        + [const(a) for a in params]
        + [pl.BlockSpec((1, ts, ch), lambda bi, j: (bi, j, 0))],
        out_specs=pl.BlockSpec((1, ts, ch), lambda bi, j: (bi, j, 0)),
        out_shape=jax.ShapeDtypeStruct((b, s, ch), BF16),
        scratch_shapes=[pltpu.VMEM((CONV_HALO + ts, ch), F32), pltpu.VMEM((7, CONV_HALO + ts - 8, ch), F32),
                        pltpu.VMEM((ts, ch), F32)],
        compiler_params=_cparams(("parallel", "arbitrary")),
        name="conv_prompt",
    )(c_in, c_in, init, *params, sb)


def _conv_sample(c_in, state, params, sb):
    db, t_new, ch = c_in.shape
    n_st = state.shape[1]
    rows_pad = -(-(n_st + t_new) // 8) * 8
    return pl.pallas_call(
        _conv_sample_kernel,
        out_shape=jax.ShapeDtypeStruct((db * t_new, ch), BF16),
        scratch_shapes=[pltpu.VMEM((db, rows_pad, ch), F32), pltpu.VMEM((db * t_new, ch), F32)],
        compiler_params=pltpu.CompilerParams(vmem_limit_bytes=VMEM_LIMIT),
        name="conv_sample",
    )(c_in, state, *params, sb)


def _out_kernel(x_ref, ma_ref, mb_ref, *rest, gated):
    if gated:
        sa_ref, wa_ref, wb_ref, gp_ref, y_ref = rest
        ma = (ma_ref[0] * sa_ref[0]).astype(BF16)
    else:
        wa_ref, wb_ref, gp_ref, y_ref = rest
        ma = ma_ref[0]
    z = _dot(ma, wa_ref[...]) + _dot(mb_ref[0], wb_ref[...])
    ms = jnp.mean(z * z, axis=-1, keepdims=True)
    y_ref[0] = x_ref[0] + z * lax.rsqrt(ms + RMS_EPS) * gp_ref[...]


def _out_proj(x, ma, mb, sa, w_out, g_post, *, tm):
    b, s, d = x.shape
    na = ma.shape[-1]
    wa = w_out[:na].astype(BF16)
    wb = w_out[na:].astype(BF16)
    row = lambda width: pl.BlockSpec((1, tm, width), lambda bi, i: (bi, i, 0))
    const = lambda a: pl.BlockSpec(a.shape, lambda bi, i: (0, 0))
    gp = g_post.reshape(1, d)
    gate_in, gate_spec = ([sa], [row(na)]) if sa is not None else ([], [])
    return pl.pallas_call(
        functools.partial(_out_kernel, gated=sa is not None),
        grid=(b, s // tm),
        in_specs=[row(d), row(na), row(mb.shape[-1])] + gate_spec + [const(wa), const(wb), const(gp)],
        out_specs=row(d),
        out_shape=jax.ShapeDtypeStruct((b, s, d), F32),
        compiler_params=_cparams(("parallel", "parallel")),
        name="out_proj",
    )(x, ma, mb, *gate_in, wa, wb, gp)


def _split_w_in(w_in):
    d = w_in.shape[0]
    c0 = NSA_WIDTH + 6 * KV_WIDTH
    n_gate = 3 * N_HEADS
    conv_ch = (w_in.shape[1] - c0 - n_gate - NSA_WIDTH) // 3
    w_qkv = w_in[:, :c0].astype(BF16)
    wg = w_in[:, c0:c0 + n_gate].reshape(d, N_KV, HPG, 3)
    wg = jnp.transpose(wg, (0, 1, 3, 2)).reshape(d, N_KV, 3 * HPG)
    wg = jnp.pad(wg, ((0, 0), (0, 0), (0, LANES - 3 * HPG))).reshape(d, N_KV * LANES)
    z_a = w_in[:, c0 + n_gate:c0 + n_gate + NSA_WIDTH]
    glu0 = c0 + n_gate + NSA_WIDTH
    w_glu = w_in[:, glu0:glu0 + 2 * conv_ch].astype(BF16)
    z_b = w_in[:, glu0 + 2 * conv_ch:]
    assert conv_ch == NSA_WIDTH
    w_gate = jnp.concatenate([z_a, z_b, wg], axis=1).astype(BF16)
    return w_qkv, w_gate, w_glu


def _token_minor(cache):
    n, tokens = cache.shape[:2]
    return jnp.transpose(cache, (0, 2, 3, 1)).reshape(n, KV_WIDTH, tokens)


def kernel(x_prompt, x_sample, cache_k_cmp, cache_v_cmp, cache_k_slc, cache_v_slc, cache_k_win, cache_v_win,
           state_conv, page_table, g_pre, w_in, cmp_w1_k, cmp_w2_k, cmp_pe_k, cmp_w1_v, cmp_w2_v, cmp_pe_v,
           rel_bias, conv_w, conv_b, ln_g, ln_b, w_pw, b_pw, w_out, g_post):
    depth = g_pre.shape[0]
    assert depth == 1
    layer = 0
    b, s, d = x_prompt.shape
    db, t_new, _ = x_sample.shape
    past = page_table.shape[1] * PAGE_SIZE
    conv_ch = conv_w.shape[-1]

    w_qkv, w_gate, w_glu = _split_w_in(w_in[layer])
    wts_k = _cmp_weights(cmp_w1_k[layer], cmp_w2_k[layer], cmp_pe_k[layer])
    wts_v = _cmp_weights(cmp_w1_v[layer], cmp_w2_v[layer], cmp_pe_v[layer])
    cparams = _conv_params(conv_w[layer], conv_b[layer], ln_g[layer], ln_b[layer], w_pw[layer], b_pw[layer])
    chunk_w = CMP_STRIDE * KV_WIDTH

    (q_t, kc, vc, ks, vs, kw, vw, ks_g, kw_g, vs_t, vw_t), sa, sb, gt, c_in = _projections(
        x_prompt, g_pre[layer], w_qkv, w_gate, w_glu, tm=512, attn_layouts=True)
    n_ch = s // CMP_STRIDE
    kcb = _compress(kc[:, :n_ch * CMP_STRIDE].reshape(b, n_ch, chunk_w), wts_k, layout="group_rows")
    vcb_t = _compress(vc[:, :n_ch * CMP_STRIDE].reshape(b, n_ch, chunk_w), wts_v, layout="group_cols")
    ma = _nsa_prompt(q_t, gt, sa, kcb, vcb_t, ks_g, vs_t, kw_g, vw_t, rel_bias)
    mb = _conv_prompt(c_in, jnp.zeros((b, CONV_HALO, conv_ch), F32), cparams, sb, ts=256)
    y_prompt = _out_proj(x_prompt, ma, mb, None, w_out[layer], g_post[layer], tm=512)
    n_keep = min(WINDOW, s)
    kv5 = lambda a: a.reshape(1, a.shape[0], a.shape[1], N_KV, HEAD_DIM)
    outs_p = (kv5(kc), kv5(vc), kv5(ks), kv5(vs), kv5(kw[:, -n_keep:]), kv5(vw[:, -n_keep:]),
              c_in[None, :, -(CONV_WIDTH - 1):])

    xs = x_sample.reshape(1, db * t_new, d)
    (q_s, kc_s, vc_s, ks_s, vs_s, kw_s, vw_s), sa_s, sb_s, gt_s, c_s = _projections(
        xs, g_pre[layer], w_qkv, w_gate, w_glu, tm=db * t_new, attn_layouts=False)
    tok = lambda a: a.reshape(db, t_new, a.shape[-1])
    kcb_s = _compress(_token_minor(cache_k_cmp[layer]), wts_k, page_table=page_table, layout="cols")
    vcb_s = _compress(_token_minor(cache_v_cmp[layer]), wts_v, page_table=page_table, layout="rows")
    o_a = _nsa_sample(tok(q_s), tok(gt_s), kcb_s, vcb_s,
                      _token_minor(cache_k_slc[layer]), _token_minor(cache_v_slc[layer]), page_table,
                      tok(ks_s), tok(vs_s), _token_minor(cache_k_win[layer]), _token_minor(cache_v_win[layer]),
                      tok(kw_s), tok(vw_s), rel_bias, past=past)
    mb_s = _conv_sample(tok(c_s), state_conv[layer], cparams, sb_s[0])
    y_sample = _out_proj(xs, o_a.reshape(1, db * t_new, NSA_WIDTH), mb_s[None], sa_s, w_out[layer],
                         g_post[layer], tm=db * t_new).reshape(db, t_new, d)
    n_keep_s = min(WINDOW, past + t_new)
    kv5s = lambda a: a.reshape(1, db, t_new, N_KV, HEAD_DIM)
    win = lambda cache, new: jnp.concatenate(
        [cache[layer], new.reshape(db, t_new, N_KV, HEAD_DIM)], axis=1)[None, :, -n_keep_s:]
    conv_s = jnp.concatenate([state_conv[layer], tok(c_s)], axis=1)[None, :, -(CONV_WIDTH - 1):]
    outs_s = (kv5s(kc_s), kv5s(vc_s), kv5s(ks_s), kv5s(vs_s), win(cache_k_win, kw_s), win(cache_v_win, vw_s), conv_s)
    return (y_prompt, y_sample) + outs_p + outs_s
```

```python
import functools
import math

import numpy as np
import jax
import jax.numpy as jnp
from jax import lax
from jax.experimental import pallas as pl
from jax.experimental.pallas import tpu as pltpu

F32 = jnp.float32
BF16 = jnp.bfloat16

HEAD_DIM = 64
N_KV = 4
HPG = 4
N_HEADS = N_KV * HPG
KV_WIDTH = N_KV * HEAD_DIM
NSA_WIDTH = N_HEADS * HEAD_DIM
CMP_BLOCK = 32
CMP_STRIDE = 16
SEL_BLOCK = 64
N_SEL = 16
WINDOW = 512
CONV_WIDTH = 31
N_BUCKETS = 32
MAX_DISTANCE = 128
FORCE_SCORE = 1e6
RMS_EPS = 1e-6
LN_EPS = 1e-5
PAGE_SIZE = 128
SCALE = HEAD_DIM ** -0.5

NEG = -1e30
PICKED = -3e38
LANES = 128
VMEM_LIMIT = 56 * 1024 * 1024

LOG2E = 1.4426950408889634
TQ = 256
FAR_GROUP = 4
AUG_PAD = 16
K_WIN_AUG = HEAD_DIM + AUG_PAD
K_SEL_AUG = HEAD_DIM + LANES + AUG_PAD
CPT = TQ // CMP_STRIDE
CONV_HALO = 32
SAMPLE_PAGES = 32
CMP_PAGES = 32


def _cparams(sem):
    return pltpu.CompilerParams(dimension_semantics=sem, vmem_limit_bytes=VMEM_LIMIT)


def _dot(a, b):
    return jnp.dot(a, b, preferred_element_type=F32)


def _dot_nt(a, b):
    return lax.dot_general(a, b, (((1,), (1,)), ((), ())), preferred_element_type=F32)


def _sigmoid(x):
    return 1.0 / (1.0 + jnp.exp(-x))


def _split_hi_lo(x):
    hi = x.astype(BF16)
    lo = (x - hi.astype(F32)).astype(BF16)
    return hi, lo


def _normed(x_ref, g_ref):
    x = x_ref[0]
    ms = jnp.mean(x * x, axis=-1, keepdims=True)
    return (x * lax.rsqrt(ms + RMS_EPS) * g_ref[...]).astype(BF16)


def _proj_qkv_kernel(x_ref, g_ref, w_ref, q_ref, kc_ref, vc_ref, ks_ref, vs_ref, kw_ref, vw_ref,
                     *attn_refs, attn_layouts):
    h = _normed(x_ref, g_ref)
    tm = h.shape[0]
    if attn_layouts:
        tok = pl.program_id(1) * tm + lax.broadcasted_iota(jnp.int32, (tm, LANES), 0)
        lane = lax.broadcasted_iota(jnp.int32, (tm, LANES), 1)
        blk_cols = jnp.where(lane == tok // SEL_BLOCK, 1.0, 0.0).astype(BF16)
        one_cols = jnp.where(lax.broadcasted_iota(jnp.int32, (tm, AUG_PAD), 1) < 2, 1.0, 0.0).astype(BF16)
    for g in range(N_KV):
        res = _dot(h, w_ref[:, g * KV_WIDTH:(g + 1) * KV_WIDTH]) * (SCALE * LOG2E if attn_layouts else SCALE)
        if attn_layouts:
            res_t = res.T
            for r in range(HPG):
                q_ref[0, g, r] = res_t[r * HEAD_DIM:(r + 1) * HEAD_DIM, :].astype(BF16)
        else:
            q_ref[0, :, g * KV_WIDTH:(g + 1) * KV_WIDTH] = res.astype(BF16)
    for j, o_ref in enumerate((kc_ref, vc_ref, ks_ref, vs_ref, kw_ref, vw_ref)):
        c0 = NSA_WIDTH + j * KV_WIDTH
        res = _dot(h, w_ref[:, c0:c0 + KV_WIDTH])
        o_ref[0] = res
        if attn_layouts and j in (2, 4):
            extra = [blk_cols, one_cols] if j == 2 else [one_cols]
            for g in range(N_KV):
                attn_refs[j // 2 - 1][0, g] = jnp.concatenate(
                    [res[:, g * HEAD_DIM:(g + 1) * HEAD_DIM].astype(BF16)] + extra, axis=1)
        if attn_layouts and j in (3, 5):
            res_t = res.T.astype(BF16)
            for g in range(N_KV):
                for kt in range(tm // TQ):
                    attn_refs[2 + j // 2 - 1][0, g, kt] = res_t[g * HEAD_DIM:(g + 1) * HEAD_DIM, kt * TQ:(kt + 1) * TQ]


def _proj_gate_kernel(x_ref, g_ref, w_ref, sa_ref, sb_ref, gt_ref):
    h = _normed(x_ref, g_ref)
    for o_ref, base in ((sa_ref, 0), (sb_ref, NSA_WIDTH)):
        for c in range(NSA_WIDTH // 256):
            z = _dot(h, w_ref[:, base + c * 256: base + (c + 1) * 256])
            o_ref[0, :, c * 256:(c + 1) * 256] = z * _sigmoid(z)
    for c in range(2):
        z = _dot(h, w_ref[:, 2 * NSA_WIDTH + c * 256: 2 * NSA_WIDTH + (c + 1) * 256])
        gt_ref[0, :, c * 256:(c + 1) * 256] = _sigmoid(z)


def _proj_glu_kernel(x_ref, g_ref, w_ref, c_ref):
    h = _normed(x_ref, g_ref)
    n = c_ref.shape[-1]
    for c in range(n // 256):
        a = _dot(h, w_ref[:, c * 256:(c + 1) * 256])
        gg = _dot(h, w_ref[:, n + c * 256: n + (c + 1) * 256])
        c_ref[0, :, c * 256:(c + 1) * 256] = a * _sigmoid(gg)


def _projections(x, g_pre, w_qkv, w_gate, w_glu, *, tm, attn_layouts):
    b, s, d = x.shape
    grid = (b, s // tm)
    x_spec = pl.BlockSpec((1, tm, d), lambda bi, i: (bi, i, 0))
    g_spec = pl.BlockSpec((1, d), lambda bi, i: (0, 0))

    def w_spec(w):
        return pl.BlockSpec(w.shape, lambda bi, i: (0, 0))

    def row_spec(width):
        return pl.BlockSpec((1, tm, width), lambda bi, i: (bi, i, 0))

    kv_shape = jax.ShapeDtypeStruct((b, s, KV_WIDTH), F32)
    if attn_layouts:
        assert tm % TQ == 0
        q_shape = jax.ShapeDtypeStruct((b, N_KV, HPG, HEAD_DIM, s), BF16)
        q_spec = pl.BlockSpec((1, N_KV, HPG, HEAD_DIM, tm), lambda bi, i: (bi, 0, 0, 0, i))
        k_shape = lambda w: jax.ShapeDtypeStruct((b, N_KV, s, w), BF16)
        k_spec = lambda w: pl.BlockSpec((1, N_KV, tm, w), lambda bi, i: (bi, 0, i, 0))
        v_shape = jax.ShapeDtypeStruct((b, N_KV, s // TQ, HEAD_DIM, TQ), BF16)
        v_spec = pl.BlockSpec((1, N_KV, tm // TQ, HEAD_DIM, TQ), lambda bi, i: (bi, 0, i, 0, 0))
        extra_shape = [k_shape(K_SEL_AUG), k_shape(K_WIN_AUG), v_shape, v_shape]
        extra_spec = [k_spec(K_SEL_AUG), k_spec(K_WIN_AUG), v_spec, v_spec]
    else:
        q_shape = jax.ShapeDtypeStruct((b, s, NSA_WIDTH), BF16)
        q_spec = row_spec(NSA_WIDTH)
        extra_shape, extra_spec = [], []
    g2 = g_pre.reshape(1, d)
    qkv = pl.pallas_call(
        functools.partial(_proj_qkv_kernel, attn_layouts=attn_layouts),
        grid=grid,
        in_specs=[x_spec, g_spec, w_spec(w_qkv)],
        out_specs=[q_spec] + [row_spec(KV_WIDTH)] * 6 + extra_spec,
        out_shape=[q_shape] + [kv_shape] * 6 + extra_shape,
        compiler_params=_cparams(("parallel", "parallel")),
        name="proj_qkv",
    )(x, g2, w_qkv)
    sa, sb, gt = pl.pallas_call(
        _proj_gate_kernel,
        grid=grid,
        in_specs=[x_spec, g_spec, w_spec(w_gate)],
        out_specs=[row_spec(NSA_WIDTH), row_spec(NSA_WIDTH), row_spec(N_KV * LANES)],
        out_shape=[jax.ShapeDtypeStruct((b, s, NSA_WIDTH), F32)] * 2
        + [jax.ShapeDtypeStruct((b, s, N_KV * LANES), F32)],
        compiler_params=_cparams(("parallel", "parallel")),
        name="proj_gate",
    )(x, g2, w_gate)
    c_in = pl.pallas_call(
        _proj_glu_kernel,
        grid=grid,
        in_specs=[x_spec, g_spec, w_spec(w_glu)],
        out_specs=row_spec(w_glu.shape[1] // 2),
        out_shape=jax.ShapeDtypeStruct((b, s, w_glu.shape[1] // 2), F32),
        compiler_params=_cparams(("parallel", "parallel")),
        name="proj_glu",
    )(x, g2, w_glu)
    return qkv, sa, sb, gt, c_in


def _cmp1_rows_kernel(x_ref, pe_ref, w_ref, pre_ref, pepre_ref):
    w = w_ref[...].reshape(-1, w_ref.shape[-1])
    pre_ref[0] = _dot(x_ref[0].astype(BF16), w)
    pepre_ref[...] = _dot(pe_ref[...], w)


def _cmp1_paged_kernel(*refs, n_in):
    x_refs, (perm_ref, pe_ref, w_ref, pre_ref, pepre_ref) = refs[1:1 + n_in], refs[1 + n_in:]
    perm = perm_ref[...]
    zs = [_dot_nt(perm, r[0].astype(BF16)) for r in x_refs]
    rpp = PAGE_SIZE // CMP_STRIDE
    acc = jnp.zeros((n_in * rpp, w_ref.shape[-1]), F32)
    for c in range(CMP_STRIDE):
        xc = jnp.concatenate([z[c * rpp:(c + 1) * rpp] for z in zs], axis=0)
        acc = acc + _dot(xc.astype(BF16), w_ref[c])
    pre_ref[0] = acc
    pepre_ref[...] = _dot(pe_ref[...], w_ref[...].reshape(-1, w_ref.shape[-1]))


def _cmp2_kernel(pre_ref, pepre_ref, w2_ref, o_ref, *, layout):
    pre = pre_ref[0]
    n_ch = pre.shape[0]
    a = pre[:, :KV_WIDTH]
    b_next = pltpu.roll(pre[:, KV_WIDTH:], n_ch - 1, axis=0)
    pe_bias = pepre_ref[0:1, :KV_WIDTH] + pepre_ref[1:2, KV_WIDTH:]
    z = a + b_next + pe_bias
    hid = 0.5 * z * (1.0 + jnp.tanh(math.sqrt(2.0 / math.pi) * (z + 0.044715 * (z * z * z))))
    out = _dot(hid.astype(BF16), w2_ref[...])
    if layout == "group_rows":
        for g in range(N_KV):
            o_ref[0, g] = out[:, g * HEAD_DIM:(g + 1) * HEAD_DIM].astype(BF16)
    elif layout == "group_cols":
        out_t = out.T.astype(BF16)
        for g in range(N_KV):
            o_ref[0, g] = out_t[g * HEAD_DIM:(g + 1) * HEAD_DIM, :]
    elif layout == "rows":
        o_ref[0] = out.astype(BF16)
    else:
        o_ref[0] = out.T.astype(BF16)


def _cmp_weights(w1, w2, pe):
    eye = jnp.eye(N_KV, dtype=F32)
    halves = w1.reshape(2, CMP_STRIDE, HEAD_DIM, -1)
    hdim = halves.shape[-1]
    wbig = jnp.einsum('acdh,gk->cgdakh', halves, eye)
    wbig = wbig.reshape(CMP_STRIDE, KV_WIDTH, 2 * N_KV * hdim).astype(BF16)
    w2big = jnp.einsum('hd,gk->ghkd', w2, eye).reshape(N_KV * hdim, KV_WIDTH).astype(BF16)
    pe_rows = jnp.broadcast_to(pe.reshape(2, CMP_STRIDE, 1, HEAD_DIM), (2, CMP_STRIDE, N_KV, HEAD_DIM))
    pe8 = jnp.zeros((8, CMP_STRIDE * KV_WIDTH), F32).at[:2].set(pe_rows.reshape(2, -1)).astype(BF16)
    return wbig, w2big, pe8


def _compress(x_view, wts, *, page_table=None, layout):
    wbig, w2big, pe8 = wts
    ncol = wbig.shape[-1]
    kdim = wbig.shape[0] * wbig.shape[1]
    if page_table is None:
        b, n_ch, _ = x_view.shape
        rows = min(n_ch, 256)
        const2 = lambda bi, i: (0, 0)
        pre, pepre = pl.pallas_call(
            _cmp1_rows_kernel, grid=(b, n_ch // rows),
            in_specs=[pl.BlockSpec((1, rows, kdim), lambda bi, i: (bi, i, 0)), pl.BlockSpec(pe8.shape, const2),
                      pl.BlockSpec(wbig.shape, lambda bi, i: (0, 0, 0))],
            out_specs=[pl.BlockSpec((1, rows, ncol), lambda bi, i: (bi, i, 0)), pl.BlockSpec((8, ncol), const2)],
            out_shape=[jax.ShapeDtypeStruct((b, n_ch, ncol), F32), jax.ShapeDtypeStruct((8, ncol), F32)],
            compiler_params=_cparams(("arbitrary", "arbitrary")), name="cmp_stage1")(x_view, pe8, wbig)
    else:
        b, n_pages = page_table.shape
        n_in = CMP_PAGES
        rpp = PAGE_SIZE // CMP_STRIDE
        rows = n_in * rpp
        n_ch = n_pages * rpp
        const2 = lambda bi, i, pt: (0, 0)
        x_specs = [pl.BlockSpec((1, KV_WIDTH, PAGE_SIZE), functools.partial(
            lambda bi, i, pt, p: (pt[bi, i * CMP_PAGES + p], 0, 0), p=p)) for p in range(n_in)]
        tok = np.arange(PAGE_SIZE)
        perm = (tok[None, :] == (tok[:, None] % rpp) * CMP_STRIDE + tok[:, None] // rpp)
        perm = jnp.asarray(perm, BF16)
        pre, pepre = pl.pallas_call(
            functools.partial(_cmp1_paged_kernel, n_in=n_in),
            grid_spec=pltpu.PrefetchScalarGridSpec(
                num_scalar_prefetch=1, grid=(b, n_pages // n_in),
                in_specs=x_specs + [pl.BlockSpec(perm.shape, const2), pl.BlockSpec(pe8.shape, const2),
                                    pl.BlockSpec(wbig.shape, lambda bi, i, pt: (0, 0, 0))],
                out_specs=[pl.BlockSpec((1, rows, ncol), lambda bi, i, pt: (bi, i, 0)),
                           pl.BlockSpec((8, ncol), const2)]),
            out_shape=[jax.ShapeDtypeStruct((b, n_ch, ncol), F32), jax.ShapeDtypeStruct((8, ncol), F32)],
            compiler_params=_cparams(("arbitrary", "arbitrary")), name="cmp_stage1_paged",
        )(page_table, *([x_view] * n_in), perm, pe8, wbig)
    o_dims = {"group_rows": (N_KV, n_ch, HEAD_DIM), "group_cols": (N_KV, HEAD_DIM, n_ch),
              "rows": (n_ch, KV_WIDTH), "cols": (KV_WIDTH, n_ch)}[layout]
    return pl.pallas_call(
        functools.partial(_cmp2_kernel, layout=layout),
        grid=(b,),
        in_specs=[pl.BlockSpec((1, n_ch, ncol), lambda bi: (bi, 0, 0)),
                  pl.BlockSpec((8, ncol), lambda bi: (0, 0)),
                  pl.BlockSpec(w2big.shape, lambda bi: (0, 0))],
        out_specs=pl.BlockSpec((1,) + o_dims, lambda bi: (bi,) + (0,) * len(o_dims)),
        out_shape=jax.ShapeDtypeStruct((b,) + o_dims, BF16),
        compiler_params=_cparams(("parallel",)), name="cmp_stage2",
    )(pre, pepre, w2big)


def _t5_bucket(dist):
    dist = np.maximum(np.asarray(dist, np.int64), 0)
    max_exact = N_BUCKETS // 2
    d32 = np.maximum(dist, 1).astype(np.float32)
    large = max_exact + (np.log(d32 / np.float32(max_exact)) / np.float32(math.log(MAX_DISTANCE / max_exact))
                         * np.float32(N_BUCKETS - max_exact)).astype(np.int32)
    large = np.minimum(large, N_BUCKETS - 1)
    return np.where(dist < max_exact, dist, large).astype(np.int32)


def _bias_lookup(cols, dist):
    bucket = jnp.asarray(_t5_bucket(dist))
    out = jnp.zeros(jnp.broadcast_shapes(cols.shape[1:], bucket.shape), F32)
    for k in range(N_BUCKETS):
        out = jnp.where(bucket == k, cols[k], out)
    return out


def _select_top_blocks(score, jrow, k_sel):
    chosen = jnp.zeros(score.shape, F32)
    work = score
    for _ in range(k_sel):
        m = jnp.max(work, axis=0, keepdims=True)
        idx = jnp.min(jnp.where(work == m, jrow, 1 << 20), axis=0, keepdims=True)
        pick = jrow == idx
        chosen = jnp.where(pick, 1.0, chosen)
        work = jnp.where(pick, PICKED, work)
    return chosen


def _nsa_prompt_kernel(q_ref, gt_ref, sa_ref, kcb_ref, vcbt_ref, ks_ref, vst_ref, kw_ref, vwt_ref,
                       pcd_ref, far_ref, far16_ref, tz0_ref, tz1_ref, covt_ref, o_ref,
                       lc_ref, qs_ref, qw_ref, m_ref, l_ref, acc_ref, *, n_slc):
    i = pl.program_id(2)
    nl = HPG * TQ
    ncp = kcb_ref.shape[2]
    q_t = jnp.concatenate([q_ref[0, 0, r] for r in range(HPG)], axis=1)
    lane_t = lax.broadcasted_iota(jnp.int32, (1, nl), 1) & (TQ - 1)
    qpos = i * TQ + lane_t
    far = far_ref[0, 0:1, :]
    qs_ref[0:HEAD_DIM, :] = q_t
    qs_ref[HEAD_DIM + LANES:, :] = far16_ref[0]
    qw_ref[0:HEAD_DIM, :] = q_t
    qw_ref[HEAD_DIM:, :] = far16_ref[0]

    lc_ref[0:CPT, :] = jnp.zeros((CPT, nl), F32)
    lc_ref[CPT:CPT + ncp, :] = _dot(kcb_ref[0, 0], q_t) + far
    near = pl.ds(pl.multiple_of(i * CPT, CPT), 2 * CPT)
    lc_ref[near, :] = lc_ref[near, :] + pcd_ref[0]
    lc = lc_ref[CPT:CPT + ncp, :]
    cend = lax.broadcasted_iota(jnp.int32, (ncp, 1), 0) * CMP_STRIDE + (CMP_BLOCK - 1)
    valid_c = cend <= qpos
    lm = jnp.where(valid_c, lc, NEG)
    mc = jnp.max(lm, axis=0, keepdims=True)
    ec = jnp.where(valid_c, jnp.exp2(lm - mc), 0.0)
    den = jnp.sum(ec, axis=0, keepdims=True)
    p_c = ec / jnp.where(den > 0, den, 1.0)
    o_c = _dot(vcbt_ref[0, 0], p_c.astype(BF16))

    psum = p_c[:, 0:TQ]
    for r in range(1, HPG):
        psum = psum + p_c[:, r * TQ:(r + 1) * TQ]
    hi, lo = _split_hi_lo(psum)
    imp_t = _dot(covt_ref[...], hi) + _dot(covt_ref[...], lo)
    jrow = lax.broadcasted_iota(jnp.int32, (LANES, TQ), 0)
    qpos_t = qpos[:, 0:TQ]
    cur = qpos_t // SEL_BLOCK
    causal = (jrow * SEL_BLOCK <= qpos_t) & (jrow < n_slc)
    forced = (jrow == 0) | (jrow == cur) | (jrow == cur - 1)
    score = jnp.where(forced, FORCE_SCORE, jnp.where(causal, imp_t, NEG))
    chosen = _select_top_blocks(score, jrow, min(N_SEL, n_slc))
    sel_add = jnp.where(causal & (chosen > 0.5), 0.0, NEG).astype(BF16)
    qs_ref[HEAD_DIM:HEAD_DIM + LANES, :] = jnp.concatenate([sel_add] * HPG, axis=1)

    key_u = lax.broadcasted_iota(jnp.int32, (TQ, 1), 0)
    causal_diag = key_u <= lane_t

    def attend(k_ref, vt_ref, qx_ref, tiles):
        s_list = []
        for kt, bias, mask in tiles:
            s = _dot(k_ref[0, 0, pl.ds(pl.multiple_of(kt * TQ, TQ), TQ), :], qx_ref[...])
            if bias is not None:
                s = s + bias
            s_list.append(s if mask is None else jnp.where(mask, s, NEG))
        m = functools.reduce(jnp.maximum, [jnp.max(s, axis=0, keepdims=True) for s in s_list])
        p_list = [jnp.exp2(s - m) for s in s_list]
        l = functools.reduce(jnp.add, [jnp.sum(p, axis=0, keepdims=True) for p in p_list])
        acc = functools.reduce(jnp.add, [_dot(vt_ref[0, 0, kt], p.astype(BF16))
                                         for (kt, _, _), p in zip(tiles, p_list)])
        return acc, m, l

    m_ref[...] = jnp.full(m_ref.shape, NEG, F32)
    l_ref[...] = jnp.zeros(l_ref.shape, F32)
    acc_ref[...] = jnp.zeros(acc_ref.shape, F32)

    def merge_far(kt0, n):
        acc, m, l = attend(ks_ref, vst_ref, qs_ref, [(kt0 + t, None, None) for t in range(n)])
        m_old = m_ref[...]
        m_new = jnp.maximum(m_old, m)
        a_old = jnp.exp2(m_old - m_new)
        a_new = jnp.exp2(m - m_new)
        l_ref[...] = a_old * l_ref[...] + a_new * l
        acc_ref[...] = a_old * acc_ref[...] + a_new * acc
        m_ref[...] = m_new

    n_far = jnp.maximum(i - 1, 0)

    def far_group(jg, carry):
        merge_far(FAR_GROUP * jg, FAR_GROUP)
        return carry

    lax.fori_loop(0, n_far // FAR_GROUP, far_group, 0)
    rest = n_far % FAR_GROUP
    size = FAR_GROUP // 2
    while size >= 1:
        @pl.when((rest & size) != 0)
        def _(size=size):
            merge_far(n_far - (rest & (2 * size - 1)), size)
        size //= 2

    prev = jnp.maximum(i - 1, 0)
    has_prev = jnp.where(i >= 1, 0.0, NEG)
    acc_n, m_n, l_n = attend(ks_ref, vst_ref, qs_ref, [(prev, tz1_ref[0] + has_prev, None),
                                                       (i, tz0_ref[0], causal_diag)])
    m_old = m_ref[...]
    m_new = jnp.maximum(m_old, m_n)
    a_old = jnp.exp2(m_old - m_new)
    a_new = jnp.exp2(m_n - m_new)
    o_s = (a_old * acc_ref[...] + a_new * acc_n) / (a_old * l_ref[...] + a_new * l_n)

    prev2 = jnp.maximum(i - 2, 0)
    acc_w, _, l_w = attend(kw_ref, vwt_ref, qw_ref, [(prev2, None, (key_u > lane_t) & (i >= 2)),
                                                     (prev, tz1_ref[0] + has_prev, None),
                                                     (i, tz0_ref[0], causal_diag)])
    o_w = acc_w / l_w

    g_t = gt_ref[0].T

    def gate_row(branch):
        return jnp.concatenate([g_t[branch * HPG + r:branch * HPG + r + 1, :] for r in range(HPG)], axis=1)

    o_t = gate_row(0) * o_c + gate_row(1) * o_s + gate_row(2) * o_w
    o_rd = jnp.concatenate([o_t[:, r * TQ:(r + 1) * TQ] for r in range(HPG)], axis=0)
    o_ref[0] = (o_rd.T * sa_ref[0]).astype(BF16)


def _nsa_prompt(q_t, gt, sa, kcb, vcb_t, ks, vs_t, kw, vw_t, rel_bias):
    b, _, _, _, s = q_t.shape
    assert WINDOW == 2 * TQ and s % TQ == 0 and TQ >= MAX_DISTANCE
    nq = s // TQ
    ncp = s // CMP_STRIDE
    n_cmp = ncp - 1
    n_slc = -(-s // SEL_BLOCK)
    assert n_slc <= LANES and ncp >= 2 * CPT
    nl = HPG * TQ
    table = rel_bias.astype(F32)
    uu = np.arange(TQ)[:, None]
    tt = np.arange(TQ)[None, :]

    def per_group(tab):
        rows = tab.shape[1]
        return jnp.transpose(tab.reshape(N_KV, HPG, rows, TQ), (0, 2, 1, 3)).reshape(N_KV, rows, nl)

    cols = table[:, :, None, None] * LOG2E
    far_h = _bias_lookup(table * LOG2E, np.full((N_HEADS,), MAX_DISTANCE))
    far_b = far_h[:, None, None]
    tz0 = per_group(_bias_lookup(cols, (tt - uu)[None]) - far_b)
    tz1 = per_group(_bias_lookup(cols, (TQ + tt - uu)[None]) - far_b)
    far = per_group(jnp.broadcast_to(far_b, (N_HEADS, 8, TQ)))
    far_hi = far_h.astype(BF16)
    far_lo = (far_h - far_hi.astype(F32)).astype(BF16)
    far16 = jnp.zeros((N_HEADS, AUG_PAD, TQ), BF16).at[:, 0].set(far_hi[:, None]).at[:, 1].set(far_lo[:, None])
    far16 = per_group(far16)
    e = np.arange(2 * CPT)[:, None] - CPT
    pcd = per_group(_bias_lookup(cols, (tt - CMP_STRIDE * e - (CMP_BLOCK - 1))[None]) - far_b)
    c = np.arange(ncp)[None, :]
    j = np.arange(LANES)[:, None]
    cov_t = ((c * CMP_STRIDE < (j + 1) * SEL_BLOCK) & (c * CMP_STRIDE + CMP_BLOCK > j * SEL_BLOCK)
             & (c < n_cmp) & (j < n_slc))
    cov_t = jnp.asarray(cov_t, BF16)

    per_g = lambda rows: pl.BlockSpec((1, rows, nl), lambda bi, g, i: (g, 0, 0))
    return pl.pallas_call(
        functools.partial(_nsa_prompt_kernel, n_slc=n_slc),
        grid=(b, N_KV, nq),
        in_specs=[
            pl.BlockSpec((1, 1, HPG, HEAD_DIM, TQ), lambda bi, g, i: (bi, g, 0, 0, i)),
            pl.BlockSpec((1, TQ, LANES), lambda bi, g, i: (bi, i, g)),
            pl.BlockSpec((1, TQ, KV_WIDTH), lambda bi, g, i: (bi, i, g)),
            pl.BlockSpec((1, 1, ncp, HEAD_DIM), lambda bi, g, i: (bi, g, 0, 0)),
            pl.BlockSpec((1, 1, HEAD_DIM, ncp), lambda bi, g, i: (bi, g, 0, 0)),
            pl.BlockSpec((1, 1, s, K_SEL_AUG), lambda bi, g, i: (bi, g, 0, 0)),
            pl.BlockSpec((1, 1, nq, HEAD_DIM, TQ), lambda bi, g, i: (bi, g, 0, 0, 0)),
            pl.BlockSpec((1, 1, s, K_WIN_AUG), lambda bi, g, i: (bi, g, 0, 0)),
            pl.BlockSpec((1, 1, nq, HEAD_DIM, TQ), lambda bi, g, i: (bi, g, 0, 0, 0)),
            per_g(2 * CPT), per_g(8), per_g(AUG_PAD), per_g(TQ), per_g(TQ),
            pl.BlockSpec(cov_t.shape, lambda bi, g, i: (0, 0)),
        ],
        out_specs=pl.BlockSpec((1, TQ, KV_WIDTH), lambda bi, g, i: (bi, i, g)),
        out_shape=jax.ShapeDtypeStruct((b, s, NSA_WIDTH), BF16),
        scratch_shapes=[pltpu.VMEM((CPT + ncp, nl), F32), pltpu.VMEM((K_SEL_AUG, nl), BF16),
                        pltpu.VMEM((K_WIN_AUG, nl), BF16),
                        pltpu.VMEM((1, nl), F32), pltpu.VMEM((1, nl), F32), pltpu.VMEM((HEAD_DIM, nl), F32)],
        compiler_params=_cparams(("parallel", "parallel", "arbitrary")),
        name="nsa_prompt",
    )(q_t, gt, sa, kcb, vcb_t, ks, vs_t, kw, vw_t, pcd, far, far16, tz0, tz1, cov_t)


def _nsa_sample_kernel(*refs, n_pg, n_chunks, n_slc, past, t_new):
    (qbd_ref, kcbt_ref, vcb_ref, bc_ref, cov_ref, rmat_ref) = refs[1:7]
    kpg = refs[7:7 + n_pg]
    vpg = refs[7 + n_pg:7 + 2 * n_pg]
    (bsl_ref, ksn_ref, vsn_ref, bsn_ref, kwc_ref, vwc_ref, kwn_ref, vwn_ref, bw_ref, gt_ref, e4_ref,
     o_ref, sel_ref, m_ref, l_ref, acc_ref, oc_ref) = refs[7 + 2 * n_pg:]
    j = pl.program_id(1)
    qbd = qbd_ref[0]
    nsp = cov_ref.shape[1]
    n_win = kwc_ref.shape[2]

    def softmax_rows(s):
        m = jnp.max(s, axis=-1, keepdims=True)
        e = jnp.where(s > 0.5 * NEG, jnp.exp(s - m), 0.0)
        den = jnp.sum(e, axis=-1, keepdims=True)
        return e / jnp.where(den > 0, den, 1.0)

    def flash_step(s, v, v_is_transposed):
        m_old = m_ref[...]
        m_new = jnp.maximum(m_old, jnp.max(s, axis=-1, keepdims=True))
        alpha = jnp.exp(m_old - m_new)
        p = jnp.exp(s - m_new)
        l_ref[...] = alpha * l_ref[...] + jnp.sum(p, axis=-1, keepdims=True)
        pv = _dot_nt(p.astype(BF16), v) if v_is_transposed else _dot(p.astype(BF16), v)
        acc_ref[...] = alpha * acc_ref[...] + pv
        m_ref[...] = m_new

    @pl.when(j == 0)
    def _():
        p_c = softmax_rows(_dot(qbd, kcbt_ref[0]) + bc_ref[...])
        oc_ref[...] = _dot(p_c.astype(BF16), vcb_ref[0])
        hi, lo = _split_hi_lo(p_c)
        psum = _dot(rmat_ref[...], hi) + _dot(rmat_ref[...], lo)
        hi, lo = _split_hi_lo(psum)
        imp_t = (_dot(hi, cov_ref[...]) + _dot(lo, cov_ref[...])).T
        jrow = lax.broadcasted_iota(jnp.int32, (nsp, LANES), 0)
        lane = lax.broadcasted_iota(jnp.int32, (1, LANES), 1)
        qpos = past + lane % t_new
        cur = qpos // SEL_BLOCK
        causal = (jrow * SEL_BLOCK <= qpos) & (jrow < n_slc)
        forced = (jrow == 0) | (jrow == cur) | (jrow == cur - 1)
        score = jnp.where(forced, FORCE_SCORE, jnp.where(causal, imp_t, NEG))
        chosen = _select_top_blocks(score, jrow, min(N_SEL, n_slc))
        sel = jnp.where(causal & (lane < N_HEADS * t_new) & (chosen > 0.5), 0.0, -1.0).T
        for k in range(nsp // LANES):
            sel_ref[k] = sel[:, k * LANES:(k + 1) * LANES]
        m_ref[...] = jnp.full(m_ref.shape, NEG, F32)
        l_ref[...] = jnp.zeros(l_ref.shape, F32)
        acc_ref[...] = jnp.zeros(acc_ref.shape, F32)

    k_t = jnp.concatenate([r[0] for r in kpg], axis=1).astype(BF16)
    v_t = jnp.concatenate([r[0] for r in vpg], axis=1).astype(BF16)
    far = bsl_ref[:, PAGE_SIZE:2 * PAGE_SIZE]
    last = jnp.where(j == n_chunks - 1, bsl_ref[:, 0:PAGE_SIZE], far)
    bias = jnp.concatenate([far] * (n_pg - 1) + [last], axis=1)
    chunks_per_tile = LANES // (n_pg * (PAGE_SIZE // SEL_BLOCK))
    mask_add = _dot(sel_ref[j // chunks_per_tile].astype(BF16), e4_ref[j % chunks_per_tile])
    flash_step(_dot(qbd, k_t) + bias + mask_add, v_t, True)

    @pl.when(j == n_chunks - 1)
    def _():
        sn = _dot_nt(qbd, ksn_ref[0].astype(BF16)) + bsn_ref[...]
        blk = n_slc - 1
        seln = sel_ref[blk // LANES][:, blk % LANES:blk % LANES + 1]
        flash_step(sn + seln * (-NEG), vsn_ref[0].astype(BF16), False)
        l = l_ref[...]
        o_s = acc_ref[...] / jnp.where(l > 0, l, 1.0)
        sw = jnp.concatenate([_dot(qbd, kwc_ref[0].astype(BF16)), _dot_nt(qbd, kwn_ref[0].astype(BF16))], axis=1)
        p_w = softmax_rows(sw + bw_ref[...]).astype(BF16)
        o_w = _dot_nt(p_w[:, :n_win], vwc_ref[0].astype(BF16)) + _dot(p_w[:, n_win:], vwn_ref[0].astype(BF16))
        gt = gt_ref[0]
        o_ref[0] = gt[:, 0:1] * oc_ref[...] + gt[:, 1:2] * o_s + gt[:, 2:3] * o_w


def _nsa_sample(q, gates, kcb_t, vcb, cache_k_slc, cache_v_slc, page_table, ks_new, vs_new,
                kw_cache, vw_cache, kw_new, vw_new, rel_bias, *, past):
    db, t_new, _ = q.shape
    n_pages = page_table.shape[1]
    assert past == n_pages * PAGE_SIZE and past % SEL_BLOCK == 0 and t_new <= SEL_BLOCK
    assert PAGE_SIZE >= MAX_DISTANCE
    n_cp = kcb_t.shape[2]
    n_slc = -(-(past + t_new) // SEL_BLOCK)
    nsp = -(-n_slc // LANES) * LANES
    n_pg = SAMPLE_PAGES
    n_chunks = n_pages // n_pg
    blocks_per_chunk = n_pg * (PAGE_SIZE // SEL_BLOCK)
    assert LANES % blocks_per_chunk == 0
    n_win = kw_cache.shape[2]
    nl = N_HEADS * t_new
    assert nl <= LANES
    row = np.arange(LANES)
    row_ok = (row < nl)[:, None]
    row_g = np.where(row < nl, row // (HPG * t_new), 0)
    row_t = (row % t_new)[:, None]
    q5 = q.reshape(db, t_new, N_KV, HPG, HEAD_DIM)
    qbd = jnp.einsum('btgrd,gk->bgrtkd', q5.astype(F32), jnp.eye(N_KV, dtype=F32)).reshape(db, nl, KV_WIDTH)
    qbd = jnp.pad(qbd, ((0, 0), (0, LANES - nl), (0, 0))).astype(BF16)

    tab_rows = jnp.pad(jnp.repeat(rel_bias.astype(F32), t_new, axis=1), ((0, 0), (0, LANES - nl)))[:, :, None]

    def bias_tab(dist, valid):
        return jnp.where(jnp.asarray(valid & row_ok), _bias_lookup(tab_rows, dist), NEG)

    qpos = past + row_t
    cblk = np.arange(n_cp)[None, :]
    dist_c = qpos - (cblk * CMP_STRIDE + CMP_BLOCK - 1)
    bc = bias_tab(dist_c, (dist_c >= 0) & (cblk < n_cp - 1))
    kpos = past - PAGE_SIZE + np.arange(PAGE_SIZE)[None, :]
    bsl = jnp.concatenate([bias_tab(qpos - kpos, np.ones((LANES, PAGE_SIZE), bool)),
                           bias_tab(np.full((LANES, PAGE_SIZE), MAX_DISTANCE), np.ones((LANES, PAGE_SIZE), bool))],
                          axis=1)
    u = np.arange(PAGE_SIZE)[None, :]
    new_ok = (u <= row_t) & (u < t_new)
    bsn = bias_tab(row_t - u, new_ok)
    wpos = past - n_win + np.arange(n_win)[None, :]
    dist_w = qpos - wpos
    bw = jnp.concatenate([bias_tab(dist_w, (dist_w >= 0) & (dist_w < WINDOW) & (wpos >= 0)),
                          bias_tab(row_t - u, new_ok & (row_t - u < WINDOW))], axis=1)
    c = np.arange(n_cp)[:, None]
    jb = np.arange(nsp)[None, :]
    cov = ((c * CMP_STRIDE < (jb + 1) * SEL_BLOCK) & (c * CMP_STRIDE + CMP_BLOCK > jb * SEL_BLOCK)
           & (c < n_cp - 1) & (jb < n_slc))
    cov = jnp.asarray(cov, BF16)
    same = (row_g[:, None] == row_g[None, :]) & (row_t == row_t.T) & row_ok & row_ok.T
    rmat = jnp.asarray(same, BF16)
    kk = np.arange(n_pg * PAGE_SIZE)[None, None, :] // SEL_BLOCK
    e4 = np.arange(LANES)[None, :, None] == (np.arange(LANES // blocks_per_chunk)[:, None, None] * blocks_per_chunk + kk)
    e4 = jnp.asarray(np.where(e4, -NEG, 0.0), BF16)
    g5 = gates.reshape(db, t_new, N_KV, LANES)[..., :3 * HPG].reshape(db, t_new, N_KV, 3, HPG)
    gcol = jnp.transpose(g5, (0, 2, 4, 1, 3)).reshape(db, nl, 3)
    gcol = jnp.pad(gcol, ((0, 0), (0, LANES - nl), (0, 5)))

    def pad_new(a):
        return jnp.pad(a, ((0, 0), (0, PAGE_SIZE - t_new), (0, 0)))

    ksn, vsn, kwn, vwn = (pad_new(a) for a in (ks_new, vs_new, kw_new, vw_new))

    per_b = lambda shape: pl.BlockSpec((1,) + shape, lambda b, jc, pt: (b, 0, 0))
    full = lambda a: pl.BlockSpec(a.shape, lambda b, jc, pt: (0,) * a.ndim)
    page = lambda p: pl.BlockSpec((1, KV_WIDTH, PAGE_SIZE), lambda b, jc, pt: (pt[b, jc * SAMPLE_PAGES + p], 0, 0))
    in_specs = ([per_b((LANES, KV_WIDTH)), per_b((KV_WIDTH, n_cp)), per_b((n_cp, KV_WIDTH)),
                 full(bc), full(cov), full(rmat)]
                + [page(p) for p in range(n_pg)] * 2
                + [full(bsl), per_b((PAGE_SIZE, KV_WIDTH)), per_b((PAGE_SIZE, KV_WIDTH)), full(bsn),
                   per_b((KV_WIDTH, n_win)), per_b((KV_WIDTH, n_win)),
                   per_b((PAGE_SIZE, KV_WIDTH)), per_b((PAGE_SIZE, KV_WIDTH)), full(bw), per_b((LANES, 8)), full(e4)])
    o = pl.pallas_call(
        functools.partial(_nsa_sample_kernel, n_pg=n_pg, n_chunks=n_chunks, n_slc=n_slc, past=past, t_new=t_new),
        grid_spec=pltpu.PrefetchScalarGridSpec(
            num_scalar_prefetch=1, grid=(db, n_chunks), in_specs=in_specs,
            out_specs=pl.BlockSpec((1, LANES, KV_WIDTH), lambda b, jc, pt: (b, 0, 0)),
            scratch_shapes=[pltpu.VMEM((nsp // LANES, LANES, LANES), F32), pltpu.VMEM((LANES, 1), F32),
                            pltpu.VMEM((LANES, 1), F32), pltpu.VMEM((LANES, KV_WIDTH), F32),
                            pltpu.VMEM((LANES, KV_WIDTH), F32)]),
        out_shape=jax.ShapeDtypeStruct((db, LANES, KV_WIDTH), F32),
        compiler_params=_cparams(("arbitrary", "arbitrary")),
        name="nsa_sample",
    )(page_table, qbd, kcb_t, vcb, bc, cov, rmat, *([cache_k_slc] * n_pg), *([cache_v_slc] * n_pg),
      bsl, ksn, vsn, bsn, kw_cache, vw_cache, kwn, vwn, bw, gcol, e4)
    o6 = o[:, :nl].reshape(db, N_KV, HPG, t_new, N_KV, HEAD_DIM)
    o_diag = jnp.stack([o6[:, g, :, :, g] for g in range(N_KV)], axis=1)
    return jnp.transpose(o_diag, (0, 3, 1, 2, 4)).reshape(db, t_new, NSA_WIDTH)


def _conv_tail(y, cb_ref, lg_ref, lb_ref, wpw_ref, bpw_ref, sb):
    y = y + cb_ref[...]
    mu = jnp.mean(y, axis=-1, keepdims=True)
    yc = y - mu
    var = jnp.mean(yc * yc, axis=-1, keepdims=True)
    yn = yc * lax.rsqrt(var + LN_EPS) * lg_ref[...] + lb_ref[...]
    act = yn * _sigmoid(yn)
    return ((_dot(act.astype(BF16), wpw_ref[...]) + bpw_ref[...]) * sb).astype(BF16)


def _conv_prompt_kernel(c_ref, halo_ref, init_ref, cw_ref, cb_ref, lg_ref, lb_ref, wpw_ref, bpw_ref, sb_ref,
                        o_ref, full_ref, sh_ref, y_ref, *, ts):
    j = pl.program_id(1)
    full_ref[CONV_HALO:CONV_HALO + ts, :] = c_ref[0]

    @pl.when(j == 0)
    def _():
        full_ref[0:CONV_HALO, :] = init_ref[0]

    @pl.when(j > 0)
    def _():
        full_ref[0:CONV_HALO, :] = halo_ref[0]

    first = CONV_HALO - (CONV_WIDTH - 1)
    span = sh_ref.shape[1]
    for sft in range(1, 8):
        sh_ref[sft - 1] = full_ref[sft:sft + span, :]
    rb = 64
    ch = full_ref.shape[1]
    for c0 in range(0, ch, LANES):
        for r0 in range(0, ts, rb):
            acc = jnp.zeros((rb, LANES), F32)
            for w in range(CONV_WIDTH):
                sft = (first + w) % 8
                base = r0 + first + w - sft
                if sft == 0:
                    x = full_ref[base:base + rb, c0:c0 + LANES]
                else:
                    x = sh_ref[sft - 1, base:base + rb, c0:c0 + LANES]
                acc = acc + x * cw_ref[w:w + 1, c0:c0 + LANES]
            y_ref[r0:r0 + rb, c0:c0 + LANES] = acc
    o_ref[0] = _conv_tail(y_ref[...], cb_ref, lg_ref, lb_ref, wpw_ref, bpw_ref, sb_ref[0])


def _conv_sample_kernel(c_ref, st_ref, cw_ref, cb_ref, lg_ref, lb_ref, wpw_ref, bpw_ref, sb_ref,
                        o_ref, full_ref, y_ref):
    nb, t_new, _ = c_ref.shape
    n_st = st_ref.shape[1]
    full_ref[:, 0:n_st, :] = st_ref[...]
    full_ref[:, n_st:n_st + t_new, :] = c_ref[...]
    first = n_st - (CONV_WIDTH - 1)
    for b in range(nb):
        acc = jnp.zeros((t_new, full_ref.shape[2]), F32)
        for w in range(CONV_WIDTH):
            acc = acc + full_ref[b, first + w:first + w + t_new, :] * cw_ref[w:w + 1, :]
        y_ref[b * t_new:(b + 1) * t_new, :] = acc
    o_ref[...] = _conv_tail(y_ref[...], cb_ref, lg_ref, lb_ref, wpw_ref, bpw_ref, sb_ref[...])


def _conv_params(conv_w, conv_b, ln_g, ln_b, w_pw, b_pw):
    ch = conv_w.shape[1]
    cw = jnp.pad(conv_w, ((0, 32 - CONV_WIDTH), (0, 0)))
    return (cw, conv_b.reshape(1, ch), ln_g.reshape(1, ch), ln_b.reshape(1, ch), w_pw.astype(BF16),
            b_pw.reshape(1, ch))


def _conv_prompt(c_in, init, params, sb, *, ts):
    b, s, ch = c_in.shape
    hb = ts // CONV_HALO
    const = lambda a: pl.BlockSpec(a.shape, lambda bi, j: (0,) * a.ndim)
    return pl.pallas_call(
        functools.partial(_conv_prompt_kernel, ts=ts),
        grid=(b, s // ts),
        in_specs=[pl.BlockSpec((1, ts, ch), lambda bi, j: (bi, j, 0)),
                  pl.BlockSpec((1, CONV_HALO, ch), lambda bi, j: (bi, jnp.maximum(j * hb - 1, 0), 0)),
                  pl.BlockSpec((1, CONV_HALO, ch), lambda bi, j: (bi, 0, 0))]
        + [const(a) for a in params]
        + [pl.BlockSpec((1, ts, ch), lambda bi, j: (bi, j, 0))],
        out_specs=pl.BlockSpec((1, ts, ch), lambda bi, j: (bi, j, 0)),
        out_shape=jax.ShapeDtypeStruct((b, s, ch), BF16),
        scratch_shapes=[pltpu.VMEM((CONV_HALO + ts, ch), F32), pltpu.VMEM((7, CONV_HALO + ts - 8, ch), F32),
                        pltpu.VMEM((ts, ch), F32)],
        compiler_params=_cparams(("parallel", "arbitrary")),
        name="conv_prompt",
    )(c_in, c_in, init, *params, sb)


def _conv_sample(c_in, state, params, sb):
    db, t_new, ch = c_in.shape
    n_st = state.shape[1]
    rows_pad = -(-(n_st + t_new) // 8) * 8
    return pl.pallas_call(
        _conv_sample_kernel,
        out_shape=jax.ShapeDtypeStruct((db * t_new, ch), BF16),
        scratch_shapes=[pltpu.VMEM((db, rows_pad, ch), F32), pltpu.VMEM((db * t_new, ch), F32)],
        compiler_params=pltpu.CompilerParams(vmem_limit_bytes=VMEM_LIMIT),
        name="conv_sample",
    )(c_in, state, *params, sb)


def _out_kernel(x_ref, ma_ref, mb_ref, *rest, gated):
    if gated:
        sa_ref, wa_ref, wb_ref, gp_ref, y_ref = rest
        ma = (ma_ref[0] * sa_ref[0]).astype(BF16)
    else:
        wa_ref, wb_ref, gp_ref, y_ref = rest
        ma = ma_ref[0]
    z = _dot(ma, wa_ref[...]) + _dot(mb_ref[0], wb_ref[...])
    ms = jnp.mean(z * z, axis=-1, keepdims=True)
    y_ref[0] = x_ref[0] + z * lax.rsqrt(ms + RMS_EPS) * gp_ref[...]


def _out_proj(x, ma, mb, sa, w_out, g_post, *, tm):
    b, s, d = x.shape
    na = ma.shape[-1]
    wa = w_out[:na].astype(BF16)
    wb = w_out[na:].astype(BF16)
    row = lambda width: pl.BlockSpec((1, tm, width), lambda bi, i: (bi, i, 0))
    const = lambda a: pl.BlockSpec(a.shape, lambda bi, i: (0, 0))
    gp = g_post.reshape(1, d)
    gate_in, gate_spec = ([sa], [row(na)]) if sa is not None else ([], [])
    return pl.pallas_call(
        functools.partial(_out_kernel, gated=sa is not None),
        grid=(b, s // tm),
        in_specs=[row(d), row(na), row(mb.shape[-1])] + gate_spec + [const(wa), const(wb), const(gp)],
        out_specs=row(d),
        out_shape=jax.ShapeDtypeStruct((b, s, d), F32),
        compiler_params=_cparams(("parallel", "parallel")),
        name="out_proj",
    )(x, ma, mb, *gate_in, wa, wb, gp)


def _split_w_in(w_in):
    d = w_in.shape[0]
    c0 = NSA_WIDTH + 6 * KV_WIDTH
    n_gate = 3 * N_HEADS
    conv_ch = (w_in.shape[1] - c0 - n_gate - NSA_WIDTH) // 3
    w_qkv = w_in[:, :c0].astype(BF16)
    wg = w_in[:, c0:c0 + n_gate].reshape(d, N_KV, HPG, 3)
    wg = jnp.transpose(wg, (0, 1, 3, 2)).reshape(d, N_KV, 3 * HPG)
    wg = jnp.pad(wg, ((0, 0), (0, 0), (0, LANES - 3 * HPG))).reshape(d, N_KV * LANES)
    z_a = w_in[:, c0 + n_gate:c0 + n_gate + NSA_WIDTH]
    glu0 = c0 + n_gate + NSA_WIDTH
    w_glu = w_in[:, glu0:glu0 + 2 * conv_ch].astype(BF16)
    z_b = w_in[:, glu0 + 2 * conv_ch:]
    assert conv_ch == NSA_WIDTH
    w_gate = jnp.concatenate([z_a, z_b, wg], axis=1).astype(BF16)
    return w_qkv, w_gate, w_glu


def _token_minor(cache):
    n, tokens = cache.shape[:2]
    return jnp.transpose(cache, (0, 2, 3, 1)).reshape(n, KV_WIDTH, tokens)


def kernel(x_prompt, x_sample, cache_k_cmp, cache_v_cmp, cache_k_slc, cache_v_slc, cache_k_win, cache_v_win,
           state_conv, page_table, g_pre, w_in, cmp_w1_k, cmp_w2_k, cmp_pe_k, cmp_w1_v, cmp_w2_v, cmp_pe_v,
           rel_bias, conv_w, conv_b, ln_g, ln_b, w_pw, b_pw, w_out, g_post):
    depth = g_pre.shape[0]
    assert depth == 1
    layer = 0
    b, s, d = x_prompt.shape
    db, t_new, _ = x_sample.shape
    past = page_table.shape[1] * PAGE_SIZE
    conv_ch = conv_w.shape[-1]

    w_qkv, w_gate, w_glu = _split_w_in(w_in[layer])
    wts_k = _cmp_weights(cmp_w1_k[layer], cmp_w2_k[layer], cmp_pe_k[layer])
    wts_v = _cmp_weights(cmp_w1_v[layer], cmp_w2_v[layer], cmp_pe_v[layer])
    cparams = _conv_params(conv_w[layer], conv_b[layer], ln_g[layer], ln_b[layer], w_pw[layer], b_pw[layer])
    chunk_w = CMP_STRIDE * KV_WIDTH

    (q_t, kc, vc, ks, vs, kw, vw, ks_g, kw_g, vs_t, vw_t), sa, sb, gt, c_in = _projections(
        x_prompt, g_pre[layer], w_qkv, w_gate, w_glu, tm=512, attn_layouts=True)
    n_ch = s // CMP_STRIDE
    kcb = _compress(kc[:, :n_ch * CMP_STRIDE].reshape(b, n_ch, chunk_w), wts_k, layout="group_rows")
    vcb_t = _compress(vc[:, :n_ch * CMP_STRIDE].reshape(b, n_ch, chunk_w), wts_v, layout="group_cols")
    ma = _nsa_prompt(q_t, gt, sa, kcb, vcb_t, ks_g, vs_t, kw_g, vw_t, rel_bias)
    mb = _conv_prompt(c_in, jnp.zeros((b, CONV_HALO, conv_ch), F32), cparams, sb, ts=256)
    y_prompt = _out_proj(x_prompt, ma, mb, None, w_out[layer], g_post[layer], tm=512)
    n_keep = min(WINDOW, s)
    kv5 = lambda a: a.reshape(1, a.shape[0], a.shape[1], N_KV, HEAD_DIM)
    outs_p = (kv5(kc), kv5(vc), kv5(ks), kv5(vs), kv5(kw[:, -n_keep:]), kv5(vw[:, -n_keep:]),
              c_in[None, :, -(CONV_WIDTH - 1):])

    xs = x_sample.reshape(1, db * t_new, d)
    (q_s, kc_s, vc_s, ks_s, vs_s, kw_s, vw_s), sa_s, sb_s, gt_s, c_s = _projections(
        xs, g_pre[layer], w_qkv, w_gate, w_glu, tm=db * t_new, attn_layouts=False)
    tok = lambda a: a.reshape(db, t_new, a.shape[-1])
    kcb_s = _compress(_token_minor(cache_k_cmp[layer]), wts_k, page_table=page_table, layout="cols")
    vcb_s = _compress(_token_minor(cache_v_cmp[layer]), wts_v, page_table=page_table, layout="rows")
    o_a = _nsa_sample(tok(q_s), tok(gt_s), kcb_s, vcb_s,
                      _token_minor(cache_k_slc[layer]), _token_minor(cache_v_slc[layer]), page_table,
                      tok(ks_s), tok(vs_s), _token_minor(cache_k_win[layer]), _token_minor(cache_v_win[layer]),
                      tok(kw_s), tok(vw_s), rel_bias, past=past)
    mb_s = _conv_sample(tok(c_s), state_conv[layer], cparams, sb_s[0])
    y_sample = _out_proj(xs, o_a.reshape(1, db * t_new, NSA_WIDTH), mb_s[None], sa_s, w_out[layer],
                         g_post[layer], tm=db * t_new).reshape(db, t_new, d)
    n_keep_s = min(WINDOW, past + t_new)
    kv5s = lambda a: a.reshape(1, db, t_new, N_KV, HEAD_DIM)
    win = lambda cache, new: jnp.concatenate(
        [cache[layer], new.reshape(db, t_new, N_KV, HEAD_DIM)], axis=1)[None, :, -n_keep_s:]
    conv_s = jnp.concatenate([state_conv[layer], tok(c_s)], axis=1)[None, :, -(CONV_WIDTH - 1):]
    outs_s = (kv5s(kc_s), kv5s(vc_s), kv5s(ks_s), kv5s(vs_s), win(cache_k_win, kw_s), win(cache_v_win, vw_s), conv_s)
    return (y_prompt, y_sample) + outs_p + outs_s
```

```python
import functools
import math

import numpy as np
import jax
import jax.numpy as jnp
from jax import lax
from jax.experimental import pallas as pl
from jax.experimental.pallas import tpu as pltpu

F32 = jnp.float32
BF16 = jnp.bfloat16

HEAD_DIM = 64
N_KV = 4
HPG = 4
N_HEADS = N_KV * HPG
KV_WIDTH = N_KV * HEAD_DIM
NSA_WIDTH = N_HEADS * HEAD_DIM
CMP_BLOCK = 32
CMP_STRIDE = 16
SEL_BLOCK = 64
N_SEL = 16
WINDOW = 512
CONV_WIDTH = 31
N_BUCKETS = 32
MAX_DISTANCE = 128
FORCE_SCORE = 1e6
RMS_EPS = 1e-6
LN_EPS = 1e-5
PAGE_SIZE = 128
SCALE = HEAD_DIM ** -0.5

NEG = -1e30
PICKED = -3e38
LANES = 128
VMEM_LIMIT = 56 * 1024 * 1024

LOG2E = 1.4426950408889634
TQ = 256
FAR_GROUP = 4
PIPE_AHEAD = 2
AUG_PAD = 16
K_WIN_AUG = HEAD_DIM + AUG_PAD
K_SEL_AUG = HEAD_DIM + LANES + AUG_PAD
CPT = TQ // CMP_STRIDE
CONV_HALO = 32
SAMPLE_PAGES = 32
SAMPLE_PIECE_PAGES = 8
CMP_PAGES = 32


def _cparams(sem):
    return pltpu.CompilerParams(dimension_semantics=sem, vmem_limit_bytes=VMEM_LIMIT)


def _dot(a, b):
    return jnp.dot(a, b, preferred_element_type=F32)


def _dot_nt(a, b):
    return lax.dot_general(a, b, (((1,), (1,)), ((), ())), preferred_element_type=F32)


def _sigmoid(x):
    return 1.0 / (1.0 + jnp.exp(-x))


def _split_hi_lo(x):
    hi = x.astype(BF16)
    lo = (x - hi.astype(F32)).astype(BF16)
    return hi, lo


def _normed(x_ref, g_ref):
    x = x_ref[0]
    ms = jnp.mean(x * x, axis=-1, keepdims=True)
    return (x * lax.rsqrt(ms + RMS_EPS) * g_ref[...]).astype(BF16)


def _proj_qkv_kernel(x_ref, g_ref, w_ref, q_ref, kc_ref, vc_ref, ks_ref, vs_ref, kw_ref, vw_ref,
                     *attn_refs, attn_layouts):
    h = _normed(x_ref, g_ref)
    tm = h.shape[0]
    if attn_layouts:
        tok = pl.program_id(1) * tm + lax.broadcasted_iota(jnp.int32, (tm, LANES), 0)
        lane = lax.broadcasted_iota(jnp.int32, (tm, LANES), 1)
        blk_cols = jnp.where(lane == tok // SEL_BLOCK, 1.0, 0.0).astype(BF16)
        one_cols = jnp.where(lax.broadcasted_iota(jnp.int32, (tm, AUG_PAD), 1) < 2, 1.0, 0.0).astype(BF16)
    for g in range(N_KV):
        res = _dot(h, w_ref[:, g * KV_WIDTH:(g + 1) * KV_WIDTH]) * (SCALE * LOG2E if attn_layouts else SCALE)
        if attn_layouts:
            res_t = res.T
            for r in range(HPG):
                q_ref[0, g, r] = res_t[r * HEAD_DIM:(r + 1) * HEAD_DIM, :].astype(BF16)
        else:
            q_ref[0, :, g * KV_WIDTH:(g + 1) * KV_WIDTH] = res.astype(BF16)
    for j, o_ref in enumerate((kc_ref, vc_ref, ks_ref, vs_ref, kw_ref, vw_ref)):
        c0 = NSA_WIDTH + j * KV_WIDTH
        res = _dot(h, w_ref[:, c0:c0 + KV_WIDTH])
        o_ref[0] = res
        if attn_layouts and j in (2, 4):
            extra = [blk_cols, one_cols] if j == 2 else [one_cols]
            for g in range(N_KV):
                attn_refs[j // 2 - 1][0, g] = jnp.concatenate(
                    [res[:, g * HEAD_DIM:(g + 1) * HEAD_DIM].astype(BF16)] + extra, axis=1)
        if attn_layouts and j in (3, 5):
            res_t = res.T.astype(BF16)
            for g in range(N_KV):
                for kt in range(tm // TQ):
                    attn_refs[2 + j // 2 - 1][0, g, kt] = res_t[g * HEAD_DIM:(g + 1) * HEAD_DIM, kt * TQ:(kt + 1) * TQ]


def _proj_gate_kernel(x_ref, g_ref, w_ref, sa_ref, sb_ref, gt_ref):
    h = _normed(x_ref, g_ref)
    for o_ref, base in ((sa_ref, 0), (sb_ref, NSA_WIDTH)):
        for c in range(NSA_WIDTH // 256):
            z = _dot(h, w_ref[:, base + c * 256: base + (c + 1) * 256])
            o_ref[0, :, c * 256:(c + 1) * 256] = z * _sigmoid(z)
    for c in range(2):
        z = _dot(h, w_ref[:, 2 * NSA_WIDTH + c * 256: 2 * NSA_WIDTH + (c + 1) * 256])
        gt_ref[0, :, c * 256:(c + 1) * 256] = _sigmoid(z)


def _proj_glu_kernel(x_ref, g_ref, w_ref, c_ref):
    h = _normed(x_ref, g_ref)
    n = c_ref.shape[-1]
    for c in range(n // 256):
        a = _dot(h, w_ref[:, c * 256:(c + 1) * 256])
        gg = _dot(h, w_ref[:, n + c * 256: n + (c + 1) * 256])
        c_ref[0, :, c * 256:(c + 1) * 256] = a * _sigmoid(gg)


def _projections(x, g_pre, w_qkv, w_gate, w_glu, *, tm, attn_layouts):
    b, s, d = x.shape
    grid = (b, s // tm)
    x_spec = pl.BlockSpec((1, tm, d), lambda bi, i: (bi, i, 0))
    g_spec = pl.BlockSpec((1, d), lambda bi, i: (0, 0))

    def w_spec(w):
        return pl.BlockSpec(w.shape, lambda bi, i: (0, 0))

    def row_spec(width):
        return pl.BlockSpec((1, tm, width), lambda bi, i: (bi, i, 0))

    kv_shape = jax.ShapeDtypeStruct((b, s, KV_WIDTH), F32)
    if attn_layouts:
        assert tm % TQ == 0
        q_shape = jax.ShapeDtypeStruct((b, N_KV, HPG, HEAD_DIM, s), BF16)
        q_spec = pl.BlockSpec((1, N_KV, HPG, HEAD_DIM, tm), lambda bi, i: (bi, 0, 0, 0, i))
        k_shape = lambda w: jax.ShapeDtypeStruct((b, N_KV, s, w), BF16)
        k_spec = lambda w: pl.BlockSpec((1, N_KV, tm, w), lambda bi, i: (bi, 0, i, 0))
        v_shape = jax.ShapeDtypeStruct((b, N_KV, s // TQ, HEAD_DIM, TQ), BF16)
        v_spec = pl.BlockSpec((1, N_KV, tm // TQ, HEAD_DIM, TQ), lambda bi, i: (bi, 0, i, 0, 0))
        extra_shape = [k_shape(K_SEL_AUG), k_shape(K_WIN_AUG), v_shape, v_shape]
        extra_spec = [k_spec(K_SEL_AUG), k_spec(K_WIN_AUG), v_spec, v_spec]
    else:
        q_shape = jax.ShapeDtypeStruct((b, s, NSA_WIDTH), BF16)
        q_spec = row_spec(NSA_WIDTH)
        extra_shape, extra_spec = [], []
    g2 = g_pre.reshape(1, d)
    qkv = pl.pallas_call(
        functools.partial(_proj_qkv_kernel, attn_layouts=attn_layouts),
        grid=grid,
        in_specs=[x_spec, g_spec, w_spec(w_qkv)],
        out_specs=[q_spec] + [row_spec(KV_WIDTH)] * 6 + extra_spec,
        out_shape=[q_shape] + [kv_shape] * 6 + extra_shape,
        compiler_params=_cparams(("parallel", "parallel")),
        name="proj_qkv",
    )(x, g2, w_qkv)
    sa, sb, gt = pl.pallas_call(
        _proj_gate_kernel,
        grid=grid,
        in_specs=[x_spec, g_spec, w_spec(w_gate)],
        out_specs=[row_spec(NSA_WIDTH), row_spec(NSA_WIDTH), row_spec(N_KV * LANES)],
        out_shape=[jax.ShapeDtypeStruct((b, s, NSA_WIDTH), F32)] * 2
        + [jax.ShapeDtypeStruct((b, s, N_KV * LANES), F32)],
        compiler_params=_cparams(("parallel", "parallel")),
        name="proj_gate",
    )(x, g2, w_gate)
    c_in = pl.pallas_call(
        _proj_glu_kernel,
        grid=grid,
        in_specs=[x_spec, g_spec, w_spec(w_glu)],
        out_specs=row_spec(w_glu.shape[1] // 2),
        out_shape=jax.ShapeDtypeStruct((b, s, w_glu.shape[1] // 2), F32),
        compiler_params=_cparams(("parallel", "parallel")),
        name="proj_glu",
    )(x, g2, w_glu)
    return qkv, sa, sb, gt, c_in


def _cmp1_rows_kernel(x_ref, pe_ref, w_ref, pre_ref, pepre_ref):
    w = w_ref[...].reshape(-1, w_ref.shape[-1])
    pre_ref[0] = _dot(x_ref[0].astype(BF16), w)
    pepre_ref[...] = _dot(pe_ref[...], w)


def _cmp1_paged_kernel(*refs, n_in):
    x_refs, (perm_ref, pe_ref, w_ref, pre_ref, pepre_ref) = refs[1:1 + n_in], refs[1 + n_in:]
    perm = perm_ref[...]
    zs = [_dot_nt(perm, r[0].astype(BF16)) for r in x_refs]
    rpp = PAGE_SIZE // CMP_STRIDE
    acc = jnp.zeros((n_in * rpp, w_ref.shape[-1]), F32)
    for c in range(CMP_STRIDE):
        xc = jnp.concatenate([z[c * rpp:(c + 1) * rpp] for z in zs], axis=0)
        acc = acc + _dot(xc.astype(BF16), w_ref[c])
    pre_ref[0] = acc
    pepre_ref[...] = _dot(pe_ref[...], w_ref[...].reshape(-1, w_ref.shape[-1]))


def _cmp2_kernel(pre_ref, pepre_ref, w2_ref, o_ref, *, layout):
    pre = pre_ref[0]
    n_ch = pre.shape[0]
    a = pre[:, :KV_WIDTH]
    b_next = pltpu.roll(pre[:, KV_WIDTH:], n_ch - 1, axis=0)
    pe_bias = pepre_ref[0:1, :KV_WIDTH] + pepre_ref[1:2, KV_WIDTH:]
    z = a + b_next + pe_bias
    hid = 0.5 * z * (1.0 + jnp.tanh(math.sqrt(2.0 / math.pi) * (z + 0.044715 * (z * z * z))))
    out = _dot(hid.astype(BF16), w2_ref[...])
    if layout == "group_rows":
        for g in range(N_KV):
            o_ref[0, g] = out[:, g * HEAD_DIM:(g + 1) * HEAD_DIM].astype(BF16)
    elif layout == "group_cols":
        out_t = out.T.astype(BF16)
        for g in range(N_KV):
            o_ref[0, g] = out_t[g * HEAD_DIM:(g + 1) * HEAD_DIM, :]
    elif layout == "rows":
        o_ref[0] = out.astype(BF16)
    else:
        o_ref[0] = out.T.astype(BF16)


def _cmp_weights(w1, w2, pe):
    eye = jnp.eye(N_KV, dtype=F32)
    halves = w1.reshape(2, CMP_STRIDE, HEAD_DIM, -1)
    hdim = halves.shape[-1]
    wbig = jnp.einsum('acdh,gk->cgdakh', halves, eye)
    wbig = wbig.reshape(CMP_STRIDE, KV_WIDTH, 2 * N_KV * hdim).astype(BF16)
    w2big = jnp.einsum('hd,gk->ghkd', w2, eye).reshape(N_KV * hdim, KV_WIDTH).astype(BF16)
    pe_rows = jnp.broadcast_to(pe.reshape(2, CMP_STRIDE, 1, HEAD_DIM), (2, CMP_STRIDE, N_KV, HEAD_DIM))
    pe8 = jnp.zeros((8, CMP_STRIDE * KV_WIDTH), F32).at[:2].set(pe_rows.reshape(2, -1)).astype(BF16)
    return wbig, w2big, pe8


def _compress(x_view, wts, *, page_table=None, layout):
    wbig, w2big, pe8 = wts
    ncol = wbig.shape[-1]
    kdim = wbig.shape[0] * wbig.shape[1]
    if page_table is None:
        b, n_ch, _ = x_view.shape
        rows = min(n_ch, 256)
        const2 = lambda bi, i: (0, 0)
        pre, pepre = pl.pallas_call(
            _cmp1_rows_kernel, grid=(b, n_ch // rows),
            in_specs=[pl.BlockSpec((1, rows, kdim), lambda bi, i: (bi, i, 0)), pl.BlockSpec(pe8.shape, const2),
                      pl.BlockSpec(wbig.shape, lambda bi, i: (0, 0, 0))],
            out_specs=[pl.BlockSpec((1, rows, ncol), lambda bi, i: (bi, i, 0)), pl.BlockSpec((8, ncol), const2)],
            out_shape=[jax.ShapeDtypeStruct((b, n_ch, ncol), F32), jax.ShapeDtypeStruct((8, ncol), F32)],
            compiler_params=_cparams(("arbitrary", "arbitrary")), name="cmp_stage1")(x_view, pe8, wbig)
    else:
        b, n_pages = page_table.shape
        n_in = CMP_PAGES
        rpp = PAGE_SIZE // CMP_STRIDE
        rows = n_in * rpp
        n_ch = n_pages * rpp
        const2 = lambda bi, i, pt: (0, 0)
        x_specs = [pl.BlockSpec((1, KV_WIDTH, PAGE_SIZE), functools.partial(
            lambda bi, i, pt, p: (pt[bi, i * CMP_PAGES + p], 0, 0), p=p)) for p in range(n_in)]
        tok = np.arange(PAGE_SIZE)
        perm = (tok[None, :] == (tok[:, None] % rpp) * CMP_STRIDE + tok[:, None] // rpp)
        perm = jnp.asarray(perm, BF16)
        pre, pepre = pl.pallas_call(
            functools.partial(_cmp1_paged_kernel, n_in=n_in),
            grid_spec=pltpu.PrefetchScalarGridSpec(
                num_scalar_prefetch=1, grid=(b, n_pages // n_in),
                in_specs=x_specs + [pl.BlockSpec(perm.shape, const2), pl.BlockSpec(pe8.shape, const2),
                                    pl.BlockSpec(wbig.shape, lambda bi, i, pt: (0, 0, 0))],
                out_specs=[pl.BlockSpec((1, rows, ncol), lambda bi, i, pt: (bi, i, 0)),
                           pl.BlockSpec((8, ncol), const2)]),
            out_shape=[jax.ShapeDtypeStruct((b, n_ch, ncol), F32), jax.ShapeDtypeStruct((8, ncol), F32)],
            compiler_params=_cparams(("arbitrary", "arbitrary")), name="cmp_stage1_paged",
        )(page_table, *([x_view] * n_in), perm, pe8, wbig)
    o_dims = {"group_rows": (N_KV, n_ch, HEAD_DIM), "group_cols": (N_KV, HEAD_DIM, n_ch),
              "rows": (n_ch, KV_WIDTH), "cols": (KV_WIDTH, n_ch)}[layout]
    return pl.pallas_call(
        functools.partial(_cmp2_kernel, layout=layout),
        grid=(b,),
        in_specs=[pl.BlockSpec((1, n_ch, ncol), lambda bi: (bi, 0, 0)),
                  pl.BlockSpec((8, ncol), lambda bi: (0, 0)),
                  pl.BlockSpec(w2big.shape, lambda bi: (0, 0))],
        out_specs=pl.BlockSpec((1,) + o_dims, lambda bi: (bi,) + (0,) * len(o_dims)),
        out_shape=jax.ShapeDtypeStruct((b,) + o_dims, BF16),
        compiler_params=_cparams(("parallel",)), name="cmp_stage2",
    )(pre, pepre, w2big)


def _t5_bucket(dist):
    dist = np.maximum(np.asarray(dist, np.int64), 0)
    max_exact = N_BUCKETS // 2
    d32 = np.maximum(dist, 1).astype(np.float32)
    large = max_exact + (np.log(d32 / np.float32(max_exact)) / np.float32(math.log(MAX_DISTANCE / max_exact))
                         * np.float32(N_BUCKETS - max_exact)).astype(np.int32)
    large = np.minimum(large, N_BUCKETS - 1)
    return np.where(dist < max_exact, dist, large).astype(np.int32)


def _bias_lookup(cols, dist):
    bucket = jnp.asarray(_t5_bucket(dist))
    out = jnp.zeros(jnp.broadcast_shapes(cols.shape[1:], bucket.shape), F32)
    for k in range(N_BUCKETS):
        out = jnp.where(bucket == k, cols[k], out)
    return out


def _select_top_blocks(score, jrow, k_sel):
    chosen = jnp.zeros(score.shape, F32)
    work = score
    for _ in range(k_sel):
        m = jnp.max(work, axis=0, keepdims=True)
        idx = jnp.min(jnp.where(work == m, jrow, 1 << 20), axis=0, keepdims=True)
        pick = jrow == idx
        chosen = jnp.where(pick, 1.0, chosen)
        work = jnp.where(pick, PICKED, work)
    return chosen


def _nsa_prompt_kernel(q_ref, gt_ref, sa_ref, kcb_ref, vcbt_ref, ks_ref, vst_ref, kw_ref, vwt_ref,
                       pcd_ref, far_ref, far16_ref, tz0_ref, tz1_ref, covt_ref, o_ref,
                       lc_ref, qs_ref, qw_ref, m_ref, l_ref, acc_ref, *, n_slc):
    i = pl.program_id(2)
    nl = HPG * TQ
    ncp = kcb_ref.shape[2]
    q_t = jnp.concatenate([q_ref[0, 0, r] for r in range(HPG)], axis=1)
    lane_t = lax.broadcasted_iota(jnp.int32, (1, nl), 1) & (TQ - 1)
    qpos = i * TQ + lane_t
    far = far_ref[0, 0:1, :]
    qs_ref[0:HEAD_DIM, :] = q_t
    qs_ref[HEAD_DIM + LANES:, :] = far16_ref[0]
    qw_ref[0:HEAD_DIM, :] = q_t
    qw_ref[HEAD_DIM:, :] = far16_ref[0]

    lc_ref[0:CPT, :] = jnp.zeros((CPT, nl), F32)
    lc_ref[CPT:CPT + ncp, :] = _dot(kcb_ref[0, 0], q_t) + far
    near = pl.ds(pl.multiple_of(i * CPT, CPT), 2 * CPT)
    lc_ref[near, :] = lc_ref[near, :] + pcd_ref[0]
    lc = lc_ref[CPT:CPT + ncp, :]
    cend = lax.broadcasted_iota(jnp.int32, (ncp, 1), 0) * CMP_STRIDE + (CMP_BLOCK - 1)
    valid_c = cend <= qpos
    lm = jnp.where(valid_c, lc, NEG)
    mc = jnp.max(lm, axis=0, keepdims=True)
    ec = jnp.where(valid_c, jnp.exp2(lm - mc), 0.0)
    den = jnp.sum(ec, axis=0, keepdims=True)
    p_c = ec / jnp.where(den > 0, den, 1.0)
    o_c = _dot(vcbt_ref[0, 0], p_c.astype(BF16))

    psum = p_c[:, 0:TQ]
    for r in range(1, HPG):
        psum = psum + p_c[:, r * TQ:(r + 1) * TQ]
    hi, lo = _split_hi_lo(psum)
    imp_t = _dot(covt_ref[...], hi) + _dot(covt_ref[...], lo)
    jrow = lax.broadcasted_iota(jnp.int32, (LANES, TQ), 0)
    qpos_t = qpos[:, 0:TQ]
    cur = qpos_t // SEL_BLOCK
    causal = (jrow * SEL_BLOCK <= qpos_t) & (jrow < n_slc)
    forced = (jrow == 0) | (jrow == cur) | (jrow == cur - 1)
    score = jnp.where(forced, FORCE_SCORE, jnp.where(causal, imp_t, NEG))
    chosen = _select_top_blocks(score, jrow, min(N_SEL, n_slc))
    sel_add = jnp.where(causal & (chosen > 0.5), 0.0, NEG).astype(BF16)
    qs_ref[HEAD_DIM:HEAD_DIM + LANES, :] = jnp.concatenate([sel_add] * HPG, axis=1)

    key_u = lax.broadcasted_iota(jnp.int32, (TQ, 1), 0)
    causal_diag = key_u <= lane_t

    def run_pieces(specs):
        def logits(t):
            k_ref, _, qx_ref, kt, bias, mask = specs[t]
            s = _dot(k_ref[0, 0, pl.ds(pl.multiple_of(kt * TQ, TQ), TQ), :], qx_ref[...])
            if bias is not None:
                s = s + bias
            return s if mask is None else jnp.where(mask, s, NEG)

        n = len(specs)
        s = {t: logits(t) for t in range(min(PIPE_AHEAD, n))}
        pieces = []
        for t in range(n):
            m = jnp.max(s[t], axis=0, keepdims=True)
            p = jnp.exp2(s.pop(t) - m)
            if t + PIPE_AHEAD < n:
                s[t + PIPE_AHEAD] = logits(t + PIPE_AHEAD)
            _, vt_ref, _, kt, _, _ = specs[t]
            pieces.append((_dot(vt_ref[0, 0, kt], p.astype(BF16)), m, jnp.sum(p, axis=0, keepdims=True)))
        return pieces

    def merge(pieces):
        m_new = functools.reduce(jnp.maximum, [m for _, m, _ in pieces])
        scales = [jnp.exp2(m - m_new) for _, m, _ in pieces]
        acc = functools.reduce(jnp.add, [a * acc for a, (acc, _, _) in zip(scales, pieces)])
        l = functools.reduce(jnp.add, [a * l for a, (_, _, l) in zip(scales, pieces)])
        return acc, m_new, l

    m_ref[...] = jnp.full(m_ref.shape, NEG, F32)
    l_ref[...] = jnp.zeros(l_ref.shape, F32)
    acc_ref[...] = jnp.zeros(acc_ref.shape, F32)

    def state():
        return acc_ref[...], m_ref[...], l_ref[...]

    def merge_far(kt0, n):
        acc, m, l = merge([state()] + run_pieces([(ks_ref, vst_ref, qs_ref, kt0 + t, None, None) for t in range(n)]))
        acc_ref[...] = acc
        m_ref[...] = m
        l_ref[...] = l

    n_far = jnp.maximum(i - 1, 0)

    def far_group(jg, carry):
        merge_far(FAR_GROUP * jg, FAR_GROUP)
        return carry

    lax.fori_loop(0, n_far // FAR_GROUP, far_group, 0)
    rest = n_far % FAR_GROUP
    size = FAR_GROUP // 2
    while size >= 1:
        @pl.when((rest & size) != 0)
        def _(size=size):
            merge_far(n_far - (rest & (2 * size - 1)), size)
        size //= 2

    prev = jnp.maximum(i - 1, 0)
    prev2 = jnp.maximum(i - 2, 0)
    tz1 = tz1_ref[0] + jnp.where(i >= 1, 0.0, NEG)
    tz0 = tz0_ref[0]
    near = run_pieces([(kw_ref, vwt_ref, qw_ref, prev2, None, (key_u > lane_t) & (i >= 2)),
                       (kw_ref, vwt_ref, qw_ref, prev, tz1, None),
                       (kw_ref, vwt_ref, qw_ref, i, tz0, causal_diag),
                       (ks_ref, vst_ref, qs_ref, prev, tz1, None),
                       (ks_ref, vst_ref, qs_ref, i, tz0, causal_diag)])
    acc_w, _, l_w = merge(near[:3])
    o_w = acc_w / l_w
    acc_s, _, l_s = merge([state()] + near[3:])
    o_s = acc_s / l_s

    g_t = gt_ref[0].T

    def gate_row(branch):
        return jnp.concatenate([g_t[branch * HPG + r:branch * HPG + r + 1, :] for r in range(HPG)], axis=1)

    o_t = gate_row(0) * o_c + gate_row(1) * o_s + gate_row(2) * o_w
    o_rd = jnp.concatenate([o_t[:, r * TQ:(r + 1) * TQ] for r in range(HPG)], axis=0)
    o_ref[0] = (o_rd.T * sa_ref[0]).astype(BF16)


def _nsa_prompt(q_t, gt, sa, kcb, vcb_t, ks, vs_t, kw, vw_t, rel_bias):
    b, _, _, _, s = q_t.shape
    assert WINDOW == 2 * TQ and s % TQ == 0 and TQ >= MAX_DISTANCE
    nq = s // TQ
    ncp = s // CMP_STRIDE
    n_cmp = ncp - 1
    n_slc = -(-s // SEL_BLOCK)
    assert n_slc <= LANES and ncp >= 2 * CPT
    nl = HPG * TQ
    table = rel_bias.astype(F32)
    uu = np.arange(TQ)[:, None]
    tt = np.arange(TQ)[None, :]

    def per_group(tab):
        rows = tab.shape[1]
        return jnp.transpose(tab.reshape(N_KV, HPG, rows, TQ), (0, 2, 1, 3)).reshape(N_KV, rows, nl)

    cols = table[:, :, None, None] * LOG2E
    far_h = _bias_lookup(table * LOG2E, np.full((N_HEADS,), MAX_DISTANCE))
    far_b = far_h[:, None, None]
    tz0 = per_group(_bias_lookup(cols, (tt - uu)[None]) - far_b)
    tz1 = per_group(_bias_lookup(cols, (TQ + tt - uu)[None]) - far_b)
    far = per_group(jnp.broadcast_to(far_b, (N_HEADS, 8, TQ)))
    far_hi = far_h.astype(BF16)
    far_lo = (far_h - far_hi.astype(F32)).astype(BF16)
    far16 = jnp.zeros((N_HEADS, AUG_PAD, TQ), BF16).at[:, 0].set(far_hi[:, None]).at[:, 1].set(far_lo[:, None])
    far16 = per_group(far16)
    e = np.arange(2 * CPT)[:, None] - CPT
    pcd = per_group(_bias_lookup(cols, (tt - CMP_STRIDE * e - (CMP_BLOCK - 1))[None]) - far_b)
    c = np.arange(ncp)[None, :]
    j = np.arange(LANES)[:, None]
    cov_t = ((c * CMP_STRIDE < (j + 1) * SEL_BLOCK) & (c * CMP_STRIDE + CMP_BLOCK > j * SEL_BLOCK)
             & (c < n_cmp) & (j < n_slc))
    cov_t = jnp.asarray(cov_t, BF16)

    per_g = lambda rows: pl.BlockSpec((1, rows, nl), lambda bi, g, i: (g, 0, 0))
    return pl.pallas_call(
        functools.partial(_nsa_prompt_kernel, n_slc=n_slc),
        grid=(b, N_KV, nq),
        in_specs=[
            pl.BlockSpec((1, 1, HPG, HEAD_DIM, TQ), lambda bi, g, i: (bi, g, 0, 0, i)),
            pl.BlockSpec((1, TQ, LANES), lambda bi, g, i: (bi, i, g)),
            pl.BlockSpec((1, TQ, KV_WIDTH), lambda bi, g, i: (bi, i, g)),
            pl.BlockSpec((1, 1, ncp, HEAD_DIM), lambda bi, g, i: (bi, g, 0, 0)),
            pl.BlockSpec((1, 1, HEAD_DIM, ncp), lambda bi, g, i: (bi, g, 0, 0)),
            pl.BlockSpec((1, 1, s, K_SEL_AUG), lambda bi, g, i: (bi, g, 0, 0)),
            pl.BlockSpec((1, 1, nq, HEAD_DIM, TQ), lambda bi, g, i: (bi, g, 0, 0, 0)),
            pl.BlockSpec((1, 1, s, K_WIN_AUG), lambda bi, g, i: (bi, g, 0, 0)),
            pl.BlockSpec((1, 1, nq, HEAD_DIM, TQ), lambda bi, g, i: (bi, g, 0, 0, 0)),
            per_g(2 * CPT), per_g(8), per_g(AUG_PAD), per_g(TQ), per_g(TQ),
            pl.BlockSpec(cov_t.shape, lambda bi, g, i: (0, 0)),
        ],
        out_specs=pl.BlockSpec((1, TQ, KV_WIDTH), lambda bi, g, i: (bi, i, g)),
        out_shape=jax.ShapeDtypeStruct((b, s, NSA_WIDTH), BF16),
        scratch_shapes=[pltpu.VMEM((CPT + ncp, nl), F32), pltpu.VMEM((K_SEL_AUG, nl), BF16),
                        pltpu.VMEM((K_WIN_AUG, nl), BF16),
                        pltpu.VMEM((1, nl), F32), pltpu.VMEM((1, nl), F32), pltpu.VMEM((HEAD_DIM, nl), F32)],
        compiler_params=_cparams(("parallel", "parallel", "arbitrary")),
        name="nsa_prompt",
    )(q_t, gt, sa, kcb, vcb_t, ks, vs_t, kw, vw_t, pcd, far, far16, tz0, tz1, cov_t)


def _nsa_sample_kernel(*refs, n_pg, n_chunks, n_slc, past, t_new):
    (qbd_ref, kcbt_ref, vcb_ref, bc_ref, cov_ref, rmat_ref) = refs[1:7]
    kpg = refs[7:7 + n_pg]
    vpg = refs[7 + n_pg:7 + 2 * n_pg]
    (bsl_ref, ksn_ref, vsn_ref, bsn_ref, kwc_ref, vwc_ref, kwn_ref, vwn_ref, bw_ref, gt_ref, e4_ref,
     o_ref, sel_ref, m_ref, l_ref, acc_ref, oc_ref) = refs[7 + 2 * n_pg:]
    j = pl.program_id(1)
    qbd = qbd_ref[0]
    nsp = cov_ref.shape[1]
    n_win = kwc_ref.shape[2]

    def softmax_rows(s):
        m = jnp.max(s, axis=-1, keepdims=True)
        e = jnp.where(s > 0.5 * NEG, jnp.exp(s - m), 0.0)
        den = jnp.sum(e, axis=-1, keepdims=True)
        return e / jnp.where(den > 0, den, 1.0)

    def flash_step(s, v, v_is_transposed):
        m_old = m_ref[...]
        m_new = jnp.maximum(m_old, jnp.max(s, axis=-1, keepdims=True))
        alpha = jnp.exp(m_old - m_new)
        p = jnp.exp(s - m_new)
        l_ref[...] = alpha * l_ref[...] + jnp.sum(p, axis=-1, keepdims=True)
        pv = _dot_nt(p.astype(BF16), v) if v_is_transposed else _dot(p.astype(BF16), v)
        acc_ref[...] = alpha * acc_ref[...] + pv
        m_ref[...] = m_new

    @pl.when(j == 0)
    def _():
        p_c = softmax_rows(_dot(qbd, kcbt_ref[0]) + bc_ref[...])
        oc_ref[...] = _dot(p_c.astype(BF16), vcb_ref[0])
        hi, lo = _split_hi_lo(p_c)
        psum = _dot(rmat_ref[...], hi) + _dot(rmat_ref[...], lo)
        hi, lo = _split_hi_lo(psum)
        imp_t = (_dot(hi, cov_ref[...]) + _dot(lo, cov_ref[...])).T
        jrow = lax.broadcasted_iota(jnp.int32, (nsp, LANES), 0)
        lane = lax.broadcasted_iota(jnp.int32, (1, LANES), 1)
        qpos = past + lane % t_new
        cur = qpos // SEL_BLOCK
        causal = (jrow * SEL_BLOCK <= qpos) & (jrow < n_slc)
        forced = (jrow == 0) | (jrow == cur) | (jrow == cur - 1)
        score = jnp.where(forced, FORCE_SCORE, jnp.where(causal, imp_t, NEG))
        chosen = _select_top_blocks(score, jrow, min(N_SEL, n_slc))
        sel = jnp.where(causal & (lane < N_HEADS * t_new) & (chosen > 0.5), 0.0, -1.0).T
        for k in range(nsp // LANES):
            sel_ref[k] = sel[:, k * LANES:(k + 1) * LANES]
        m_ref[...] = jnp.full(m_ref.shape, NEG, F32)
        l_ref[...] = jnp.zeros(l_ref.shape, F32)
        acc_ref[...] = jnp.zeros(acc_ref.shape, F32)

    far = bsl_ref[:, PAGE_SIZE:2 * PAGE_SIZE]
    last = jnp.where(j == n_chunks - 1, bsl_ref[:, 0:PAGE_SIZE], far)
    chunks_per_tile = LANES // (n_pg * (PAGE_SIZE // SEL_BLOCK))
    sel_tile = sel_ref[j // chunks_per_tile].astype(BF16)
    pp = SAMPLE_PIECE_PAGES
    n_pieces = n_pg // pp

    def logits(t):
        k_t = jnp.concatenate([r[0] for r in kpg[t * pp:(t + 1) * pp]], axis=1).astype(BF16)
        bias = jnp.concatenate([far] * (pp - 1) + [last if t == n_pieces - 1 else far], axis=1)
        mask_add = _dot(sel_tile, e4_ref[j % chunks_per_tile, :, t * pp * PAGE_SIZE:(t + 1) * pp * PAGE_SIZE])
        return _dot(qbd, k_t) + bias + mask_add

    s = {t: logits(t) for t in range(min(PIPE_AHEAD, n_pieces))}
    pieces = []
    for t in range(n_pieces):
        m = jnp.max(s[t], axis=-1, keepdims=True)
        p = jnp.exp(s.pop(t) - m)
        if t + PIPE_AHEAD < n_pieces:
            s[t + PIPE_AHEAD] = logits(t + PIPE_AHEAD)
        v_t = jnp.concatenate([r[0] for r in vpg[t * pp:(t + 1) * pp]], axis=1).astype(BF16)
        pieces.append((_dot_nt(p.astype(BF16), v_t), m, jnp.sum(p, axis=-1, keepdims=True)))
    m_old = m_ref[...]
    m_new = functools.reduce(jnp.maximum, [m_old] + [m for _, m, _ in pieces])
    scales = [jnp.exp(m - m_new) for _, m, _ in pieces]
    a_old = jnp.exp(m_old - m_new)
    l_ref[...] = a_old * l_ref[...] + functools.reduce(jnp.add, [a * l for a, (_, _, l) in zip(scales, pieces)])
    acc_ref[...] = a_old * acc_ref[...] + functools.reduce(jnp.add, [a * acc for a, (acc, _, _) in zip(scales, pieces)])
    m_ref[...] = m_new

    @pl.when(j == n_chunks - 1)
    def _():
        sn = _dot_nt(qbd, ksn_ref[0].astype(BF16)) + bsn_ref[...]
        blk = n_slc - 1
        seln = sel_ref[blk // LANES][:, blk % LANES:blk % LANES + 1]
        flash_step(sn + seln * (-NEG), vsn_ref[0].astype(BF16), False)
        l = l_ref[...]
        o_s = acc_ref[...] / jnp.where(l > 0, l, 1.0)
        sw = jnp.concatenate([_dot(qbd, kwc_ref[0].astype(BF16)), _dot_nt(qbd, kwn_ref[0].astype(BF16))], axis=1)
        p_w = softmax_rows(sw + bw_ref[...]).astype(BF16)
        o_w = _dot_nt(p_w[:, :n_win], vwc_ref[0].astype(BF16)) + _dot(p_w[:, n_win:], vwn_ref[0].astype(BF16))
        gt = gt_ref[0]
        o_ref[0] = gt[:, 0:1] * oc_ref[...] + gt[:, 1:2] * o_s + gt[:, 2:3] * o_w


def _nsa_sample(q, gates, kcb_t, vcb, cache_k_slc, cache_v_slc, page_table, ks_new, vs_new,
                kw_cache, vw_cache, kw_new, vw_new, rel_bias, *, past):
    db, t_new, _ = q.shape
    n_pages = page_table.shape[1]
    assert past == n_pages * PAGE_SIZE and past % SEL_BLOCK == 0 and t_new <= SEL_BLOCK
    assert PAGE_SIZE >= MAX_DISTANCE
    n_cp = kcb_t.shape[2]
    n_slc = -(-(past + t_new) // SEL_BLOCK)
    nsp = -(-n_slc // LANES) * LANES
    n_pg = SAMPLE_PAGES
    n_chunks = n_pages // n_pg
    blocks_per_chunk = n_pg * (PAGE_SIZE // SEL_BLOCK)
    assert LANES % blocks_per_chunk == 0
    n_win = kw_cache.shape[2]
    nl = N_HEADS * t_new
    assert nl <= LANES
    row = np.arange(LANES)
    row_ok = (row < nl)[:, None]
    row_g = np.where(row < nl, row // (HPG * t_new), 0)
    row_t = (row % t_new)[:, None]
    q5 = q.reshape(db, t_new, N_KV, HPG, HEAD_DIM)
    qbd = jnp.einsum('btgrd,gk->bgrtkd', q5.astype(F32), jnp.eye(N_KV, dtype=F32)).reshape(db, nl, KV_WIDTH)
    qbd = jnp.pad(qbd, ((0, 0), (0, LANES - nl), (0, 0))).astype(BF16)

    tab_rows = jnp.pad(jnp.repeat(rel_bias.astype(F32), t_new, axis=1), ((0, 0), (0, LANES - nl)))[:, :, None]

    def bias_tab(dist, valid):
        return jnp.where(jnp.asarray(valid & row_ok), _bias_lookup(tab_rows, dist), NEG)

    qpos = past + row_t
    cblk = np.arange(n_cp)[None, :]
    dist_c = qpos - (cblk * CMP_STRIDE + CMP_BLOCK - 1)
    bc = bias_tab(dist_c, (dist_c >= 0) & (cblk < n_cp - 1))
    kpos = past - PAGE_SIZE + np.arange(PAGE_SIZE)[None, :]
    bsl = jnp.concatenate([bias_tab(qpos - kpos, np.ones((LANES, PAGE_SIZE), bool)),
                           bias_tab(np.full((LANES, PAGE_SIZE), MAX_DISTANCE), np.ones((LANES, PAGE_SIZE), bool))],
                          axis=1)
    u = np.arange(PAGE_SIZE)[None, :]
    new_ok = (u <= row_t) & (u < t_new)
    bsn = bias_tab(row_t - u, new_ok)
    wpos = past - n_win + np.arange(n_win)[None, :]
    dist_w = qpos - wpos
    bw = jnp.concatenate([bias_tab(dist_w, (dist_w >= 0) & (dist_w < WINDOW) & (wpos >= 0)),
                          bias_tab(row_t - u, new_ok & (row_t - u < WINDOW))], axis=1)
    c = np.arange(n_cp)[:, None]
    jb = np.arange(nsp)[None, :]
    cov = ((c * CMP_STRIDE < (jb + 1) * SEL_BLOCK) & (c * CMP_STRIDE + CMP_BLOCK > jb * SEL_BLOCK)
           & (c < n_cp - 1) & (jb < n_slc))
    cov = jnp.asarray(cov, BF16)
    same = (row_g[:, None] == row_g[None, :]) & (row_t == row_t.T) & row_ok & row_ok.T
    rmat = jnp.asarray(same, BF16)
    kk = np.arange(n_pg * PAGE_SIZE)[None, None, :] // SEL_BLOCK
    e4 = np.arange(LANES)[None, :, None] == (np.arange(LANES // blocks_per_chunk)[:, None, None] * blocks_per_chunk + kk)
    e4 = jnp.asarray(np.where(e4, -NEG, 0.0), BF16)
    g5 = gates.reshape(db, t_new, N_KV, LANES)[..., :3 * HPG].reshape(db, t_new, N_KV, 3, HPG)
    gcol = jnp.transpose(g5, (0, 2, 4, 1, 3)).reshape(db, nl, 3)
    gcol = jnp.pad(gcol, ((0, 0), (0, LANES - nl), (0, 5)))

    def pad_new(a):
        return jnp.pad(a, ((0, 0), (0, PAGE_SIZE - t_new), (0, 0)))

    ksn, vsn, kwn, vwn = (pad_new(a) for a in (ks_new, vs_new, kw_new, vw_new))

    per_b = lambda shape: pl.BlockSpec((1,) + shape, lambda b, jc, pt: (b, 0, 0))
    full = lambda a: pl.BlockSpec(a.shape, lambda b, jc, pt: (0,) * a.ndim)
    page = lambda p: pl.BlockSpec((1, KV_WIDTH, PAGE_SIZE), lambda b, jc, pt: (pt[b, jc * SAMPLE_PAGES + p], 0, 0))
    in_specs = ([per_b((LANES, KV_WIDTH)), per_b((KV_WIDTH, n_cp)), per_b((n_cp, KV_WIDTH)),
                 full(bc), full(cov), full(rmat)]
                + [page(p) for p in range(n_pg)] * 2
                + [full(bsl), per_b((PAGE_SIZE, KV_WIDTH)), per_b((PAGE_SIZE, KV_WIDTH)), full(bsn),
                   per_b((KV_WIDTH, n_win)), per_b((KV_WIDTH, n_win)),
                   per_b((PAGE_SIZE, KV_WIDTH)), per_b((PAGE_SIZE, KV_WIDTH)), full(bw), per_b((LANES, 8)), full(e4)])
    o = pl.pallas_call(
        functools.partial(_nsa_sample_kernel, n_pg=n_pg, n_chunks=n_chunks, n_slc=n_slc, past=past, t_new=t_new),
        grid_spec=pltpu.PrefetchScalarGridSpec(
            num_scalar_prefetch=1, grid=(db, n_chunks), in_specs=in_specs,
            out_specs=pl.BlockSpec((1, LANES, KV_WIDTH), lambda b, jc, pt: (b, 0, 0)),
            scratch_shapes=[pltpu.VMEM((nsp // LANES, LANES, LANES), F32), pltpu.VMEM((LANES, 1), F32),
                            pltpu.VMEM((LANES, 1), F32), pltpu.VMEM((LANES, KV_WIDTH), F32),
                            pltpu.VMEM((LANES, KV_WIDTH), F32)]),
        out_shape=jax.ShapeDtypeStruct((db, LANES, KV_WIDTH), F32),
        compiler_params=_cparams(("arbitrary", "arbitrary")),
        name="nsa_sample",
    )(page_table, qbd, kcb_t, vcb, bc, cov, rmat, *([cache_k_slc] * n_pg), *([cache_v_slc] * n_pg),
      bsl, ksn, vsn, bsn, kw_cache, vw_cache, kwn, vwn, bw, gcol, e4)
    o6 = o[:, :nl].reshape(db, N_KV, HPG, t_new, N_KV, HEAD_DIM)
    o_diag = jnp.stack([o6[:, g, :, :, g] for g in range(N_KV)], axis=1)
    return jnp.transpose(o_diag, (0, 3, 1, 2, 4)).reshape(db, t_new, NSA_WIDTH)


def _conv_tail(y, cb_ref, lg_ref, lb_ref, wpw_ref, bpw_ref, sb):
    y = y + cb_ref[...]
    mu = jnp.mean(y, axis=-1, keepdims=True)
    yc = y - mu
    var = jnp.mean(yc * yc, axis=-1, keepdims=True)
    yn = yc * lax.rsqrt(var + LN_EPS) * lg_ref[...] + lb_ref[...]
    act = yn * _sigmoid(yn)
    return ((_dot(act.astype(BF16), wpw_ref[...]) + bpw_ref[...]) * sb).astype(BF16)


def _conv_prompt_kernel(c_ref, halo_ref, init_ref, cw_ref, cb_ref, lg_ref, lb_ref, wpw_ref, bpw_ref, sb_ref,
                        o_ref, full_ref, sh_ref, y_ref, *, ts):
    j = pl.program_id(1)
    full_ref[CONV_HALO:CONV_HALO + ts, :] = c_ref[0]

    @pl.when(j == 0)
    def _():
        full_ref[0:CONV_HALO, :] = init_ref[0]

    @pl.when(j > 0)
    def _():
        full_ref[0:CONV_HALO, :] = halo_ref[0]

    first = CONV_HALO - (CONV_WIDTH - 1)
    span = sh_ref.shape[1]
    for sft in range(1, 8):
        sh_ref[sft - 1] = full_ref[sft:sft + span, :]
    rb = 64
    ch = full_ref.shape[1]
    for c0 in range(0, ch, LANES):
        for r0 in range(0, ts, rb):
            acc = jnp.zeros((rb, LANES), F32)
            for w in range(CONV_WIDTH):
                sft = (first + w) % 8
                base = r0 + first + w - sft
                if sft == 0:
                    x = full_ref[base:base + rb, c0:c0 + LANES]
                else:
                    x = sh_ref[sft - 1, base:base + rb, c0:c0 + LANES]
                acc = acc + x * cw_ref[w:w + 1, c0:c0 + LANES]
            y_ref[r0:r0 + rb, c0:c0 + LANES] = acc
    o_ref[0] = _conv_tail(y_ref[...], cb_ref, lg_ref, lb_ref, wpw_ref, bpw_ref, sb_ref[0])


def _conv_sample_kernel(c_ref, st_ref, cw_ref, cb_ref, lg_ref, lb_ref, wpw_ref, bpw_ref, sb_ref,
                        o_ref, full_ref, y_ref):
    nb, t_new, _ = c_ref.shape
    n_st = st_ref.shape[1]
    full_ref[:, 0:n_st, :] = st_ref[...]
    full_ref[:, n_st:n_st + t_new, :] = c_ref[...]
    first = n_st - (CONV_WIDTH - 1)
    for b in range(nb):
        acc = jnp.zeros((t_new, full_ref.shape[2]), F32)
        for w in range(CONV_WIDTH):
            acc = acc + full_ref[b, first + w:first + w + t_new, :] * cw_ref[w:w + 1, :]
        y_ref[b * t_new:(b + 1) * t_new, :] = acc
    o_ref[...] = _conv_tail(y_ref[...], cb_ref, lg_ref, lb_ref, wpw_ref, bpw_ref, sb_ref[...])


def _conv_params(conv_w, conv_b, ln_g, ln_b, w_pw, b_pw):
    ch = conv_w.shape[1]
    cw = jnp.pad(conv_w, ((0, 32 - CONV_WIDTH), (0, 0)))
    return (cw, conv_b.reshape(1, ch), ln_g.reshape(1, ch), ln_b.reshape(1, ch), w_pw.astype(BF16),
            b_pw.reshape(1, ch))


def _conv_prompt(c_in, init, params, sb, *, ts):
    b, s, ch = c_in.shape
    hb = ts // CONV_HALO
    const = lambda a: pl.BlockSpec(a.shape, lambda bi, j: (0,) * a.ndim)
    return pl.pallas_call(
        functools.partial(_conv_prompt_kernel, ts=ts),
        grid=(b, s // ts),
        in_specs=[pl.BlockSpec((1, ts, ch), lambda bi, j: (bi, j, 0)),
                  pl.BlockSpec((1, CONV_HALO, ch), lambda bi, j: (bi, jnp.maximum(j * hb - 1, 0), 0)),
                  pl.BlockSpec((1, CONV_HALO, ch), lambda bi, j: (bi, 0, 0))]
        + [const(a) for a in params]
        + [pl.BlockSpec((1, ts, ch), lambda bi, j: (bi, j, 0))],
        out_specs=pl.BlockSpec((1, ts, ch), lambda bi, j: (bi, j, 0)),
        out_shape=jax.ShapeDtypeStruct((b, s, ch), BF16),
        scratch_shapes=[pltpu.VMEM((CONV_HALO + ts, ch), F32), pltpu.VMEM((7, CONV_HALO + ts - 8, ch), F32),
                        pltpu.VMEM((ts, ch), F32)],
        compiler_params=_cparams(("parallel", "arbitrary")),
        name="conv_prompt",
    )(c_in, c_in, init, *params, sb)


def _conv_sample(c_in, state, params, sb):
    db, t_new, ch = c_in.shape
    n_st = state.shape[1]
    rows_pad = -(-(n_st + t_new) // 8) * 8
    return pl.pallas_call(
        _conv_sample_kernel,
        out_shape=jax.ShapeDtypeStruct((db * t_new, ch), BF16),
        scratch_shapes=[pltpu.VMEM((db, rows_pad, ch), F32), pltpu.VMEM((db * t_new, ch), F32)],
        compiler_params=pltpu.CompilerParams(vmem_limit_bytes=VMEM_LIMIT),
        name="conv_sample",
    )(c_in, state, *params, sb)


def _out_kernel(x_ref, ma_ref, mb_ref, *rest, gated):
    if gated:
        sa_ref, wa_ref, wb_ref, gp_ref, y_ref = rest
        ma = (ma_ref[0] * sa_ref[0]).astype(BF16)
    else:
        wa_ref, wb_ref, gp_ref, y_ref = rest
        ma = ma_ref[0]
    z = _dot(ma, wa_ref[...]) + _dot(mb_ref[0], wb_ref[...])
    ms = jnp.mean(z * z, axis=-1, keepdims=True)
    y_ref[0] = x_ref[0] + z * lax.rsqrt(ms + RMS_EPS) * gp_ref[...]


def _out_proj(x, ma, mb, sa, w_out, g_post, *, tm):
    b, s, d = x.shape
    na = ma.shape[-1]
    wa = w_out[:na].astype(BF16)
    wb = w_out[na:].astype(BF16)
    row = lambda width: pl.BlockSpec((1, tm, width), lambda bi, i: (bi, i, 0))
    const = lambda a: pl.BlockSpec(a.shape, lambda bi, i: (0, 0))
    gp = g_post.reshape(1, d)
    gate_in, gate_spec = ([sa], [row(na)]) if sa is not None else ([], [])
    return pl.pallas_call(
        functools.partial(_out_kernel, gated=sa is not None),
        grid=(b, s // tm),
        in_specs=[row(d), row(na), row(mb.shape[-1])] + gate_spec + [const(wa), const(wb), const(gp)],
        out_specs=row(d),
        out_shape=jax.ShapeDtypeStruct((b, s, d), F32),
        compiler_params=_cparams(("parallel", "parallel")),
        name="out_proj",
    )(x, ma, mb, *gate_in, wa, wb, gp)


def _split_w_in(w_in):
    d = w_in.shape[0]
    c0 = NSA_WIDTH + 6 * KV_WIDTH
    n_gate = 3 * N_HEADS
    conv_ch = (w_in.shape[1] - c0 - n_gate - NSA_WIDTH) // 3
    w_qkv = w_in[:, :c0].astype(BF16)
    wg = w_in[:, c0:c0 + n_gate].reshape(d, N_KV, HPG, 3)
    wg = jnp.transpose(wg, (0, 1, 3, 2)).reshape(d, N_KV, 3 * HPG)
    wg = jnp.pad(wg, ((0, 0), (0, 0), (0, LANES - 3 * HPG))).reshape(d, N_KV * LANES)
    z_a = w_in[:, c0 + n_gate:c0 + n_gate + NSA_WIDTH]
    glu0 = c0 + n_gate + NSA_WIDTH
    w_glu = w_in[:, glu0:glu0 + 2 * conv_ch].astype(BF16)
    z_b = w_in[:, glu0 + 2 * conv_ch:]
    assert conv_ch == NSA_WIDTH
    w_gate = jnp.concatenate([z_a, z_b, wg], axis=1).astype(BF16)
    return w_qkv, w_gate, w_glu


def _token_minor(cache):
    n, tokens = cache.shape[:2]
    return jnp.transpose(cache, (0, 2, 3, 1)).reshape(n, KV_WIDTH, tokens)


def kernel(x_prompt, x_sample, cache_k_cmp, cache_v_cmp, cache_k_slc, cache_v_slc, cache_k_win, cache_v_win,
           state_conv, page_table, g_pre, w_in, cmp_w1_k, cmp_w2_k, cmp_pe_k, cmp_w1_v, cmp_w2_v, cmp_pe_v,
           rel_bias, conv_w, conv_b, ln_g, ln_b, w_pw, b_pw, w_out, g_post):
    depth = g_pre.shape[0]
    assert depth == 1
    layer = 0
    b, s, d = x_prompt.shape
    db, t_new, _ = x_sample.shape
    past = page_table.shape[1] * PAGE_SIZE
    conv_ch = conv_w.shape[-1]

    w_qkv, w_gate, w_glu = _split_w_in(w_in[layer])
    wts_k = _cmp_weights(cmp_w1_k[layer], cmp_w2_k[layer], cmp_pe_k[layer])
    wts_v = _cmp_weights(cmp_w1_v[layer], cmp_w2_v[layer], cmp_pe_v[layer])
    cparams = _conv_params(conv_w[layer], conv_b[layer], ln_g[layer], ln_b[layer], w_pw[layer], b_pw[layer])
    chunk_w = CMP_STRIDE * KV_WIDTH

    (q_t, kc, vc, ks, vs, kw, vw, ks_g, kw_g, vs_t, vw_t), sa, sb, gt, c_in = _projections(
        x_prompt, g_pre[layer], w_qkv, w_gate, w_glu, tm=512, attn_layouts=True)
    n_ch = s // CMP_STRIDE
    kcb = _compress(kc[:, :n_ch * CMP_STRIDE].reshape(b, n_ch, chunk_w), wts_k, layout="group_rows")
    vcb_t = _compress(vc[:, :n_ch * CMP_STRIDE].reshape(b, n_ch, chunk_w), wts_v, layout="group_cols")
    ma = _nsa_prompt(q_t, gt, sa, kcb, vcb_t, ks_g, vs_t, kw_g, vw_t, rel_bias)
    mb = _conv_prompt(c_in, jnp.zeros((b, CONV_HALO, conv_ch), F32), cparams, sb, ts=256)
    y_prompt = _out_proj(x_prompt, ma, mb, None, w_out[layer], g_post[layer], tm=512)
    n_keep = min(WINDOW, s)
    kv5 = lambda a: a.reshape(1, a.shape[0], a.shape[1], N_KV, HEAD_DIM)
    outs_p = (kv5(kc), kv5(vc), kv5(ks), kv5(vs), kv5(kw[:, -n_keep:]), kv5(vw[:, -n_keep:]),
              c_in[None, :, -(CONV_WIDTH - 1):])

    xs = x_sample.reshape(1, db * t_new, d)
    (q_s, kc_s, vc_s, ks_s, vs_s, kw_s, vw_s), sa_s, sb_s, gt_s, c_s = _projections(
        xs, g_pre[layer], w_qkv, w_gate, w_glu, tm=db * t_new, attn_layouts=False)
    tok = lambda a: a.reshape(db, t_new, a.shape[-1])
    kcb_s = _compress(_token_minor(cache_k_cmp[layer]), wts_k, page_table=page_table, layout="cols")
    vcb_s = _compress(_token_minor(cache_v_cmp[layer]), wts_v, page_table=page_table, layout="rows")
    o_a = _nsa_sample(tok(q_s), tok(gt_s), kcb_s, vcb_s,
                      _token_minor(cache_k_slc[layer]), _token_minor(cache_v_slc[layer]), page_table,
                      tok(ks_s), tok(vs_s), _token_minor(cache_k_win[layer]), _token_minor(cache_v_win[layer]),
                      tok(kw_s), tok(vw_s), rel_bias, past=past)
    mb_s = _conv_sample(tok(c_s), state_conv[layer], cparams, sb_s[0])
    y_sample = _out_proj(xs, o_a.reshape(1, db * t_new, NSA_WIDTH), mb_s[None], sa_s, w_out[layer],
                         g_post[layer], tm=db * t_new).reshape(db, t_new, d)
    n_keep_s = min(WINDOW, past + t_new)
    kv5s = lambda a: a.reshape(1, db, t_new, N_KV, HEAD_DIM)
    win = lambda cache, new: jnp.concatenate(
        [cache[layer], new.reshape(db, t_new, N_KV, HEAD_DIM)], axis=1)[None, :, -n_keep_s:]
    conv_s = jnp.concatenate([state_conv[layer], tok(c_s)], axis=1)[None, :, -(CONV_WIDTH - 1):]
    outs_s = (kv5s(kc_s), kv5s(vc_s), kv5s(ks_s), kv5s(vs_s), win(cache_k_win, kw_s), win(cache_v_win, vw_s), conv_s)
    return (y_prompt, y_sample) + outs_p + outs_s
```

```python
import functools
import math

import numpy as np
import jax
import jax.numpy as jnp
from jax import lax
from jax.experimental import pallas as pl
from jax.experimental.pallas import tpu as pltpu

F32 = jnp.float32
BF16 = jnp.bfloat16

HEAD_DIM = 64
N_KV = 4
HPG = 4
N_HEADS = N_KV * HPG
KV_WIDTH = N_KV * HEAD_DIM
NSA_WIDTH = N_HEADS * HEAD_DIM
CMP_BLOCK = 32
CMP_STRIDE = 16
SEL_BLOCK = 64
N_SEL = 16
WINDOW = 512
CONV_WIDTH = 31
N_BUCKETS = 32
MAX_DISTANCE = 128
FORCE_SCORE = 1e6
RMS_EPS = 1e-6
LN_EPS = 1e-5
PAGE_SIZE = 128
SCALE = HEAD_DIM ** -0.5

NEG = -1e30
PICKED = -3e38
LANES = 128
VMEM_LIMIT = 56 * 1024 * 1024

LOG2E = 1.4426950408889634
TQ = 256
FAR_GROUP = 4
PIPE_AHEAD = 2
AUG_PAD = 16
K_WIN_AUG = HEAD_DIM + AUG_PAD
K_SEL_AUG = HEAD_DIM + LANES + AUG_PAD
CPT = TQ // CMP_STRIDE
CONV_HALO = 32
SAMPLE_PAGES = 32
SAMPLE_PIECE_PAGES = 8
CMP_PAGES = 32


def _cparams(sem):
    return pltpu.CompilerParams(dimension_semantics=sem, vmem_limit_bytes=VMEM_LIMIT)


def _dot(a, b):
    return jnp.dot(a, b, preferred_element_type=F32)


def _dot_nt(a, b):
    return lax.dot_general(a, b, (((1,), (1,)), ((), ())), preferred_element_type=F32)


def _sigmoid(x):
    return 1.0 / (1.0 + jnp.exp(-x))


def _split_hi_lo(x):
    hi = x.astype(BF16)
    lo = (x - hi.astype(F32)).astype(BF16)
    return hi, lo


def _normed(x_ref, g_ref):
    x = x_ref[0]
    ms = jnp.mean(x * x, axis=-1, keepdims=True)
    return (x * lax.rsqrt(ms + RMS_EPS) * g_ref[...]).astype(BF16)


def _proj_qkv_kernel(x_ref, g_ref, w_ref, q_ref, kc_ref, vc_ref, ks_ref, vs_ref, kw_ref, vw_ref,
                     *attn_refs, attn_layouts):
    h = _normed(x_ref, g_ref)
    tm = h.shape[0]
    if attn_layouts:
        tok = pl.program_id(1) * tm + lax.broadcasted_iota(jnp.int32, (tm, LANES), 0)
        lane = lax.broadcasted_iota(jnp.int32, (tm, LANES), 1)
        blk_cols = jnp.where(lane == tok // SEL_BLOCK, 1.0, 0.0).astype(BF16)
        one_cols = jnp.where(lax.broadcasted_iota(jnp.int32, (tm, AUG_PAD), 1) < 2, 1.0, 0.0).astype(BF16)
    for g in range(N_KV):
        res = _dot(h, w_ref[:, g * KV_WIDTH:(g + 1) * KV_WIDTH]) * (SCALE * LOG2E if attn_layouts else SCALE)
        if attn_layouts:
            res_t = res.T
            for r in range(HPG):
                q_ref[0, g, r] = res_t[r * HEAD_DIM:(r + 1) * HEAD_DIM, :].astype(BF16)
        else:
            q_ref[0, :, g * KV_WIDTH:(g + 1) * KV_WIDTH] = res.astype(BF16)
    for j, o_ref in enumerate((kc_ref, vc_ref, ks_ref, vs_ref, kw_ref, vw_ref)):
        c0 = NSA_WIDTH + j * KV_WIDTH
        res = _dot(h, w_ref[:, c0:c0 + KV_WIDTH])
        o_ref[0] = res
        if attn_layouts and j in (2, 4):
            extra = [blk_cols, one_cols] if j == 2 else [one_cols]
            for g in range(N_KV):
                attn_refs[j // 2 - 1][0, g] = jnp.concatenate(
                    [res[:, g * HEAD_DIM:(g + 1) * HEAD_DIM].astype(BF16)] + extra, axis=1)
        if attn_layouts and j in (3, 5):
            res_t = res.T.astype(BF16)
            for g in range(N_KV):
                for kt in range(tm // TQ):
                    attn_refs[2 + j // 2 - 1][0, g, kt] = res_t[g * HEAD_DIM:(g + 1) * HEAD_DIM, kt * TQ:(kt + 1) * TQ]


def _proj_gate_kernel(x_ref, g_ref, w_ref, sa_ref, sb_ref, gt_ref):
    h = _normed(x_ref, g_ref)
    for o_ref, base in ((sa_ref, 0), (sb_ref, NSA_WIDTH)):
        for c in range(NSA_WIDTH // 256):
            z = _dot(h, w_ref[:, base + c * 256: base + (c + 1) * 256])
            o_ref[0, :, c * 256:(c + 1) * 256] = z * _sigmoid(z)
    for c in range(2):
        z = _dot(h, w_ref[:, 2 * NSA_WIDTH + c * 256: 2 * NSA_WIDTH + (c + 1) * 256])
        gt_ref[0, :, c * 256:(c + 1) * 256] = _sigmoid(z)


def _proj_glu_kernel(x_ref, g_ref, w_ref, c_ref):
    h = _normed(x_ref, g_ref)
    n = c_ref.shape[-1]
    for c in range(n // 256):
        a = _dot(h, w_ref[:, c * 256:(c + 1) * 256])
        gg = _dot(h, w_ref[:, n + c * 256: n + (c + 1) * 256])
        c_ref[0, :, c * 256:(c + 1) * 256] = a * _sigmoid(gg)


def _projections(x, g_pre, w_qkv, w_gate, w_glu, *, tm, attn_layouts):
    b, s, d = x.shape
    grid = (b, s // tm)
    x_spec = pl.BlockSpec((1, tm, d), lambda bi, i: (bi, i, 0))
    g_spec = pl.BlockSpec((1, d), lambda bi, i: (0, 0))

    def w_spec(w):
        return pl.BlockSpec(w.shape, lambda bi, i: (0, 0))

    def row_spec(width):
        return pl.BlockSpec((1, tm, width), lambda bi, i: (bi, i, 0))

    kv_shape = jax.ShapeDtypeStruct((b, s, KV_WIDTH), F32)
    if attn_layouts:
        assert tm % TQ == 0
        q_shape = jax.ShapeDtypeStruct((b, N_KV, HPG, HEAD_DIM, s), BF16)
        q_spec = pl.BlockSpec((1, N_KV, HPG, HEAD_DIM, tm), lambda bi, i: (bi, 0, 0, 0, i))
        k_shape = lambda w: jax.ShapeDtypeStruct((b, N_KV, s, w), BF16)
        k_spec = lambda w: pl.BlockSpec((1, N_KV, tm, w), lambda bi, i: (bi, 0, i, 0))
        v_shape = jax.ShapeDtypeStruct((b, N_KV, s // TQ, HEAD_DIM, TQ), BF16)
        v_spec = pl.BlockSpec((1, N_KV, tm // TQ, HEAD_DIM, TQ), lambda bi, i: (bi, 0, i, 0, 0))
        extra_shape = [k_shape(K_SEL_AUG), k_shape(K_WIN_AUG), v_shape, v_shape]
        extra_spec = [k_spec(K_SEL_AUG), k_spec(K_WIN_AUG), v_spec, v_spec]
    else:
        q_shape = jax.ShapeDtypeStruct((b, s, NSA_WIDTH), BF16)
        q_spec = row_spec(NSA_WIDTH)
        extra_shape, extra_spec = [], []
    g2 = g_pre.reshape(1, d)
    qkv = pl.pallas_call(
        functools.partial(_proj_qkv_kernel, attn_layouts=attn_layouts),
        grid=grid,
        in_specs=[x_spec, g_spec, w_spec(w_qkv)],
        out_specs=[q_spec] + [row_spec(KV_WIDTH)] * 6 + extra_spec,
        out_shape=[q_shape] + [kv_shape] * 6 + extra_shape,
        compiler_params=_cparams(("parallel", "parallel")),
        name="proj_qkv",
    )(x, g2, w_qkv)
    sa, sb, gt = pl.pallas_call(
        _proj_gate_kernel,
        grid=grid,
        in_specs=[x_spec, g_spec, w_spec(w_gate)],
        out_specs=[row_spec(NSA_WIDTH), row_spec(NSA_WIDTH), row_spec(N_KV * LANES)],
        out_shape=[jax.ShapeDtypeStruct((b, s, NSA_WIDTH), F32)] * 2
        + [jax.ShapeDtypeStruct((b, s, N_KV * LANES), F32)],
        compiler_params=_cparams(("parallel", "parallel")),
        name="proj_gate",
    )(x, g2, w_gate)
    c_in = pl.pallas_call(
        _proj_glu_kernel,
        grid=grid,
        in_specs=[x_spec, g_spec, w_spec(w_glu)],
        out_specs=row_spec(w_glu.shape[1] // 2),
        out_shape=jax.ShapeDtypeStruct((b, s, w_glu.shape[1] // 2), F32),
        compiler_params=_cparams(("parallel", "parallel")),
        name="proj_glu",
    )(x, g2, w_glu)
    return qkv, sa, sb, gt, c_in


def _cmp1_rows_kernel(x_ref, pe_ref, w_ref, pre_ref, pepre_ref):
    w = w_ref[...].reshape(-1, w_ref.shape[-1])
    pre_ref[0] = _dot(x_ref[0].astype(BF16), w)
    pepre_ref[...] = _dot(pe_ref[...], w)


def _cmp1_paged_kernel(*refs, n_in):
    x_refs, (perm_ref, pe_ref, w_ref, pre_ref, pepre_ref) = refs[1:1 + n_in], refs[1 + n_in:]
    perm = perm_ref[...]
    zs = [_dot_nt(perm, r[0].astype(BF16)) for r in x_refs]
    rpp = PAGE_SIZE // CMP_STRIDE
    acc = jnp.zeros((n_in * rpp, w_ref.shape[-1]), F32)
    for c in range(CMP_STRIDE):
        xc = jnp.concatenate([z[c * rpp:(c + 1) * rpp] for z in zs], axis=0)
        acc = acc + _dot(xc.astype(BF16), w_ref[c])
    pre_ref[0] = acc
    pepre_ref[...] = _dot(pe_ref[...], w_ref[...].reshape(-1, w_ref.shape[-1]))


def _cmp2_kernel(pre_ref, pepre_ref, w2_ref, o_ref, *, layout):
    pre = pre_ref[0]
    n_ch = pre.shape[0]
    a = pre[:, :KV_WIDTH]
    b_next = pltpu.roll(pre[:, KV_WIDTH:], n_ch - 1, axis=0)
    pe_bias = pepre_ref[0:1, :KV_WIDTH] + pepre_ref[1:2, KV_WIDTH:]
    z = a + b_next + pe_bias
    hid = 0.5 * z * (1.0 + jnp.tanh(math.sqrt(2.0 / math.pi) * (z + 0.044715 * (z * z * z))))
    out = _dot(hid.astype(BF16), w2_ref[...])
    if layout == "group_rows":
        for g in range(N_KV):
            o_ref[0, g] = out[:, g * HEAD_DIM:(g + 1) * HEAD_DIM].astype(BF16)
    elif layout == "group_cols":
        out_t = out.T.astype(BF16)
        for g in range(N_KV):
            o_ref[0, g] = out_t[g * HEAD_DIM:(g + 1) * HEAD_DIM, :]
    elif layout == "rows":
        o_ref[0] = out.astype(BF16)
    else:
        o_ref[0] = out.T.astype(BF16)


def _cmp_weights(w1, w2, pe):
    eye = jnp.eye(N_KV, dtype=F32)
    halves = w1.reshape(2, CMP_STRIDE, HEAD_DIM, -1)
    hdim = halves.shape[-1]
    wbig = jnp.einsum('acdh,gk->cgdakh', halves, eye)
    wbig = wbig.reshape(CMP_STRIDE, KV_WIDTH, 2 * N_KV * hdim).astype(BF16)
    w2big = jnp.einsum('hd,gk->ghkd', w2, eye).reshape(N_KV * hdim, KV_WIDTH).astype(BF16)
    pe_rows = jnp.broadcast_to(pe.reshape(2, CMP_STRIDE, 1, HEAD_DIM), (2, CMP_STRIDE, N_KV, HEAD_DIM))
    pe8 = jnp.zeros((8, CMP_STRIDE * KV_WIDTH), F32).at[:2].set(pe_rows.reshape(2, -1)).astype(BF16)
    return wbig, w2big, pe8


def _compress(x_view, wts, *, page_table=None, layout):
    wbig, w2big, pe8 = wts
    ncol = wbig.shape[-1]
    kdim = wbig.shape[0] * wbig.shape[1]
    if page_table is None:
        b, n_ch, _ = x_view.shape
        rows = min(n_ch, 256)
        const2 = lambda bi, i: (0, 0)
        pre, pepre = pl.pallas_call(
            _cmp1_rows_kernel, grid=(b, n_ch // rows),
            in_specs=[pl.BlockSpec((1, rows, kdim), lambda bi, i: (bi, i, 0)), pl.BlockSpec(pe8.shape, const2),
                      pl.BlockSpec(wbig.shape, lambda bi, i: (0, 0, 0))],
            out_specs=[pl.BlockSpec((1, rows, ncol), lambda bi, i: (bi, i, 0)), pl.BlockSpec((8, ncol), const2)],
            out_shape=[jax.ShapeDtypeStruct((b, n_ch, ncol), F32), jax.ShapeDtypeStruct((8, ncol), F32)],
            compiler_params=_cparams(("arbitrary", "arbitrary")), name="cmp_stage1")(x_view, pe8, wbig)
    else:
        b, n_pages = page_table.shape
        n_in = CMP_PAGES
        rpp = PAGE_SIZE // CMP_STRIDE
        rows = n_in * rpp
        n_ch = n_pages * rpp
        const2 = lambda bi, i, pt: (0, 0)
        x_specs = [pl.BlockSpec((1, KV_WIDTH, PAGE_SIZE), functools.partial(
            lambda bi, i, pt, p: (pt[bi, i * CMP_PAGES + p], 0, 0), p=p)) for p in range(n_in)]
        tok = np.arange(PAGE_SIZE)
        perm = (tok[None, :] == (tok[:, None] % rpp) * CMP_STRIDE + tok[:, None] // rpp)
        perm = jnp.asarray(perm, BF16)
        pre, pepre = pl.pallas_call(
            functools.partial(_cmp1_paged_kernel, n_in=n_in),
            grid_spec=pltpu.PrefetchScalarGridSpec(
                num_scalar_prefetch=1, grid=(b, n_pages // n_in),
                in_specs=x_specs + [pl.BlockSpec(perm.shape, const2), pl.BlockSpec(pe8.shape, const2),
                                    pl.BlockSpec(wbig.shape, lambda bi, i, pt: (0, 0, 0))],
                out_specs=[pl.BlockSpec((1, rows, ncol), lambda bi, i, pt: (bi, i, 0)),
                           pl.BlockSpec((8, ncol), const2)]),
            out_shape=[jax.ShapeDtypeStruct((b, n_ch, ncol), F32), jax.ShapeDtypeStruct((8, ncol), F32)],
            compiler_params=_cparams(("arbitrary", "arbitrary")), name="cmp_stage1_paged",
        )(page_table, *([x_view] * n_in), perm, pe8, wbig)
    o_dims = {"group_rows": (N_KV, n_ch, HEAD_DIM), "group_cols": (N_KV, HEAD_DIM, n_ch),
              "rows": (n_ch, KV_WIDTH), "cols": (KV_WIDTH, n_ch)}[layout]
    return pl.pallas_call(
        functools.partial(_cmp2_kernel, layout=layout),
        grid=(b,),
        in_specs=[pl.BlockSpec((1, n_ch, ncol), lambda bi: (bi, 0, 0)),
                  pl.BlockSpec((8, ncol), lambda bi: (0, 0)),
                  pl.BlockSpec(w2big.shape, lambda bi: (0, 0))],
        out_specs=pl.BlockSpec((1,) + o_dims, lambda bi: (bi,) + (0,) * len(o_dims)),
        out_shape=jax.ShapeDtypeStruct((b,) + o_dims, BF16),
        compiler_params=_cparams(("parallel",)), name="cmp_stage2",
    )(pre, pepre, w2big)


def _t5_bucket(dist):
    dist = np.maximum(np.asarray(dist, np.int64), 0)
    max_exact = N_BUCKETS // 2
    d32 = np.maximum(dist, 1).astype(np.float32)
    large = max_exact + (np.log(d32 / np.float32(max_exact)) / np.float32(math.log(MAX_DISTANCE / max_exact))
                         * np.float32(N_BUCKETS - max_exact)).astype(np.int32)
    large = np.minimum(large, N_BUCKETS - 1)
    return np.where(dist < max_exact, dist, large).astype(np.int32)


def _bias_lookup(table, dist):
    bucket = jnp.asarray(_t5_bucket(dist).reshape(-1, 1))
    onehot = (bucket == jnp.arange(N_BUCKETS, dtype=jnp.int32)[None, :]).astype(F32)
    out = jnp.dot(onehot, table, precision=lax.Precision.HIGHEST)
    return out.reshape(tuple(np.shape(dist)) + (table.shape[1],))


def _top_block_round(chosen, work, jrow):
    m = jnp.max(work, axis=0, keepdims=True)
    idx = jnp.min(jnp.where(work == m, jrow, 1 << 20), axis=0, keepdims=True)
    pick = jrow == idx
    return jnp.where(pick, 1.0, chosen), jnp.where(pick, PICKED, work)


def _select_top_blocks(score, jrow, k_sel):
    state = (jnp.zeros(score.shape, F32), score)
    for _ in range(k_sel):
        state = _top_block_round(*state, jrow)
    return state[0]


def _nsa_prompt_kernel(q_ref, gt_ref, sa_ref, kcb_ref, vcbt_ref, ks_ref, vst_ref, kw_ref, vwt_ref,
                       pcd_ref, far_ref, far16_ref, tz0_ref, tz1_ref, covt_ref, o_ref,
                       lc_ref, qs_ref, qw_ref, m_ref, l_ref, acc_ref, *, n_slc):
    i = pl.program_id(2)
    nl = HPG * TQ
    ncp = kcb_ref.shape[2]
    q_t = jnp.concatenate([q_ref[0, 0, r] for r in range(HPG)], axis=1)
    lane_t = lax.broadcasted_iota(jnp.int32, (1, nl), 1) & (TQ - 1)
    qpos = i * TQ + lane_t
    far = far_ref[0, 0:1, :]
    qs_ref[0:HEAD_DIM, :] = q_t
    qs_ref[HEAD_DIM + LANES:, :] = far16_ref[0]
    qw_ref[0:HEAD_DIM, :] = q_t
    qw_ref[HEAD_DIM:, :] = far16_ref[0]

    lc_ref[0:CPT, :] = jnp.zeros((CPT, nl), F32)
    lc_ref[CPT:CPT + ncp, :] = _dot(kcb_ref[0, 0], q_t) + far
    near = pl.ds(pl.multiple_of(i * CPT, CPT), 2 * CPT)
    lc_ref[near, :] = lc_ref[near, :] + pcd_ref[0]
    lc = lc_ref[CPT:CPT + ncp, :]
    cend = lax.broadcasted_iota(jnp.int32, (ncp, 1), 0) * CMP_STRIDE + (CMP_BLOCK - 1)
    valid_c = cend <= qpos
    lm = jnp.where(valid_c, lc, NEG)
    mc = jnp.max(lm, axis=0, keepdims=True)
    ec = jnp.where(valid_c, jnp.exp2(lm - mc), 0.0)
    den = jnp.sum(ec, axis=0, keepdims=True)
    p_c = ec / jnp.where(den > 0, den, 1.0)
    o_c = _dot(vcbt_ref[0, 0], p_c.astype(BF16))

    psum = p_c[:, 0:TQ]
    for r in range(1, HPG):
        psum = psum + p_c[:, r * TQ:(r + 1) * TQ]
    hi, lo = _split_hi_lo(psum)
    imp_t = _dot(covt_ref[...], hi) + _dot(covt_ref[...], lo)
    jrow = lax.broadcasted_iota(jnp.int32, (LANES, TQ), 0)
    qpos_t = qpos[:, 0:TQ]
    cur = qpos_t // SEL_BLOCK
    causal = (jrow * SEL_BLOCK <= qpos_t) & (jrow < n_slc)
    forced = (jrow == 0) | (jrow == cur) | (jrow == cur - 1)
    score = jnp.where(forced, FORCE_SCORE, jnp.where(causal, imp_t, NEG))
    k_sel = min(N_SEL, n_slc)
    topk = [(jnp.zeros(score.shape, F32), score), 0]

    def topk_rounds(n):
        for _ in range(min(n, k_sel - topk[1])):
            topk[0] = _top_block_round(*topk[0], jrow)
            topk[1] += 1

    key_u = lax.broadcasted_iota(jnp.int32, (TQ, 1), 0)
    causal_diag = key_u <= lane_t

    def run_pieces(specs, between=lambda: None):
        def logits(t):
            k_ref, _, qx_ref, kt, bias, mask = specs[t]
            s = _dot(k_ref[0, 0, pl.ds(pl.multiple_of(kt * TQ, TQ), TQ), :], qx_ref[...])
            if bias is not None:
                s = s + bias
            return s if mask is None else jnp.where(mask, s, NEG)

        n = len(specs)
        s = {t: logits(t) for t in range(min(PIPE_AHEAD, n))}
        between()
        pieces = []
        for t in range(n):
            m = jnp.max(s[t], axis=0, keepdims=True)
            p = jnp.exp2(s.pop(t) - m)
            if t + PIPE_AHEAD < n:
                s[t + PIPE_AHEAD] = logits(t + PIPE_AHEAD)
            between()
            _, vt_ref, _, kt, _, _ = specs[t]
            pieces.append((_dot(vt_ref[0, 0, kt], p.astype(BF16)), m, jnp.sum(p, axis=0, keepdims=True)))
            between()
        return pieces

    def merge(pieces):
        m_new = functools.reduce(jnp.maximum, [m for _, m, _ in pieces])
        scales = [jnp.exp2(m - m_new) for _, m, _ in pieces]
        acc = functools.reduce(jnp.add, [a * acc for a, (acc, _, _) in zip(scales, pieces)])
        l = functools.reduce(jnp.add, [a * l for a, (_, _, l) in zip(scales, pieces)])
        return acc, m_new, l

    prev = jnp.maximum(i - 1, 0)
    prev2 = jnp.maximum(i - 2, 0)
    tz1 = tz1_ref[0] + jnp.where(i >= 1, 0.0, NEG)
    tz0 = tz0_ref[0]
    n_gaps = 1 + 2 * 3
    acc_w, _, l_w = merge(run_pieces([(kw_ref, vwt_ref, qw_ref, prev2, None, (key_u > lane_t) & (i >= 2)),
                                      (kw_ref, vwt_ref, qw_ref, prev, tz1, None),
                                      (kw_ref, vwt_ref, qw_ref, i, tz0, causal_diag)],
                                     between=lambda: topk_rounds(-(-k_sel // n_gaps))))
    o_w = acc_w / l_w
    topk_rounds(k_sel)
    chosen = topk[0][0]
    sel_add = jnp.where(causal & (chosen > 0.5), 0.0, NEG).astype(BF16)
    qs_ref[HEAD_DIM:HEAD_DIM + LANES, :] = jnp.concatenate([sel_add] * HPG, axis=1)

    m_ref[...] = jnp.full(m_ref.shape, NEG, F32)
    l_ref[...] = jnp.zeros(l_ref.shape, F32)
    acc_ref[...] = jnp.zeros(acc_ref.shape, F32)

    def state():
        return acc_ref[...], m_ref[...], l_ref[...]

    def merge_far(kt0, n):
        acc, m, l = merge([state()] + run_pieces([(ks_ref, vst_ref, qs_ref, kt0 + t, None, None) for t in range(n)]))
        acc_ref[...] = acc
        m_ref[...] = m
        l_ref[...] = l

    n_far = jnp.maximum(i - 1, 0)

    def far_group(jg, carry):
        merge_far(FAR_GROUP * jg, FAR_GROUP)
        return carry

    lax.fori_loop(0, n_far // FAR_GROUP, far_group, 0)
    rest = n_far % FAR_GROUP
    size = FAR_GROUP // 2
    while size >= 1:
        @pl.when((rest & size) != 0)
        def _(size=size):
            merge_far(n_far - (rest & (2 * size - 1)), size)
        size //= 2

    acc_s, _, l_s = merge([state()] + run_pieces([(ks_ref, vst_ref, qs_ref, prev, tz1, None),
                                                  (ks_ref, vst_ref, qs_ref, i, tz0, causal_diag)]))
    o_s = acc_s / l_s

    g_t = gt_ref[0].T

    def gate_row(branch):
        return jnp.concatenate([g_t[branch * HPG + r:branch * HPG + r + 1, :] for r in range(HPG)], axis=1)

    o_t = gate_row(0) * o_c + gate_row(1) * o_s + gate_row(2) * o_w
    o_rd = jnp.concatenate([o_t[:, r * TQ:(r + 1) * TQ] for r in range(HPG)], axis=0)
    o_ref[0] = (o_rd.T * sa_ref[0]).astype(BF16)


def _nsa_prompt(q_t, gt, sa, kcb, vcb_t, ks, vs_t, kw, vw_t, rel_bias):
    b, _, _, _, s = q_t.shape
    assert WINDOW == 2 * TQ and s % TQ == 0 and TQ >= MAX_DISTANCE
    nq = s // TQ
    ncp = s // CMP_STRIDE
    n_cmp = ncp - 1
    n_slc = -(-s // SEL_BLOCK)
    assert n_slc <= LANES and ncp >= 2 * CPT
    nl = HPG * TQ
    table = rel_bias.astype(F32)
    uu = np.arange(TQ)[:, None]
    tt = np.arange(TQ)[None, :]

    def per_group(tab):
        rows = tab.shape[1]
        return jnp.transpose(tab.reshape(N_KV, HPG, rows, TQ), (0, 2, 1, 3)).reshape(N_KV, rows, nl)

    table2 = table * LOG2E
    heads_first = lambda dist: jnp.moveaxis(_bias_lookup(table2, dist), -1, 0)
    far_h = _bias_lookup(table2, np.array([MAX_DISTANCE]))[0]
    far_b = far_h[:, None, None]
    tz0 = per_group(heads_first(tt - uu) - far_b)
    tz1 = per_group(heads_first(TQ + tt - uu) - far_b)
    far = per_group(jnp.broadcast_to(far_b, (N_HEADS, 8, TQ)))
    far_hi = far_h.astype(BF16)
    far_lo = (far_h - far_hi.astype(F32)).astype(BF16)
    far16 = jnp.zeros((N_HEADS, AUG_PAD, TQ), BF16).at[:, 0].set(far_hi[:, None]).at[:, 1].set(far_lo[:, None])
    far16 = per_group(far16)
    e = np.arange(2 * CPT)[:, None] - CPT
    pcd = per_group(heads_first(tt - CMP_STRIDE * e - (CMP_BLOCK - 1)) - far_b)
    c = np.arange(ncp)[None, :]
    j = np.arange(LANES)[:, None]
    cov_t = ((c * CMP_STRIDE < (j + 1) * SEL_BLOCK) & (c * CMP_STRIDE + CMP_BLOCK > j * SEL_BLOCK)
             & (c < n_cmp) & (j < n_slc))
    cov_t = jnp.asarray(cov_t, BF16)

    per_g = lambda rows: pl.BlockSpec((1, rows, nl), lambda bi, g, i: (g, 0, 0))
    return pl.pallas_call(
        functools.partial(_nsa_prompt_kernel, n_slc=n_slc),
        grid=(b, N_KV, nq),
        in_specs=[
            pl.BlockSpec((1, 1, HPG, HEAD_DIM, TQ), lambda bi, g, i: (bi, g, 0, 0, i)),
            pl.BlockSpec((1, TQ, LANES), lambda bi, g, i: (bi, i, g)),
            pl.BlockSpec((1, TQ, KV_WIDTH), lambda bi, g, i: (bi, i, g)),
            pl.BlockSpec((1, 1, ncp, HEAD_DIM), lambda bi, g, i: (bi, g, 0, 0)),
            pl.BlockSpec((1, 1, HEAD_DIM, ncp), lambda bi, g, i: (bi, g, 0, 0)),
            pl.BlockSpec((1, 1, s, K_SEL_AUG), lambda bi, g, i: (bi, g, 0, 0)),
            pl.BlockSpec((1, 1, nq, HEAD_DIM, TQ), lambda bi, g, i: (bi, g, 0, 0, 0)),
            pl.BlockSpec((1, 1, s, K_WIN_AUG), lambda bi, g, i: (bi, g, 0, 0)),
            pl.BlockSpec((1, 1, nq, HEAD_DIM, TQ), lambda bi, g, i: (bi, g, 0, 0, 0)),
            per_g(2 * CPT), per_g(8), per_g(AUG_PAD), per_g(TQ), per_g(TQ),
            pl.BlockSpec(cov_t.shape, lambda bi, g, i: (0, 0)),
        ],
        out_specs=pl.BlockSpec((1, TQ, KV_WIDTH), lambda bi, g, i: (bi, i, g)),
        out_shape=jax.ShapeDtypeStruct((b, s, NSA_WIDTH), BF16),
        scratch_shapes=[pltpu.VMEM((CPT + ncp, nl), F32), pltpu.VMEM((K_SEL_AUG, nl), BF16),
                        pltpu.VMEM((K_WIN_AUG, nl), BF16),
                        pltpu.VMEM((1, nl), F32), pltpu.VMEM((1, nl), F32), pltpu.VMEM((HEAD_DIM, nl), F32)],
        compiler_params=_cparams(("parallel", "parallel", "arbitrary")),
        name="nsa_prompt",
    )(q_t, gt, sa, kcb, vcb_t, ks, vs_t, kw, vw_t, pcd, far, far16, tz0, tz1, cov_t)


def _nsa_sample_kernel(*refs, n_pg, n_chunks, n_slc, past, t_new):
    (qbd_ref, kcbt_ref, vcb_ref, bc_ref, cov_ref, rmat_ref) = refs[1:7]
    kpg = refs[7:7 + n_pg]
    vpg = refs[7 + n_pg:7 + 2 * n_pg]
    (bsl_ref, ksn_ref, vsn_ref, bsn_ref, kwc_ref, vwc_ref, kwn_ref, vwn_ref, bw_ref, gt_ref, e4_ref,
     o_ref, sel_ref, m_ref, l_ref, acc_ref, oc_ref) = refs[7 + 2 * n_pg:]
    j = pl.program_id(1)
    qbd = qbd_ref[0]
    nsp = cov_ref.shape[1]
    n_win = kwc_ref.shape[2]

    def softmax_rows(s):
        m = jnp.max(s, axis=-1, keepdims=True)
        e = jnp.where(s > 0.5 * NEG, jnp.exp(s - m), 0.0)
        den = jnp.sum(e, axis=-1, keepdims=True)
        return e / jnp.where(den > 0, den, 1.0)

    def flash_step(s, v, v_is_transposed):
        m_old = m_ref[...]
        m_new = jnp.maximum(m_old, jnp.max(s, axis=-1, keepdims=True))
        alpha = jnp.exp(m_old - m_new)
        p = jnp.exp(s - m_new)
        l_ref[...] = alpha * l_ref[...] + jnp.sum(p, axis=-1, keepdims=True)
        pv = _dot_nt(p.astype(BF16), v) if v_is_transposed else _dot(p.astype(BF16), v)
        acc_ref[...] = alpha * acc_ref[...] + pv
        m_ref[...] = m_new

    @pl.when(j == 0)
    def _():
        p_c = softmax_rows(_dot(qbd, kcbt_ref[0]) + bc_ref[...])
        oc_ref[...] = _dot(p_c.astype(BF16), vcb_ref[0])
        hi, lo = _split_hi_lo(p_c)
        psum = _dot(rmat_ref[...], hi) + _dot(rmat_ref[...], lo)
        hi, lo = _split_hi_lo(psum)
        imp_t = (_dot(hi, cov_ref[...]) + _dot(lo, cov_ref[...])).T
        jrow = lax.broadcasted_iota(jnp.int32, (nsp, LANES), 0)
        lane = lax.broadcasted_iota(jnp.int32, (1, LANES), 1)
        qpos = past + lane % t_new
        cur = qpos // SEL_BLOCK
        causal = (jrow * SEL_BLOCK <= qpos) & (jrow < n_slc)
        forced = (jrow == 0) | (jrow == cur) | (jrow == cur - 1)
        score = jnp.where(forced, FORCE_SCORE, jnp.where(causal, imp_t, NEG))
        chosen = _select_top_blocks(score, jrow, min(N_SEL, n_slc))
        sel = jnp.where(causal & (lane < N_HEADS * t_new) & (chosen > 0.5), 0.0, -1.0).T
        for k in range(nsp // LANES):
            sel_ref[k] = sel[:, k * LANES:(k + 1) * LANES]
        m_ref[...] = jnp.full(m_ref.shape, NEG, F32)
        l_ref[...] = jnp.zeros(l_ref.shape, F32)
        acc_ref[...] = jnp.zeros(acc_ref.shape, F32)

    far = bsl_ref[:, PAGE_SIZE:2 * PAGE_SIZE]
    last = jnp.where(j == n_chunks - 1, bsl_ref[:, 0:PAGE_SIZE], far)
    chunks_per_tile = LANES // (n_pg * (PAGE_SIZE // SEL_BLOCK))
    sel_tile = sel_ref[j // chunks_per_tile].astype(BF16)
    pp = SAMPLE_PIECE_PAGES
    n_pieces = n_pg // pp

    def logits(t):
        k_t = jnp.concatenate([r[0] for r in kpg[t * pp:(t + 1) * pp]], axis=1).astype(BF16)
        bias = jnp.concatenate([far] * (pp - 1) + [last if t == n_pieces - 1 else far], axis=1)
        mask_add = _dot(sel_tile, e4_ref[j % chunks_per_tile, :, t * pp * PAGE_SIZE:(t + 1) * pp * PAGE_SIZE])
        return _dot(qbd, k_t) + bias + mask_add

    s = {t: logits(t) for t in range(min(PIPE_AHEAD, n_pieces))}
    pieces = []
    for t in range(n_pieces):
        m = jnp.max(s[t], axis=-1, keepdims=True)
        p = jnp.exp(s.pop(t) - m)
        if t + PIPE_AHEAD < n_pieces:
            s[t + PIPE_AHEAD] = logits(t + PIPE_AHEAD)
        v_t = jnp.concatenate([r[0] for r in vpg[t * pp:(t + 1) * pp]], axis=1).astype(BF16)
        pieces.append((_dot_nt(p.astype(BF16), v_t), m, jnp.sum(p, axis=-1, keepdims=True)))
    m_old = m_ref[...]
    m_new = functools.reduce(jnp.maximum, [m_old] + [m for _, m, _ in pieces])
    scales = [jnp.exp(m - m_new) for _, m, _ in pieces]
    a_old = jnp.exp(m_old - m_new)
    l_ref[...] = a_old * l_ref[...] + functools.reduce(jnp.add, [a * l for a, (_, _, l) in zip(scales, pieces)])
    acc_ref[...] = a_old * acc_ref[...] + functools.reduce(jnp.add, [a * acc for a, (acc, _, _) in zip(scales, pieces)])
    m_ref[...] = m_new

    @pl.when(j == n_chunks - 1)
    def _():
        sn = _dot_nt(qbd, ksn_ref[0].astype(BF16)) + bsn_ref[...]
        blk = n_slc - 1
        seln = sel_ref[blk // LANES][:, blk % LANES:blk % LANES + 1]
        flash_step(sn + seln * (-NEG), vsn_ref[0].astype(BF16), False)
        l = l_ref[...]
        o_s = acc_ref[...] / jnp.where(l > 0, l, 1.0)
        sw = jnp.concatenate([_dot(qbd, kwc_ref[0].astype(BF16)), _dot_nt(qbd, kwn_ref[0].astype(BF16))], axis=1)
        p_w = softmax_rows(sw + bw_ref[...]).astype(BF16)
        o_w = _dot_nt(p_w[:, :n_win], vwc_ref[0].astype(BF16)) + _dot(p_w[:, n_win:], vwn_ref[0].astype(BF16))
        gt = gt_ref[0]
        o_ref[0] = gt[:, 0:1] * oc_ref[...] + gt[:, 1:2] * o_s + gt[:, 2:3] * o_w


def _nsa_sample(q, gates, kcb_t, vcb, cache_k_slc, cache_v_slc, page_table, ks_new, vs_new,
                kw_cache, vw_cache, kw_new, vw_new, rel_bias, *, past):
    db, t_new, _ = q.shape
    n_pages = page_table.shape[1]
    assert past == n_pages * PAGE_SIZE and past % SEL_BLOCK == 0 and t_new <= SEL_BLOCK
    assert PAGE_SIZE >= MAX_DISTANCE
    n_cp = kcb_t.shape[2]
    n_slc = -(-(past + t_new) // SEL_BLOCK)
    nsp = -(-n_slc // LANES) * LANES
    n_pg = SAMPLE_PAGES
    n_chunks = n_pages // n_pg
    blocks_per_chunk = n_pg * (PAGE_SIZE // SEL_BLOCK)
    assert LANES % blocks_per_chunk == 0
    n_win = kw_cache.shape[2]
    nl = N_HEADS * t_new
    assert nl <= LANES
    row = np.arange(LANES)
    row_ok = (row < nl)[:, None]
    row_g = np.where(row < nl, row // (HPG * t_new), 0)
    row_t = (row % t_new)[:, None]
    q5 = q.reshape(db, t_new, N_KV, HPG, HEAD_DIM)
    qbd = jnp.einsum('btgrd,gk->bgrtkd', q5.astype(F32), jnp.eye(N_KV, dtype=F32)).reshape(db, nl, KV_WIDTH)
    qbd = jnp.pad(qbd, ((0, 0), (0, LANES - nl), (0, 0))).astype(BF16)

    table = rel_bias.astype(F32)

    def bias_tab(dist, valid):
        vals = jnp.moveaxis(_bias_lookup(table, dist), -1, 0)
        vals = jnp.where(jnp.asarray(valid)[None], vals, NEG).reshape(nl, -1)
        return jnp.pad(vals, ((0, LANES - nl), (0, 0)), constant_values=NEG)

    tq = np.arange(t_new)[:, None]
    qpos = past + tq
    cblk = np.arange(n_cp)[None, :]
    dist_c = qpos - (cblk * CMP_STRIDE + CMP_BLOCK - 1)
    bc = bias_tab(dist_c, (dist_c >= 0) & (cblk < n_cp - 1))
    kpos = past - PAGE_SIZE + np.arange(PAGE_SIZE)[None, :]
    all_ok = np.ones((t_new, PAGE_SIZE), bool)
    bsl = jnp.concatenate([bias_tab(qpos - kpos, all_ok),
                           bias_tab(np.full((t_new, PAGE_SIZE), MAX_DISTANCE), all_ok)], axis=1)
    u = np.arange(PAGE_SIZE)[None, :]
    new_ok = (u <= tq) & (u < t_new)
    bsn = bias_tab(tq - u, new_ok)
    wpos = past - n_win + np.arange(n_win)[None, :]
    dist_w = qpos - wpos
    bw = jnp.concatenate([bias_tab(dist_w, (dist_w >= 0) & (dist_w < WINDOW) & (wpos >= 0)),
                          bias_tab(tq - u, new_ok & (tq - u < WINDOW))], axis=1)
    c = np.arange(n_cp)[:, None]
    jb = np.arange(nsp)[None, :]
    cov = ((c * CMP_STRIDE < (jb + 1) * SEL_BLOCK) & (c * CMP_STRIDE + CMP_BLOCK > jb * SEL_BLOCK)
           & (c < n_cp - 1) & (jb < n_slc))
    cov = jnp.asarray(cov, BF16)
    same = (row_g[:, None] == row_g[None, :]) & (row_t == row_t.T) & row_ok & row_ok.T
    rmat = jnp.asarray(same, BF16)
    kk = np.arange(n_pg * PAGE_SIZE)[None, None, :] // SEL_BLOCK
    e4 = np.arange(LANES)[None, :, None] == (np.arange(LANES // blocks_per_chunk)[:, None, None] * blocks_per_chunk + kk)
    e4 = jnp.asarray(np.where(e4, -NEG, 0.0), BF16)
    g5 = gates.reshape(db, t_new, N_KV, LANES)[..., :3 * HPG].reshape(db, t_new, N_KV, 3, HPG)
    gcol = jnp.transpose(g5, (0, 2, 4, 1, 3)).reshape(db, nl, 3)
    gcol = jnp.pad(gcol, ((0, 0), (0, LANES - nl), (0, 5)))

    def pad_new(a):
        return jnp.pad(a, ((0, 0), (0, PAGE_SIZE - t_new), (0, 0)))

    ksn, vsn, kwn, vwn = (pad_new(a) for a in (ks_new, vs_new, kw_new, vw_new))

    per_b = lambda shape: pl.BlockSpec((1,) + shape, lambda b, jc, pt: (b, 0, 0))
    full = lambda a: pl.BlockSpec(a.shape, lambda b, jc, pt: (0,) * a.ndim)
    page = lambda p: pl.BlockSpec((1, KV_WIDTH, PAGE_SIZE), lambda b, jc, pt: (pt[b, jc * SAMPLE_PAGES + p], 0, 0))
    in_specs = ([per_b((LANES, KV_WIDTH)), per_b((KV_WIDTH, n_cp)), per_b((n_cp, KV_WIDTH)),
                 full(bc), full(cov), full(rmat)]
                + [page(p) for p in range(n_pg)] * 2
                + [full(bsl), per_b((PAGE_SIZE, KV_WIDTH)), per_b((PAGE_SIZE, KV_WIDTH)), full(bsn),
                   per_b((KV_WIDTH, n_win)), per_b((KV_WIDTH, n_win)),
                   per_b((PAGE_SIZE, KV_WIDTH)), per_b((PAGE_SIZE, KV_WIDTH)), full(bw), per_b((LANES, 8)), full(e4)])
    o = pl.pallas_call(
        functools.partial(_nsa_sample_kernel, n_pg=n_pg, n_chunks=n_chunks, n_slc=n_slc, past=past, t_new=t_new),
        grid_spec=pltpu.PrefetchScalarGridSpec(
            num_scalar_prefetch=1, grid=(db, n_chunks), in_specs=in_specs,
            out_specs=pl.BlockSpec((1, LANES, KV_WIDTH), lambda b, jc, pt: (b, 0, 0)),
            scratch_shapes=[pltpu.VMEM((nsp // LANES, LANES, LANES), F32), pltpu.VMEM((LANES, 1), F32),
                            pltpu.VMEM((LANES, 1), F32), pltpu.VMEM((LANES, KV_WIDTH), F32),
                            pltpu.VMEM((LANES, KV_WIDTH), F32)]),
        out_shape=jax.ShapeDtypeStruct((db, LANES, KV_WIDTH), F32),
        compiler_params=_cparams(("arbitrary", "arbitrary")),
        name="nsa_sample",
    )(page_table, qbd, kcb_t, vcb, bc, cov, rmat, *([cache_k_slc] * n_pg), *([cache_v_slc] * n_pg),
      bsl, ksn, vsn, bsn, kw_cache, vw_cache, kwn, vwn, bw, gcol, e4)
    o6 = o[:, :nl].reshape(db, N_KV, HPG, t_new, N_KV, HEAD_DIM)
    o_diag = jnp.stack([o6[:, g, :, :, g] for g in range(N_KV)], axis=1)
    return jnp.transpose(o_diag, (0, 3, 1, 2, 4)).reshape(db, t_new, NSA_WIDTH)


def _conv_tail(y, cb_ref, lg_ref, lb_ref, wpw_ref, bpw_ref, sb):
    y = y + cb_ref[...]
    mu = jnp.mean(y, axis=-1, keepdims=True)
    yc = y - mu
    var = jnp.mean(yc * yc, axis=-1, keepdims=True)
    yn = yc * lax.rsqrt(var + LN_EPS) * lg_ref[...] + lb_ref[...]
    act = yn * _sigmoid(yn)
    return ((_dot(act.astype(BF16), wpw_ref[...]) + bpw_ref[...]) * sb).astype(BF16)


def _conv_prompt_kernel(c_ref, halo_ref, init_ref, cw_ref, cb_ref, lg_ref, lb_ref, wpw_ref, bpw_ref, sb_ref,
                        o_ref, full_ref, sh_ref, y_ref, *, ts):
    j = pl.program_id(1)
    full_ref[CONV_HALO:CONV_HALO + ts, :] = c_ref[0]

    @pl.when(j == 0)
    def _():
        full_ref[0:CONV_HALO, :] = init_ref[0]

    @pl.when(j > 0)
    def _():
        full_ref[0:CONV_HALO, :] = halo_ref[0]

    first = CONV_HALO - (CONV_WIDTH - 1)
    span = sh_ref.shape[1]
    for sft in range(1, 8):
        sh_ref[sft - 1] = full_ref[sft:sft + span, :]
    rb = 64
    ch = full_ref.shape[1]
    for c0 in range(0, ch, LANES):
        for r0 in range(0, ts, rb):
            acc = jnp.zeros((rb, LANES), F32)
            for w in range(CONV_WIDTH):
                sft = (first + w) % 8
                base = r0 + first + w - sft
                if sft == 0:
                    x = full_ref[base:base + rb, c0:c0 + LANES]
                else:
                    x = sh_ref[sft - 1, base:base + rb, c0:c0 + LANES]
                acc = acc + x * cw_ref[w:w + 1, c0:c0 + LANES]
            y_ref[r0:r0 + rb, c0:c0 + LANES] = acc
    o_ref[0] = _conv_tail(y_ref[...], cb_ref, lg_ref, lb_ref, wpw_ref, bpw_ref, sb_ref[0])


def _conv_sample_kernel(c_ref, st_ref, cw_ref, cb_ref, lg_ref, lb_ref, wpw_ref, bpw_ref, sb_ref,
                        o_ref, full_ref, y_ref):
    nb, t_new, _ = c_ref.shape
    n_st = st_ref.shape[1]
    full_ref[:, 0:n_st, :] = st_ref[...]
    full_ref[:, n_st:n_st + t_new, :] = c_ref[...]
    first = n_st - (CONV_WIDTH - 1)
    for b in range(nb):
        acc = jnp.zeros((t_new, full_ref.shape[2]), F32)
        for w in range(CONV_WIDTH):
            acc = acc + full_ref[b, first + w:first + w + t_new, :] * cw_ref[w:w + 1, :]
        y_ref[b * t_new:(b + 1) * t_new, :] = acc
    o_ref[...] = _conv_tail(y_ref[...], cb_ref, lg_ref, lb_ref, wpw_ref, bpw_ref, sb_ref[...])


def _conv_params(conv_w, conv_b, ln_g, ln_b, w_pw, b_pw):
    ch = conv_w.shape[1]
    cw = jnp.pad(conv_w, ((0, 32 - CONV_WIDTH), (0, 0)))
    return (cw, conv_b.reshape(1, ch), ln_g.reshape(1, ch), ln_b.reshape(1, ch), w_pw.astype(BF16),
            b_pw.reshape(1, ch))


def _conv_prompt(c_in, init, params, sb, *, ts):
    b, s, ch = c_in.shape
    hb = ts // CONV_HALO
    const = lambda a: pl.BlockSpec(a.shape, lambda bi, j: (0,) * a.ndim)
    return pl.pallas_call(
        functools.partial(_conv_prompt_kernel, ts=ts),
        grid=(b, s // ts),
        in_specs=[pl.BlockSpec((1, ts, ch), lambda bi, j: (bi, j, 0)),
                  pl.BlockSpec((1, CONV_HALO, ch), lambda bi, j: (bi, jnp.maximum(j * hb - 1, 0), 0)),
                  pl.BlockSpec((1, CONV_HALO, ch), lambda bi, j: (bi, 0, 0))]
        + [const(a) for a in params]
        + [pl.BlockSpec((1, ts, ch), lambda bi, j: (bi, j, 0))],
        out_specs=pl.BlockSpec((1, ts, ch), lambda bi, j: (bi, j, 0)),
        out_shape=jax.ShapeDtypeStruct((b, s, ch), BF16),
        scratch_shapes=[pltpu.VMEM((CONV_HALO + ts, ch), F32), pltpu.VMEM((7, CONV_HALO + ts - 8, ch), F32),
                        pltpu.VMEM((ts, ch), F32)],
        compiler_params=_cparams(("parallel", "arbitrary")),
        name="conv_prompt",
    )(c_in, c_in, init, *params, sb)


def _conv_sample(c_in, state, params, sb):
    db, t_new, ch = c_in.shape
    n_st = state.shape[1]
    rows_pad = -(-(n_st + t_new) // 8) * 8
    return pl.pallas_call(
        _conv_sample_kernel,
        out_shape=jax.ShapeDtypeStruct((db * t_new, ch), BF16),
        scratch_shapes=[pltpu.VMEM((db, rows_pad, ch), F32), pltpu.VMEM((db * t_new, ch), F32)],
        compiler_params=pltpu.CompilerParams(vmem_limit_bytes=VMEM_LIMIT),
        name="conv_sample",
    )(c_in, state, *params, sb)


def _out_kernel(x_ref, ma_ref, mb_ref, *rest, gated):
    if gated:
        sa_ref, wa_ref, wb_ref, gp_ref, y_ref = rest
        ma = (ma_ref[0] * sa_ref[0]).astype(BF16)
    else:
        wa_ref, wb_ref, gp_ref, y_ref = rest
        ma = ma_ref[0]
    z = _dot(ma, wa_ref[...]) + _dot(mb_ref[0], wb_ref[...])
    ms = jnp.mean(z * z, axis=-1, keepdims=True)
    y_ref[0] = x_ref[0] + z * lax.rsqrt(ms + RMS_EPS) * gp_ref[...]


def _out_proj(x, ma, mb, sa, w_out, g_post, *, tm):
    b, s, d = x.shape
    na = ma.shape[-1]
    wa = w_out[:na].astype(BF16)
    wb = w_out[na:].astype(BF16)
    row = lambda width: pl.BlockSpec((1, tm, width), lambda bi, i: (bi, i, 0))
    const = lambda a: pl.BlockSpec(a.shape, lambda bi, i: (0, 0))
    gp = g_post.reshape(1, d)
    gate_in, gate_spec = ([sa], [row(na)]) if sa is not None else ([], [])
    return pl.pallas_call(
        functools.partial(_out_kernel, gated=sa is not None),
        grid=(b, s // tm),
        in_specs=[row(d), row(na), row(mb.shape[-1])] + gate_spec + [const(wa), const(wb), const(gp)],
        out_specs=row(d),
        out_shape=jax.ShapeDtypeStruct((b, s, d), F32),
        compiler_params=_cparams(("parallel", "parallel")),
        name="out_proj",
    )(x, ma, mb, *gate_in, wa, wb, gp)


def _split_w_in(w_in):
    d = w_in.shape[0]
    c0 = NSA_WIDTH + 6 * KV_WIDTH
    n_gate = 3 * N_HEADS
    conv_ch = (w_in.shape[1] - c0 - n_gate - NSA_WIDTH) // 3
    w_qkv = w_in[:, :c0].astype(BF16)
    wg = w_in[:, c0:c0 + n_gate].reshape(d, N_KV, HPG, 3)
    wg = jnp.transpose(wg, (0, 1, 3, 2)).reshape(d, N_KV, 3 * HPG)
    wg = jnp.pad(wg, ((0, 0), (0, 0), (0, LANES - 3 * HPG))).reshape(d, N_KV * LANES)
    z_a = w_in[:, c0 + n_gate:c0 + n_gate + NSA_WIDTH]
    glu0 = c0 + n_gate + NSA_WIDTH
    w_glu = w_in[:, glu0:glu0 + 2 * conv_ch].astype(BF16)
    z_b = w_in[:, glu0 + 2 * conv_ch:]
    assert conv_ch == NSA_WIDTH
    w_gate = jnp.concatenate([z_a, z_b, wg], axis=1).astype(BF16)
    return w_qkv, w_gate, w_glu


def _token_minor(cache):
    n, tokens = cache.shape[:2]
    return jnp.transpose(cache, (0, 2, 3, 1)).reshape(n, KV_WIDTH, tokens)


def kernel(x_prompt, x_sample, cache_k_cmp, cache_v_cmp, cache_k_slc, cache_v_slc, cache_k_win, cache_v_win,
           state_conv, page_table, g_pre, w_in, cmp_w1_k, cmp_w2_k, cmp_pe_k, cmp_w1_v, cmp_w2_v, cmp_pe_v,
           rel_bias, conv_w, conv_b, ln_g, ln_b, w_pw, b_pw, w_out, g_post):
    depth = g_pre.shape[0]
    assert depth == 1
    layer = 0
    b, s, d = x_prompt.shape
    db, t_new, _ = x_sample.shape
    past = page_table.shape[1] * PAGE_SIZE
    conv_ch = conv_w.shape[-1]

    w_qkv, w_gate, w_glu = _split_w_in(w_in[layer])
    wts_k = _cmp_weights(cmp_w1_k[layer], cmp_w2_k[layer], cmp_pe_k[layer])
    wts_v = _cmp_weights(cmp_w1_v[layer], cmp_w2_v[layer], cmp_pe_v[layer])
    cparams = _conv_params(conv_w[layer], conv_b[layer], ln_g[layer], ln_b[layer], w_pw[layer], b_pw[layer])
    chunk_w = CMP_STRIDE * KV_WIDTH

    (q_t, kc, vc, ks, vs, kw, vw, ks_g, kw_g, vs_t, vw_t), sa, sb, gt, c_in = _projections(
        x_prompt, g_pre[layer], w_qkv, w_gate, w_glu, tm=512, attn_layouts=True)
    n_ch = s // CMP_STRIDE
    kcb = _compress(kc[:, :n_ch * CMP_STRIDE].reshape(b, n_ch, chunk_w), wts_k, layout="group_rows")
    vcb_t = _compress(vc[:, :n_ch * CMP_STRIDE].reshape(b, n_ch, chunk_w), wts_v, layout="group_cols")
    ma = _nsa_prompt(q_t, gt, sa, kcb, vcb_t, ks_g, vs_t, kw_g, vw_t, rel_bias)
    mb = _conv_prompt(c_in, jnp.zeros((b, CONV_HALO, conv_ch), F32), cparams, sb, ts=256)
    y_prompt = _out_proj(x_prompt, ma, mb, None, w_out[layer], g_post[layer], tm=512)
    n_keep = min(WINDOW, s)
    kv5 = lambda a: a.reshape(1, a.shape[0], a.shape[1], N_KV, HEAD_DIM)
    outs_p = (kv5(kc), kv5(vc), kv5(ks), kv5(vs), kv5(kw[:, -n_keep:]), kv5(vw[:, -n_keep:]),
              c_in[None, :, -(CONV_WIDTH - 1):])

    xs = x_sample.reshape(1, db * t_new, d)
    (q_s, kc_s, vc_s, ks_s, vs_s, kw_s, vw_s), sa_s, sb_s, gt_s, c_s = _projections(
        xs, g_pre[layer], w_qkv, w_gate, w_glu, tm=db * t_new, attn_layouts=False)
    tok = lambda a: a.reshape(db, t_new, a.shape[-1])
    kcb_s = _compress(_token_minor(cache_k_cmp[layer]), wts_k, page_table=page_table, layout="cols")
    vcb_s = _compress(_token_minor(cache_v_cmp[layer]), wts_v, page_table=page_table, layout="rows")
    o_a = _nsa_sample(tok(q_s), tok(gt_s), kcb_s, vcb_s,
                      _token_minor(cache_k_slc[layer]), _token_minor(cache_v_slc[layer]), page_table,
                      tok(ks_s), tok(vs_s), _token_minor(cache_k_win[layer]), _token_minor(cache_v_win[layer]),
                      tok(kw_s), tok(vw_s), rel_bias, past=past)
    mb_s = _conv_sample(tok(c_s), state_conv[layer], cparams, sb_s[0])
    y_sample = _out_proj(xs, o_a.reshape(1, db * t_new, NSA_WIDTH), mb_s[None], sa_s, w_out[layer],
                         g_post[layer], tm=db * t_new).reshape(db, t_new, d)
    n_keep_s = min(WINDOW, past + t_new)
    kv5s = lambda a: a.reshape(1, db, t_new, N_KV, HEAD_DIM)
    win = lambda cache, new: jnp.concatenate(
        [cache[layer], new.reshape(db, t_new, N_KV, HEAD_DIM)], axis=1)[None, :, -n_keep_s:]
    conv_s = jnp.concatenate([state_conv[layer], tok(c_s)], axis=1)[None, :, -(CONV_WIDTH - 1):]
    outs_s = (kv5s(kc_s), kv5s(vc_s), kv5s(ks_s), kv5s(vs_s), win(cache_k_win, kw_s), win(cache_v_win, vw_s), conv_s)
    return (y_prompt, y_sample) + outs_p + outs_s
```

```python
import functools
import math

import numpy as np
import jax
import jax.numpy as jnp
from jax import lax
from jax.experimental import pallas as pl
from jax.experimental.pallas import tpu as pltpu

F32 = jnp.float32
BF16 = jnp.bfloat16

HEAD_DIM = 64
N_KV = 4
HPG = 4
N_HEADS = N_KV * HPG
KV_WIDTH = N_KV * HEAD_DIM
NSA_WIDTH = N_HEADS * HEAD_DIM
CMP_BLOCK = 32
CMP_STRIDE = 16
SEL_BLOCK = 64
N_SEL = 16
WINDOW = 512
CONV_WIDTH = 31
N_BUCKETS = 32
MAX_DISTANCE = 128
FORCE_SCORE = 1e6
RMS_EPS = 1e-6
LN_EPS = 1e-5
PAGE_SIZE = 128
SCALE = HEAD_DIM ** -0.5

NEG = -1e30
PICKED = -3e38
LANES = 128
VMEM_LIMIT = 56 * 1024 * 1024

LOG2E = 1.4426950408889634
TQ = 256
FAR_SPAN = 1
FAR_GROUP = 4
PIPE_AHEAD = 2
AUG_PAD = 16
K_WIN_AUG = HEAD_DIM + AUG_PAD
K_SEL_AUG = HEAD_DIM + LANES + AUG_PAD
CPT = TQ // CMP_STRIDE
CMP_PATH_BLOCKS = 128
CONV_HALO = 32
SAMPLE_PAGES = 32
SAMPLE_PIECE_PAGES = 8
CMP_PAGES = 32


def _cparams(sem):
    return pltpu.CompilerParams(dimension_semantics=sem, vmem_limit_bytes=VMEM_LIMIT)


def _dot(a, b):
    return jnp.dot(a, b, preferred_element_type=F32)


def _dot_nt(a, b):
    return lax.dot_general(a, b, (((1,), (1,)), ((), ())), preferred_element_type=F32)


def _sigmoid(x):
    return 1.0 / (1.0 + jnp.exp(-x))


def _split_hi_lo(x):
    hi = x.astype(BF16)
    lo = (x - hi.astype(F32)).astype(BF16)
    return hi, lo


def _normed(x_ref, g_ref):
    x = x_ref[0]
    ms = jnp.mean(x * x, axis=-1, keepdims=True)
    return (x * lax.rsqrt(ms + RMS_EPS) * g_ref[...]).astype(BF16)


def _proj_qkv_kernel(x_ref, g_ref, w_ref, q_ref, kc_ref, vc_ref, ks_ref, vs_ref, kw_ref, vw_ref,
                     *attn_refs, attn_layouts):
    h = _normed(x_ref, g_ref)
    tm = h.shape[0]
    if attn_layouts:
        tok = pl.program_id(1) * tm + lax.broadcasted_iota(jnp.int32, (tm, LANES), 0)
        lane = lax.broadcasted_iota(jnp.int32, (tm, LANES), 1)
        blk_cols = jnp.where(lane == tok // SEL_BLOCK, 1.0, 0.0).astype(BF16)
        one_cols = jnp.where(lax.broadcasted_iota(jnp.int32, (tm, AUG_PAD), 1) < 2, 1.0, 0.0).astype(BF16)
    for g in range(N_KV):
        res = _dot(h, w_ref[:, g * KV_WIDTH:(g + 1) * KV_WIDTH]) * (SCALE * LOG2E if attn_layouts else SCALE)
        if attn_layouts:
            res_t = res.T
            for r in range(HPG):
                q_ref[0, g, r] = res_t[r * HEAD_DIM:(r + 1) * HEAD_DIM, :].astype(BF16)
        else:
            q_ref[0, :, g * KV_WIDTH:(g + 1) * KV_WIDTH] = res.astype(BF16)
    for j, o_ref in enumerate((kc_ref, vc_ref, ks_ref, vs_ref, kw_ref, vw_ref)):
        c0 = NSA_WIDTH + j * KV_WIDTH
        res = _dot(h, w_ref[:, c0:c0 + KV_WIDTH])
        o_ref[0] = res
        if attn_layouts and j in (2, 4):
            extra = [blk_cols, one_cols] if j == 2 else [one_cols]
            for g in range(N_KV):
                attn_refs[j // 2 - 1][0, g] = jnp.concatenate(
                    [res[:, g * HEAD_DIM:(g + 1) * HEAD_DIM].astype(BF16)] + extra, axis=1)
        if attn_layouts and j in (3, 5):
            res_t = res.T.astype(BF16)
            for g in range(N_KV):
                for kt in range(tm // TQ):
                    attn_refs[2 + j // 2 - 1][0, g, kt] = res_t[g * HEAD_DIM:(g + 1) * HEAD_DIM, kt * TQ:(kt + 1) * TQ]


def _proj_gate_kernel(x_ref, g_ref, w_ref, sa_ref, sb_ref, gt_ref):
    h = _normed(x_ref, g_ref)
    for o_ref, base in ((sa_ref, 0), (sb_ref, NSA_WIDTH)):
        for c in range(NSA_WIDTH // 256):
            z = _dot(h, w_ref[:, base + c * 256: base + (c + 1) * 256])
            o_ref[0, :, c * 256:(c + 1) * 256] = z * _sigmoid(z)
    for c in range(2):
        z = _dot(h, w_ref[:, 2 * NSA_WIDTH + c * 256: 2 * NSA_WIDTH + (c + 1) * 256])
        gt_ref[0, :, c * 256:(c + 1) * 256] = _sigmoid(z)


def _proj_glu_kernel(x_ref, g_ref, w_ref, c_ref):
    h = _normed(x_ref, g_ref)
    n = c_ref.shape[-1]
    for c in range(n // 256):
        a = _dot(h, w_ref[:, c * 256:(c + 1) * 256])
        gg = _dot(h, w_ref[:, n + c * 256: n + (c + 1) * 256])
        c_ref[0, :, c * 256:(c + 1) * 256] = a * _sigmoid(gg)


def _projections(x, g_pre, w_qkv, w_gate, w_glu, *, tm, attn_layouts):
    b, s, d = x.shape
    grid = (b, s // tm)
    x_spec = pl.BlockSpec((1, tm, d), lambda bi, i: (bi, i, 0))
    g_spec = pl.BlockSpec((1, d), lambda bi, i: (0, 0))

    def w_spec(w):
        return pl.BlockSpec(w.shape, lambda bi, i: (0, 0))

    def row_spec(width):
        return pl.BlockSpec((1, tm, width), lambda bi, i: (bi, i, 0))

    kv_shape = jax.ShapeDtypeStruct((b, s, KV_WIDTH), F32)
    if attn_layouts:
        assert tm % TQ == 0
        q_shape = jax.ShapeDtypeStruct((b, N_KV, HPG, HEAD_DIM, s), BF16)
        q_spec = pl.BlockSpec((1, N_KV, HPG, HEAD_DIM, tm), lambda bi, i: (bi, 0, 0, 0, i))
        k_shape = lambda w: jax.ShapeDtypeStruct((b, N_KV, s, w), BF16)
        k_spec = lambda w: pl.BlockSpec((1, N_KV, tm, w), lambda bi, i: (bi, 0, i, 0))
        v_shape = jax.ShapeDtypeStruct((b, N_KV, s // TQ, HEAD_DIM, TQ), BF16)
        v_spec = pl.BlockSpec((1, N_KV, tm // TQ, HEAD_DIM, TQ), lambda bi, i: (bi, 0, i, 0, 0))
        extra_shape = [k_shape(K_SEL_AUG), k_shape(K_WIN_AUG), v_shape, v_shape]
        extra_spec = [k_spec(K_SEL_AUG), k_spec(K_WIN_AUG), v_spec, v_spec]
    else:
        q_shape = jax.ShapeDtypeStruct((b, s, NSA_WIDTH), BF16)
        q_spec = row_spec(NSA_WIDTH)
        extra_shape, extra_spec = [], []
    g2 = g_pre.reshape(1, d)
    qkv = pl.pallas_call(
        functools.partial(_proj_qkv_kernel, attn_layouts=attn_layouts),
        grid=grid,
        in_specs=[x_spec, g_spec, w_spec(w_qkv)],
        out_specs=[q_spec] + [row_spec(KV_WIDTH)] * 6 + extra_spec,
        out_shape=[q_shape] + [kv_shape] * 6 + extra_shape,
        compiler_params=_cparams(("parallel", "parallel")),
        name="proj_qkv",
    )(x, g2, w_qkv)
    sa, sb, gt = pl.pallas_call(
        _proj_gate_kernel,
        grid=grid,
        in_specs=[x_spec, g_spec, w_spec(w_gate)],
        out_specs=[row_spec(NSA_WIDTH), row_spec(NSA_WIDTH), row_spec(N_KV * LANES)],
        out_shape=[jax.ShapeDtypeStruct((b, s, NSA_WIDTH), F32)] * 2
        + [jax.ShapeDtypeStruct((b, s, N_KV * LANES), F32)],
        compiler_params=_cparams(("parallel", "parallel")),
        name="proj_gate",
    )(x, g2, w_gate)
    c_in = pl.pallas_call(
        _proj_glu_kernel,
        grid=grid,
        in_specs=[x_spec, g_spec, w_spec(w_glu)],
        out_specs=row_spec(w_glu.shape[1] // 2),
        out_shape=jax.ShapeDtypeStruct((b, s, w_glu.shape[1] // 2), F32),
        compiler_params=_cparams(("parallel", "parallel")),
        name="proj_glu",
    )(x, g2, w_glu)
    return qkv, sa, sb, gt, c_in


def _cmp1_rows_kernel(x_ref, pe_ref, w_ref, pre_ref, pepre_ref):
    w = w_ref[...].reshape(-1, w_ref.shape[-1])
    pre_ref[0] = _dot(x_ref[0].astype(BF16), w)
    pepre_ref[...] = _dot(pe_ref[...], w)


def _cmp1_paged_kernel(*refs, n_in):
    x_refs, (perm_ref, pe_ref, w_ref, pre_ref, pepre_ref) = refs[1:1 + n_in], refs[1 + n_in:]
    perm = perm_ref[...]
    zs = [_dot_nt(perm, r[0].astype(BF16)) for r in x_refs]
    rpp = PAGE_SIZE // CMP_STRIDE
    acc = jnp.zeros((n_in * rpp, w_ref.shape[-1]), F32)
    for c in range(CMP_STRIDE):
        xc = jnp.concatenate([z[c * rpp:(c + 1) * rpp] for z in zs], axis=0)
        acc = acc + _dot(xc.astype(BF16), w_ref[c])
    pre_ref[0] = acc
    pepre_ref[...] = _dot(pe_ref[...], w_ref[...].reshape(-1, w_ref.shape[-1]))


def _cmp2_kernel(pre_ref, pepre_ref, w2_ref, o_ref, *, layout):
    pre = pre_ref[0]
    n_ch = pre.shape[0]
    a = pre[:, :KV_WIDTH]
    b_next = pltpu.roll(pre[:, KV_WIDTH:], n_ch - 1, axis=0)
    pe_bias = pepre_ref[0:1, :KV_WIDTH] + pepre_ref[1:2, KV_WIDTH:]
    z = a + b_next + pe_bias
    hid = 0.5 * z * (1.0 + jnp.tanh(math.sqrt(2.0 / math.pi) * (z + 0.044715 * (z * z * z))))
    out = _dot(hid.astype(BF16), w2_ref[...])
    if layout == "group_rows":
        for g in range(N_KV):
            o_ref[0, g] = out[:, g * HEAD_DIM:(g + 1) * HEAD_DIM].astype(BF16)
    elif layout == "group_cols":
        out_t = out.T.astype(BF16)
        for g in range(N_KV):
            o_ref[0, g] = out_t[g * HEAD_DIM:(g + 1) * HEAD_DIM, :]
    elif layout == "rows":
        o_ref[0] = out.astype(BF16)
    else:
        o_ref[0] = out.T.astype(BF16)


def _cmp_weights(w1, w2, pe):
    eye = jnp.eye(N_KV, dtype=F32)
    halves = w1.reshape(2, CMP_STRIDE, HEAD_DIM, -1)
    hdim = halves.shape[-1]
    wbig = jnp.einsum('acdh,gk->cgdakh', halves, eye)
    wbig = wbig.reshape(CMP_STRIDE, KV_WIDTH, 2 * N_KV * hdim).astype(BF16)
    w2big = jnp.einsum('hd,gk->ghkd', w2, eye).reshape(N_KV * hdim, KV_WIDTH).astype(BF16)
    pe_rows = jnp.broadcast_to(pe.reshape(2, CMP_STRIDE, 1, HEAD_DIM), (2, CMP_STRIDE, N_KV, HEAD_DIM))
    pe8 = jnp.zeros((8, CMP_STRIDE * KV_WIDTH), F32).at[:2].set(pe_rows.reshape(2, -1)).astype(BF16)
    return wbig, w2big, pe8


def _compress(x_view, wts, *, page_table=None, layout):
    wbig, w2big, pe8 = wts
    ncol = wbig.shape[-1]
    kdim = wbig.shape[0] * wbig.shape[1]
    if page_table is None:
        b, n_ch, _ = x_view.shape
        rows = min(n_ch, 256)
        const2 = lambda bi, i: (0, 0)
        pre, pepre = pl.pallas_call(
            _cmp1_rows_kernel, grid=(b, n_ch // rows),
            in_specs=[pl.BlockSpec((1, rows, kdim), lambda bi, i: (bi, i, 0)), pl.BlockSpec(pe8.shape, const2),
                      pl.BlockSpec(wbig.shape, lambda bi, i: (0, 0, 0))],
            out_specs=[pl.BlockSpec((1, rows, ncol), lambda bi, i: (bi, i, 0)), pl.BlockSpec((8, ncol), const2)],
            out_shape=[jax.ShapeDtypeStruct((b, n_ch, ncol), F32), jax.ShapeDtypeStruct((8, ncol), F32)],
            compiler_params=_cparams(("arbitrary", "arbitrary")), name="cmp_stage1")(x_view, pe8, wbig)
    else:
        b, n_pages = page_table.shape
        n_in = CMP_PAGES
        rpp = PAGE_SIZE // CMP_STRIDE
        rows = n_in * rpp
        n_ch = n_pages * rpp
        const2 = lambda bi, i, pt: (0, 0)
        x_specs = [pl.BlockSpec((1, KV_WIDTH, PAGE_SIZE), functools.partial(
            lambda bi, i, pt, p: (pt[bi, i * CMP_PAGES + p], 0, 0), p=p)) for p in range(n_in)]
        tok = np.arange(PAGE_SIZE)
        perm = (tok[None, :] == (tok[:, None] % rpp) * CMP_STRIDE + tok[:, None] // rpp)
        perm = jnp.asarray(perm, BF16)
        pre, pepre = pl.pallas_call(
            functools.partial(_cmp1_paged_kernel, n_in=n_in),
            grid_spec=pltpu.PrefetchScalarGridSpec(
                num_scalar_prefetch=1, grid=(b, n_pages // n_in),
                in_specs=x_specs + [pl.BlockSpec(perm.shape, const2), pl.BlockSpec(pe8.shape, const2),
                                    pl.BlockSpec(wbig.shape, lambda bi, i, pt: (0, 0, 0))],
                out_specs=[pl.BlockSpec((1, rows, ncol), lambda bi, i, pt: (bi, i, 0)),
                           pl.BlockSpec((8, ncol), const2)]),
            out_shape=[jax.ShapeDtypeStruct((b, n_ch, ncol), F32), jax.ShapeDtypeStruct((8, ncol), F32)],
            compiler_params=_cparams(("arbitrary", "arbitrary")), name="cmp_stage1_paged",
        )(page_table, *([x_view] * n_in), perm, pe8, wbig)
    o_dims = {"group_rows": (N_KV, n_ch, HEAD_DIM), "group_cols": (N_KV, HEAD_DIM, n_ch),
              "rows": (n_ch, KV_WIDTH), "cols": (KV_WIDTH, n_ch)}[layout]
    return pl.pallas_call(
        functools.partial(_cmp2_kernel, layout=layout),
        grid=(b,),
        in_specs=[pl.BlockSpec((1, n_ch, ncol), lambda bi: (bi, 0, 0)),
                  pl.BlockSpec((8, ncol), lambda bi: (0, 0)),
                  pl.BlockSpec(w2big.shape, lambda bi: (0, 0))],
        out_specs=pl.BlockSpec((1,) + o_dims, lambda bi: (bi,) + (0,) * len(o_dims)),
        out_shape=jax.ShapeDtypeStruct((b,) + o_dims, BF16),
        compiler_params=_cparams(("parallel",)), name="cmp_stage2",
    )(pre, pepre, w2big)


def _t5_bucket(dist):
    dist = np.maximum(np.asarray(dist, np.int64), 0)
    max_exact = N_BUCKETS // 2
    d32 = np.maximum(dist, 1).astype(np.float32)
    large = max_exact + (np.log(d32 / np.float32(max_exact)) / np.float32(math.log(MAX_DISTANCE / max_exact))
                         * np.float32(N_BUCKETS - max_exact)).astype(np.int32)
    large = np.minimum(large, N_BUCKETS - 1)
    return np.where(dist < max_exact, dist, large).astype(np.int32)


def _bias_lookup(table, dist):
    bucket = jnp.asarray(_t5_bucket(dist).reshape(-1, 1))
    onehot = (bucket == jnp.arange(N_BUCKETS, dtype=jnp.int32)[None, :]).astype(F32)
    out = jnp.dot(onehot, table, precision=lax.Precision.HIGHEST)
    return out.reshape(tuple(np.shape(dist)) + (table.shape[1],))


def _top_block_round(chosen, work, jrow):
    m = jnp.max(work, axis=0, keepdims=True)
    idx = jnp.min(jnp.where(work == m, jrow, 1 << 20), axis=0, keepdims=True)
    pick = jrow == idx
    return jnp.where(pick, 1.0, chosen), jnp.where(pick, PICKED, work)


def _select_top_blocks(score, jrow, k_sel):
    state = (jnp.zeros(score.shape, F32), score)
    for _ in range(k_sel):
        state = _top_block_round(*state, jrow)
    return state[0]


def _nsa_prompt_kernel(q_ref, gt_ref, sa_ref, kcb_ref, vcbt_ref, ks_ref, vst_ref, kw_ref, vwt_ref,
                       pcd_ref, far_ref, far16_ref, tz0_ref, tz1_ref, covt_ref, o_ref,
                       lc_ref, qs_ref, qw_ref, m_ref, l_ref, acc_ref, oc_ref, imp_ref, *, n_slc):
    i = pl.program_id(2)
    nl = HPG * TQ
    ncp = kcb_ref.shape[2]
    q_t = jnp.concatenate([q_ref[0, 0, r] for r in range(HPG)], axis=1)
    lane_t = lax.broadcasted_iota(jnp.int32, (1, nl), 1) & (TQ - 1)
    qpos = i * TQ + lane_t
    far = far_ref[0, 0:1, :]
    qs_ref[0:HEAD_DIM, :] = q_t
    qs_ref[HEAD_DIM + LANES:, :] = far16_ref[0]
    qw_ref[0:HEAD_DIM, :] = q_t
    qw_ref[HEAD_DIM:, :] = far16_ref[0]

    def compressed(rows):
        lc_ref[0:CPT, :] = jnp.zeros((CPT, nl), F32)
        lc_ref[CPT:CPT + rows, :] = _dot(kcb_ref[0, 0, 0:rows, :], q_t) + far
        near = pl.ds(pl.multiple_of(i * CPT, CPT), 2 * CPT)
        lc_ref[near, :] = lc_ref[near, :] + pcd_ref[0]
        lc = lc_ref[CPT:CPT + rows, :]
        cend = lax.broadcasted_iota(jnp.int32, (rows, 1), 0) * CMP_STRIDE + (CMP_BLOCK - 1)
        valid_c = cend <= qpos
        lm = jnp.where(valid_c, lc, NEG)
        mc = jnp.max(lm, axis=0, keepdims=True)
        ec = jnp.where(valid_c, jnp.exp2(lm - mc), 0.0)
        den = jnp.sum(ec, axis=0, keepdims=True)
        p_c = ec / jnp.where(den > 0, den, 1.0)
        oc_ref[...] = _dot(vcbt_ref[0, 0, :, 0:rows], p_c.astype(BF16))
        psum = p_c[:, 0:TQ]
        for r in range(1, HPG):
            psum = psum + p_c[:, r * TQ:(r + 1) * TQ]
        hi, lo = _split_hi_lo(psum)
        imp_ref[...] = _dot(covt_ref[:, 0:rows], hi) + _dot(covt_ref[:, 0:rows], lo)

    n_paths = max(ncp // CMP_PATH_BLOCKS, 1)
    tiles_per_path = CMP_PATH_BLOCKS // CPT
    for path in range(n_paths):
        rows = ncp if path == n_paths - 1 else (path + 1) * CMP_PATH_BLOCKS
        lo_tile = path * tiles_per_path
        cond = (i >= lo_tile) if path == n_paths - 1 else ((i >= lo_tile) & (i < lo_tile + tiles_per_path))
        pl.when(cond)(functools.partial(compressed, rows))
    o_c = oc_ref[...]
    imp_t = imp_ref[...]

    jrow = lax.broadcasted_iota(jnp.int32, (LANES, TQ), 0)
    qpos_t = qpos[:, 0:TQ]
    cur = qpos_t // SEL_BLOCK
    causal = (jrow * SEL_BLOCK <= qpos_t) & (jrow < n_slc)
    forced = (jrow == 0) | (jrow == cur) | (jrow == cur - 1)
    score = jnp.where(forced, FORCE_SCORE, jnp.where(causal, imp_t, NEG))
    k_sel = min(N_SEL, n_slc)
    topk = [(jnp.zeros(score.shape, F32), score), 0]

    def topk_rounds(n):
        for _ in range(min(n, k_sel - topk[1])):
            topk[0] = _top_block_round(*topk[0], jrow)
            topk[1] += 1

    key_u = lax.broadcasted_iota(jnp.int32, (TQ, 1), 0)
    causal_diag = key_u <= lane_t

    def run_pieces(specs, between=lambda: None):
        def logits(t):
            k_ref, _, qx_ref, kt, bias, mask = specs[t][:6]
            span = specs[t][6] if len(specs[t]) > 6 else 1
            s = _dot(k_ref[0, 0, pl.ds(pl.multiple_of(kt * TQ, TQ), span * TQ), :], qx_ref[...])
            if bias is not None:
                s = s + bias
            return s if mask is None else jnp.where(mask, s, NEG)

        n = len(specs)
        s = {t: logits(t) for t in range(min(PIPE_AHEAD, n))}
        between()
        pieces = []
        for t in range(n):
            m = jnp.max(s[t], axis=0, keepdims=True)
            p = jnp.exp2(s.pop(t) - m)
            if t + PIPE_AHEAD < n:
                s[t + PIPE_AHEAD] = logits(t + PIPE_AHEAD)
            between()
            vt_ref, kt = specs[t][1], specs[t][3]
            span = specs[t][6] if len(specs[t]) > 6 else 1
            pb = p.astype(BF16)
            acc = functools.reduce(jnp.add, [_dot(vt_ref[0, 0, kt + u], pb[u * TQ:(u + 1) * TQ]) for u in range(span)])
            pieces.append((acc, m, jnp.sum(p, axis=0, keepdims=True)))
            between()
        return pieces

    def merge(pieces):
        m_new = functools.reduce(jnp.maximum, [m for _, m, _ in pieces])
        scales = [jnp.exp2(m - m_new) for _, m, _ in pieces]
        acc = functools.reduce(jnp.add, [a * acc for a, (acc, _, _) in zip(scales, pieces)])
        l = functools.reduce(jnp.add, [a * l for a, (_, _, l) in zip(scales, pieces)])
        return acc, m_new, l

    prev = jnp.maximum(i - 1, 0)
    prev2 = jnp.maximum(i - 2, 0)
    tz1 = tz1_ref[0] + jnp.where(i >= 1, 0.0, NEG)
    tz0 = tz0_ref[0]
    n_gaps = 1 + 2 * 3
    acc_w, _, l_w = merge(run_pieces([(kw_ref, vwt_ref, qw_ref, prev2, None, (key_u > lane_t) & (i >= 2)),
                                      (kw_ref, vwt_ref, qw_ref, prev, tz1, None),
                                      (kw_ref, vwt_ref, qw_ref, i, tz0, causal_diag)],
                                     between=lambda: topk_rounds(-(-k_sel // n_gaps))))
    o_w = acc_w / l_w
    topk_rounds(k_sel)
    chosen = topk[0][0]
    sel_add = jnp.where(causal & (chosen > 0.5), 0.0, NEG).astype(BF16)
    qs_ref[HEAD_DIM:HEAD_DIM + LANES, :] = jnp.concatenate([sel_add] * HPG, axis=1)

    m_ref[...] = jnp.full(m_ref.shape, NEG, F32)
    l_ref[...] = jnp.zeros(l_ref.shape, F32)
    acc_ref[...] = jnp.zeros(acc_ref.shape, F32)

    def state():
        return acc_ref[...], m_ref[...], l_ref[...]

    def merge_far(kt0, n):
        span = min(n, FAR_SPAN)
        acc, m, l = merge([state()] + run_pieces([(ks_ref, vst_ref, qs_ref, kt0 + t, None, None, span)
                                                  for t in range(0, n, span)]))
        acc_ref[...] = acc
        m_ref[...] = m
        l_ref[...] = l

    n_far = jnp.maximum(i - 1, 0)

    def far_group(jg, carry):
        merge_far(FAR_GROUP * jg, FAR_GROUP)
        return carry

    lax.fori_loop(0, n_far // FAR_GROUP, far_group, 0)
    rest = n_far % FAR_GROUP
    size = FAR_GROUP // 2
    while size >= 1:
        @pl.when((rest & size) != 0)
        def _(size=size):
            merge_far(n_far - (rest & (2 * size - 1)), size)
        size //= 2

    acc_s, _, l_s = merge([state()] + run_pieces([(ks_ref, vst_ref, qs_ref, prev, tz1, None),
                                                  (ks_ref, vst_ref, qs_ref, i, tz0, causal_diag)]))
    o_s = acc_s / l_s

    g_t = gt_ref[0].T

    def gate_row(branch):
        return jnp.concatenate([g_t[branch * HPG + r:branch * HPG + r + 1, :] for r in range(HPG)], axis=1)

    o_t = gate_row(0) * o_c + gate_row(1) * o_s + gate_row(2) * o_w
    o_rd = jnp.concatenate([o_t[:, r * TQ:(r + 1) * TQ] for r in range(HPG)], axis=0)
    o_ref[0] = (o_rd.T * sa_ref[0]).astype(BF16)


def _nsa_prompt(q_t, gt, sa, kcb, vcb_t, ks, vs_t, kw, vw_t, rel_bias):
    b, _, _, _, s = q_t.shape
    assert WINDOW == 2 * TQ and s % TQ == 0 and TQ >= MAX_DISTANCE
    nq = s // TQ
    ncp = s // CMP_STRIDE
    n_cmp = ncp - 1
    n_slc = -(-s // SEL_BLOCK)
    assert n_slc <= LANES and ncp >= 2 * CPT
    nl = HPG * TQ
    table = rel_bias.astype(F32)
    uu = np.arange(TQ)[:, None]
    tt = np.arange(TQ)[None, :]

    def per_group(tab):
        rows = tab.shape[1]
        return jnp.transpose(tab.reshape(N_KV, HPG, rows, TQ), (0, 2, 1, 3)).reshape(N_KV, rows, nl)

    table2 = table * LOG2E
    heads_first = lambda dist: jnp.moveaxis(_bias_lookup(table2, dist), -1, 0)
    far_h = _bias_lookup(table2, np.array([MAX_DISTANCE]))[0]
    far_b = far_h[:, None, None]
    tz0 = per_group(heads_first(tt - uu) - far_b)
    tz1 = per_group(heads_first(TQ + tt - uu) - far_b)
    far = per_group(jnp.broadcast_to(far_b, (N_HEADS, 8, TQ)))
    far_hi = far_h.astype(BF16)
    far_lo = (far_h - far_hi.astype(F32)).astype(BF16)
    far16 = jnp.zeros((N_HEADS, AUG_PAD, TQ), BF16).at[:, 0].set(far_hi[:, None]).at[:, 1].set(far_lo[:, None])
    far16 = per_group(far16)
    e = np.arange(2 * CPT)[:, None] - CPT
    pcd = per_group(heads_first(tt - CMP_STRIDE * e - (CMP_BLOCK - 1)) - far_b)
    c = np.arange(ncp)[None, :]
    j = np.arange(LANES)[:, None]
    cov_t = ((c * CMP_STRIDE < (j + 1) * SEL_BLOCK) & (c * CMP_STRIDE + CMP_BLOCK > j * SEL_BLOCK)
             & (c < n_cmp) & (j < n_slc))
    cov_t = jnp.asarray(cov_t, BF16)

    per_g = lambda rows: pl.BlockSpec((1, rows, nl), lambda bi, g, i: (g, 0, 0))
    return pl.pallas_call(
        functools.partial(_nsa_prompt_kernel, n_slc=n_slc),
        grid=(b, N_KV, nq),
        in_specs=[
            pl.BlockSpec((1, 1, HPG, HEAD_DIM, TQ), lambda bi, g, i: (bi, g, 0, 0, i)),
            pl.BlockSpec((1, TQ, LANES), lambda bi, g, i: (bi, i, g)),
            pl.BlockSpec((1, TQ, KV_WIDTH), lambda bi, g, i: (bi, i, g)),
            pl.BlockSpec((1, 1, ncp, HEAD_DIM), lambda bi, g, i: (bi, g, 0, 0)),
            pl.BlockSpec((1, 1, HEAD_DIM, ncp), lambda bi, g, i: (bi, g, 0, 0)),
            pl.BlockSpec((1, 1, s, K_SEL_AUG), lambda bi, g, i: (bi, g, 0, 0)),
            pl.BlockSpec((1, 1, nq, HEAD_DIM, TQ), lambda bi, g, i: (bi, g, 0, 0, 0)),
            pl.BlockSpec((1, 1, s, K_WIN_AUG), lambda bi, g, i: (bi, g, 0, 0)),
            pl.BlockSpec((1, 1, nq, HEAD_DIM, TQ), lambda bi, g, i: (bi, g, 0, 0, 0)),
            per_g(2 * CPT), per_g(8), per_g(AUG_PAD), per_g(TQ), per_g(TQ),
            pl.BlockSpec(cov_t.shape, lambda bi, g, i: (0, 0)),
        ],
        out_specs=pl.BlockSpec((1, TQ, KV_WIDTH), lambda bi, g, i: (bi, i, g)),
        out_shape=jax.ShapeDtypeStruct((b, s, NSA_WIDTH), BF16),
        scratch_shapes=[pltpu.VMEM((CPT + ncp, nl), F32), pltpu.VMEM((K_SEL_AUG, nl), BF16),
                        pltpu.VMEM((K_WIN_AUG, nl), BF16),
                        pltpu.VMEM((1, nl), F32), pltpu.VMEM((1, nl), F32), pltpu.VMEM((HEAD_DIM, nl), F32),
                        pltpu.VMEM((HEAD_DIM, nl), F32), pltpu.VMEM((LANES, TQ), F32)],
        compiler_params=_cparams(("parallel", "parallel", "arbitrary")),
        name="nsa_prompt",
    )(q_t, gt, sa, kcb, vcb_t, ks, vs_t, kw, vw_t, pcd, far, far16, tz0, tz1, cov_t)


def _nsa_sample_kernel(*refs, n_pg, n_chunks, n_slc, past, t_new):
    (qbd_ref, kcbt_ref, vcb_ref, bc_ref, cov_ref, rmat_ref) = refs[1:7]
    kpg = refs[7:7 + n_pg]
    vpg = refs[7 + n_pg:7 + 2 * n_pg]
    (bsl_ref, ksn_ref, vsn_ref, bsn_ref, kwc_ref, vwc_ref, kwn_ref, vwn_ref, bw_ref, gt_ref, e4_ref,
     o_ref, sel_ref, m_ref, l_ref, acc_ref, oc_ref) = refs[7 + 2 * n_pg:]
    j = pl.program_id(1)
    qbd = qbd_ref[0]
    nsp = cov_ref.shape[1]
    n_win = kwc_ref.shape[2]

    def softmax_rows(s):
        m = jnp.max(s, axis=-1, keepdims=True)
        e = jnp.where(s > 0.5 * NEG, jnp.exp(s - m), 0.0)
        den = jnp.sum(e, axis=-1, keepdims=True)
        return e / jnp.where(den > 0, den, 1.0)

    def flash_step(s, v, v_is_transposed):
        m_old = m_ref[...]
        m_new = jnp.maximum(m_old, jnp.max(s, axis=-1, keepdims=True))
        alpha = jnp.exp(m_old - m_new)
        p = jnp.exp(s - m_new)
        l_ref[...] = alpha * l_ref[...] + jnp.sum(p, axis=-1, keepdims=True)
        pv = _dot_nt(p.astype(BF16), v) if v_is_transposed else _dot(p.astype(BF16), v)
        acc_ref[...] = alpha * acc_ref[...] + pv
        m_ref[...] = m_new

    @pl.when(j == 0)
    def _():
        p_c = softmax_rows(_dot(qbd, kcbt_ref[0]) + bc_ref[...])
        oc_ref[...] = _dot(p_c.astype(BF16), vcb_ref[0])
        hi, lo = _split_hi_lo(p_c)
        psum = _dot(rmat_ref[...], hi) + _dot(rmat_ref[...], lo)
        hi, lo = _split_hi_lo(psum)
        imp_t = (_dot(hi, cov_ref[...]) + _dot(lo, cov_ref[...])).T
        jrow = lax.broadcasted_iota(jnp.int32, (nsp, LANES), 0)
        lane = lax.broadcasted_iota(jnp.int32, (1, LANES), 1)
        qpos = past + lane % t_new
        cur = qpos // SEL_BLOCK
        causal = (jrow * SEL_BLOCK <= qpos) & (jrow < n_slc)
        forced = (jrow == 0) | (jrow == cur) | (jrow == cur - 1)
        score = jnp.where(forced, FORCE_SCORE, jnp.where(causal, imp_t, NEG))
        chosen = _select_top_blocks(score, jrow, min(N_SEL, n_slc))
        sel = jnp.where(causal & (lane < N_HEADS * t_new) & (chosen > 0.5), 0.0, -1.0).T
        for k in range(nsp // LANES):
            sel_ref[k] = sel[:, k * LANES:(k + 1) * LANES]
        m_ref[...] = jnp.full(m_ref.shape, NEG, F32)
        l_ref[...] = jnp.zeros(l_ref.shape, F32)
        acc_ref[...] = jnp.zeros(acc_ref.shape, F32)

    far = bsl_ref[:, PAGE_SIZE:2 * PAGE_SIZE]
    last = jnp.where(j == n_chunks - 1, bsl_ref[:, 0:PAGE_SIZE], far)
    chunks_per_tile = LANES // (n_pg * (PAGE_SIZE // SEL_BLOCK))
    sel_tile = sel_ref[j // chunks_per_tile].astype(BF16)
    pp = SAMPLE_PIECE_PAGES
    n_pieces = n_pg // pp

    def logits(t):
        k_t = jnp.concatenate([r[0] for r in kpg[t * pp:(t + 1) * pp]], axis=1).astype(BF16)
        bias = jnp.concatenate([far] * (pp - 1) + [last if t == n_pieces - 1 else far], axis=1)
        mask_add = _dot(sel_tile, e4_ref[j % chunks_per_tile, :, t * pp * PAGE_SIZE:(t + 1) * pp * PAGE_SIZE])
        return _dot(qbd, k_t) + bias + mask_add

    s = {t: logits(t) for t in range(min(PIPE_AHEAD, n_pieces))}
    pieces = []
    for t in range(n_pieces):
        m = jnp.max(s[t], axis=-1, keepdims=True)
        p = jnp.exp(s.pop(t) - m)
        if t + PIPE_AHEAD < n_pieces:
            s[t + PIPE_AHEAD] = logits(t + PIPE_AHEAD)
        v_t = jnp.concatenate([r[0] for r in vpg[t * pp:(t + 1) * pp]], axis=1).astype(BF16)
        pieces.append((_dot_nt(p.astype(BF16), v_t), m, jnp.sum(p, axis=-1, keepdims=True)))
    m_old = m_ref[...]
    m_new = functools.reduce(jnp.maximum, [m_old] + [m for _, m, _ in pieces])
    scales = [jnp.exp(m - m_new) for _, m, _ in pieces]
    a_old = jnp.exp(m_old - m_new)
    l_ref[...] = a_old * l_ref[...] + functools.reduce(jnp.add, [a * l for a, (_, _, l) in zip(scales, pieces)])
    acc_ref[...] = a_old * acc_ref[...] + functools.reduce(jnp.add, [a * acc for a, (acc, _, _) in zip(scales, pieces)])
    m_ref[...] = m_new

    @pl.when(j == n_chunks - 1)
    def _():
        sn = _dot_nt(qbd, ksn_ref[0].astype(BF16)) + bsn_ref[...]
        blk = n_slc - 1
        seln = sel_ref[blk // LANES][:, blk % LANES:blk % LANES + 1]
        flash_step(sn + seln * (-NEG), vsn_ref[0].astype(BF16), False)
        l = l_ref[...]
        o_s = acc_ref[...] / jnp.where(l > 0, l, 1.0)
        sw = jnp.concatenate([_dot(qbd, kwc_ref[0].astype(BF16)), _dot_nt(qbd, kwn_ref[0].astype(BF16))], axis=1)
        p_w = softmax_rows(sw + bw_ref[...]).astype(BF16)
        o_w = _dot_nt(p_w[:, :n_win], vwc_ref[0].astype(BF16)) + _dot(p_w[:, n_win:], vwn_ref[0].astype(BF16))
        gt = gt_ref[0]
        o_ref[0] = gt[:, 0:1] * oc_ref[...] + gt[:, 1:2] * o_s + gt[:, 2:3] * o_w


def _nsa_sample(q, gates, kcb_t, vcb, cache_k_slc, cache_v_slc, page_table, ks_new, vs_new,
                kw_cache, vw_cache, kw_new, vw_new, rel_bias, *, past):
    db, t_new, _ = q.shape
    n_pages = page_table.shape[1]
    assert past == n_pages * PAGE_SIZE and past % SEL_BLOCK == 0 and t_new <= SEL_BLOCK
    assert PAGE_SIZE >= MAX_DISTANCE
    n_cp = kcb_t.shape[2]
    n_slc = -(-(past + t_new) // SEL_BLOCK)
    nsp = -(-n_slc // LANES) * LANES
    n_pg = SAMPLE_PAGES
    n_chunks = n_pages // n_pg
    blocks_per_chunk = n_pg * (PAGE_SIZE // SEL_BLOCK)
    assert LANES % blocks_per_chunk == 0
    n_win = kw_cache.shape[2]
    nl = N_HEADS * t_new
    assert nl <= LANES
    row = np.arange(LANES)
    row_ok = (row < nl)[:, None]
    row_g = np.where(row < nl, row // (HPG * t_new), 0)
    row_t = (row % t_new)[:, None]
    q5 = q.reshape(db, t_new, N_KV, HPG, HEAD_DIM)
    qbd = jnp.einsum('btgrd,gk->bgrtkd', q5.astype(F32), jnp.eye(N_KV, dtype=F32)).reshape(db, nl, KV_WIDTH)
    qbd = jnp.pad(qbd, ((0, 0), (0, LANES - nl), (0, 0))).astype(BF16)

    table = rel_bias.astype(F32)

    def bias_tab(dist, valid):
        vals = jnp.moveaxis(_bias_lookup(table, dist), -1, 0)
        vals = jnp.where(jnp.asarray(valid)[None], vals, NEG).reshape(nl, -1)
        return jnp.pad(vals, ((0, LANES - nl), (0, 0)), constant_values=NEG)

    tq = np.arange(t_new)[:, None]
    qpos = past + tq
    cblk = np.arange(n_cp)[None, :]
    dist_c = qpos - (cblk * CMP_STRIDE + CMP_BLOCK - 1)
    bc = bias_tab(dist_c, (dist_c >= 0) & (cblk < n_cp - 1))
    kpos = past - PAGE_SIZE + np.arange(PAGE_SIZE)[None, :]
    all_ok = np.ones((t_new, PAGE_SIZE), bool)
    bsl = jnp.concatenate([bias_tab(qpos - kpos, all_ok),
                           bias_tab(np.full((t_new, PAGE_SIZE), MAX_DISTANCE), all_ok)], axis=1)
    u = np.arange(PAGE_SIZE)[None, :]
    new_ok = (u <= tq) & (u < t_new)
    bsn = bias_tab(tq - u, new_ok)
    wpos = past - n_win + np.arange(n_win)[None, :]
    dist_w = qpos - wpos
    bw = jnp.concatenate([bias_tab(dist_w, (dist_w >= 0) & (dist_w < WINDOW) & (wpos >= 0)),
                          bias_tab(tq - u, new_ok & (tq - u < WINDOW))], axis=1)
    c = np.arange(n_cp)[:, None]
    jb = np.arange(nsp)[None, :]
    cov = ((c * CMP_STRIDE < (jb + 1) * SEL_BLOCK) & (c * CMP_STRIDE + CMP_BLOCK > jb * SEL_BLOCK)
           & (c < n_cp - 1) & (jb < n_slc))
    cov = jnp.asarray(cov, BF16)
    same = (row_g[:, None] == row_g[None, :]) & (row_t == row_t.T) & row_ok & row_ok.T
    rmat = jnp.asarray(same, BF16)
    kk = np.arange(n_pg * PAGE_SIZE)[None, None, :] // SEL_BLOCK
    e4 = np.arange(LANES)[None, :, None] == (np.arange(LANES // blocks_per_chunk)[:, None, None] * blocks_per_chunk + kk)
    e4 = jnp.asarray(np.where(e4, -NEG, 0.0), BF16)
    g5 = gates.reshape(db, t_new, N_KV, LANES)[..., :3 * HPG].reshape(db, t_new, N_KV, 3, HPG)
    gcol = jnp.transpose(g5, (0, 2, 4, 1, 3)).reshape(db, nl, 3)
    gcol = jnp.pad(gcol, ((0, 0), (0, LANES - nl), (0, 5)))

    def pad_new(a):
        return jnp.pad(a, ((0, 0), (0, PAGE_SIZE - t_new), (0, 0)))

    ksn, vsn, kwn, vwn = (pad_new(a) for a in (ks_new, vs_new, kw_new, vw_new))

    per_b = lambda shape: pl.BlockSpec((1,) + shape, lambda b, jc, pt: (b, 0, 0))
    full = lambda a: pl.BlockSpec(a.shape, lambda b, jc, pt: (0,) * a.ndim)
    page = lambda p: pl.BlockSpec((1, KV_WIDTH, PAGE_SIZE), lambda b, jc, pt: (pt[b, jc * SAMPLE_PAGES + p], 0, 0))
    in_specs = ([per_b((LANES, KV_WIDTH)), per_b((KV_WIDTH, n_cp)), per_b((n_cp, KV_WIDTH)),
                 full(bc), full(cov), full(rmat)]
                + [page(p) for p in range(n_pg)] * 2
                + [full(bsl), per_b((PAGE_SIZE, KV_WIDTH)), per_b((PAGE_SIZE, KV_WIDTH)), full(bsn),
                   per_b((KV_WIDTH, n_win)), per_b((KV_WIDTH, n_win)),
                   per_b((PAGE_SIZE, KV_WIDTH)), per_b((PAGE_SIZE, KV_WIDTH)), full(bw), per_b((LANES, 8)), full(e4)])
    o = pl.pallas_call(
        functools.partial(_nsa_sample_kernel, n_pg=n_pg, n_chunks=n_chunks, n_slc=n_slc, past=past, t_new=t_new),
        grid_spec=pltpu.PrefetchScalarGridSpec(
            num_scalar_prefetch=1, grid=(db, n_chunks), in_specs=in_specs,
            out_specs=pl.BlockSpec((1, LANES, KV_WIDTH), lambda b, jc, pt: (b, 0, 0)),
            scratch_shapes=[pltpu.VMEM((nsp // LANES, LANES, LANES), F32), pltpu.VMEM((LANES, 1), F32),
                            pltpu.VMEM((LANES, 1), F32), pltpu.VMEM((LANES, KV_WIDTH), F32),
                            pltpu.VMEM((LANES, KV_WIDTH), F32)]),
        out_shape=jax.ShapeDtypeStruct((db, LANES, KV_WIDTH), F32),
        compiler_params=_cparams(("arbitrary", "arbitrary")),
        name="nsa_sample",
    )(page_table, qbd, kcb_t, vcb, bc, cov, rmat, *([cache_k_slc] * n_pg), *([cache_v_slc] * n_pg),
      bsl, ksn, vsn, bsn, kw_cache, vw_cache, kwn, vwn, bw, gcol, e4)
    o6 = o[:, :nl].reshape(db, N_KV, HPG, t_new, N_KV, HEAD_DIM)
    o_diag = jnp.stack([o6[:, g, :, :, g] for g in range(N_KV)], axis=1)
    return jnp.transpose(o_diag, (0, 3, 1, 2, 4)).reshape(db, t_new, NSA_WIDTH)


def _conv_tail(y, cb_ref, lg_ref, lb_ref, wpw_ref, bpw_ref, sb):
    y = y + cb_ref[...]
    mu = jnp.mean(y, axis=-1, keepdims=True)
    yc = y - mu
    var = jnp.mean(yc * yc, axis=-1, keepdims=True)
    yn = yc * lax.rsqrt(var + LN_EPS) * lg_ref[...] + lb_ref[...]
    act = yn * _sigmoid(yn)
    return ((_dot(act.astype(BF16), wpw_ref[...]) + bpw_ref[...]) * sb).astype(BF16)


def _conv_prompt_kernel(c_ref, halo_ref, init_ref, cw_ref, cb_ref, lg_ref, lb_ref, wpw_ref, bpw_ref, sb_ref,
                        o_ref, full_ref, sh_ref, y_ref, *, ts):
    j = pl.program_id(1)
    full_ref[CONV_HALO:CONV_HALO + ts, :] = c_ref[0]

    @pl.when(j == 0)
    def _():
        full_ref[0:CONV_HALO, :] = init_ref[0]

    @pl.when(j > 0)
    def _():
        full_ref[0:CONV_HALO, :] = halo_ref[0]

    first = CONV_HALO - (CONV_WIDTH - 1)
    span = sh_ref.shape[1]
    for sft in range(1, 8):
        sh_ref[sft - 1] = full_ref[sft:sft + span, :]
    rb = 64
    ch = full_ref.shape[1]
    for c0 in range(0, ch, LANES):
        for r0 in range(0, ts, rb):
            acc = jnp.zeros((rb, LANES), F32)
            for w in range(CONV_WIDTH):
                sft = (first + w) % 8
                base = r0 + first + w - sft
                if sft == 0:
                    x = full_ref[base:base + rb, c0:c0 + LANES]
                else:
                    x = sh_ref[sft - 1, base:base + rb, c0:c0 + LANES]
                acc = acc + x * cw_ref[w:w + 1, c0:c0 + LANES]
            y_ref[r0:r0 + rb, c0:c0 + LANES] = acc
    o_ref[0] = _conv_tail(y_ref[...], cb_ref, lg_ref, lb_ref, wpw_ref, bpw_ref, sb_ref[0])


def _conv_sample_kernel(c_ref, st_ref, cw_ref, cb_ref, lg_ref, lb_ref, wpw_ref, bpw_ref, sb_ref,
                        o_ref, full_ref, y_ref):
    nb, t_new, _ = c_ref.shape
    n_st = st_ref.shape[1]
    full_ref[:, 0:n_st, :] = st_ref[...]
    full_ref[:, n_st:n_st + t_new, :] = c_ref[...]
    first = n_st - (CONV_WIDTH - 1)
    for b in range(nb):
        acc = jnp.zeros((t_new, full_ref.shape[2]), F32)
        for w in range(CONV_WIDTH):
            acc = acc + full_ref[b, first + w:first + w + t_new, :] * cw_ref[w:w + 1, :]
        y_ref[b * t_new:(b + 1) * t_new, :] = acc
    o_ref[...] = _conv_tail(y_ref[...], cb_ref, lg_ref, lb_ref, wpw_ref, bpw_ref, sb_ref[...])


def _conv_params(conv_w, conv_b, ln_g, ln_b, w_pw, b_pw):
    ch = conv_w.shape[1]
    cw = jnp.pad(conv_w, ((0, 32 - CONV_WIDTH), (0, 0)))
    return (cw, conv_b.reshape(1, ch), ln_g.reshape(1, ch), ln_b.reshape(1, ch), w_pw.astype(BF16),
            b_pw.reshape(1, ch))


def _conv_prompt(c_in, init, params, sb, *, ts):
    b, s, ch = c_in.shape
    hb = ts // CONV_HALO
    const = lambda a: pl.BlockSpec(a.shape, lambda bi, j: (0,) * a.ndim)
    return pl.pallas_call(
        functools.partial(_conv_prompt_kernel, ts=ts),
        grid=(b, s // ts),
        in_specs=[pl.BlockSpec((1, ts, ch), lambda bi, j: (bi, j, 0)),
                  pl.BlockSpec((1, CONV_HALO, ch), lambda bi, j: (bi, jnp.maximum(j * hb - 1, 0), 0)),
                  pl.BlockSpec((1, CONV_HALO, ch), lambda bi, j: (bi, 0, 0))]
        + [const(a) for a in params]
        + [pl.BlockSpec((1, ts, ch), lambda bi, j: (bi, j, 0))],
        out_specs=pl.BlockSpec((1, ts, ch), lambda bi, j: (bi, j, 0)),
        out_shape=jax.ShapeDtypeStruct((b, s, ch), BF16),
        scratch_shapes=[pltpu.VMEM((CONV_HALO + ts, ch), F32), pltpu.VMEM((7, CONV_HALO + ts - 8, ch), F32),
                        pltpu.VMEM((ts, ch), F32)],
        compiler_params=_cparams(("parallel", "arbitrary")),
        name="conv_prompt",
    )(c_in, c_in, init, *params, sb)


def _conv_sample(c_in, state, params, sb):
    db, t_new, ch = c_in.shape
    n_st = state.shape[1]
    rows_pad = -(-(n_st + t_new) // 8) * 8
    return pl.pallas_call(
        _conv_sample_kernel,
        out_shape=jax.ShapeDtypeStruct((db * t_new, ch), BF16),
        scratch_shapes=[pltpu.VMEM((db, rows_pad, ch), F32), pltpu.VMEM((db * t_new, ch), F32)],
        compiler_params=pltpu.CompilerParams(vmem_limit_bytes=VMEM_LIMIT),
        name="conv_sample",
    )(c_in, state, *params, sb)


def _out_kernel(x_ref, ma_ref, mb_ref, *rest, gated):
    if gated:
        sa_ref, wa_ref, wb_ref, gp_ref, y_ref = rest
        ma = (ma_ref[0] * sa_ref[0]).astype(BF16)
    else:
        wa_ref, wb_ref, gp_ref, y_ref = rest
        ma = ma_ref[0]
    z = _dot(ma, wa_ref[...]) + _dot(mb_ref[0], wb_ref[...])
    ms = jnp.mean(z * z, axis=-1, keepdims=True)
    y_ref[0] = x_ref[0] + z * lax.rsqrt(ms + RMS_EPS) * gp_ref[...]


def _out_proj(x, ma, mb, sa, w_out, g_post, *, tm):
    b, s, d = x.shape
    na = ma.shape[-1]
    wa = w_out[:na].astype(BF16)
    wb = w_out[na:].astype(BF16)
    row = lambda width: pl.BlockSpec((1, tm, width), lambda bi, i: (bi, i, 0))
    const = lambda a: pl.BlockSpec(a.shape, lambda bi, i: (0, 0))
    gp = g_post.reshape(1, d)
    gate_in, gate_spec = ([sa], [row(na)]) if sa is not None else ([], [])
    return pl.pallas_call(
        functools.partial(_out_kernel, gated=sa is not None),
        grid=(b, s // tm),
        in_specs=[row(d), row(na), row(mb.shape[-1])] + gate_spec + [const(wa), const(wb), const(gp)],
        out_specs=row(d),
        out_shape=jax.ShapeDtypeStruct((b, s, d), F32),
        compiler_params=_cparams(("parallel", "parallel")),
        name="out_proj",
    )(x, ma, mb, *gate_in, wa, wb, gp)


def _split_w_in(w_in):
    d = w_in.shape[0]
    c0 = NSA_WIDTH + 6 * KV_WIDTH
    n_gate = 3 * N_HEADS
    conv_ch = (w_in.shape[1] - c0 - n_gate - NSA_WIDTH) // 3
    w_qkv = w_in[:, :c0].astype(BF16)
    wg = w_in[:, c0:c0 + n_gate].reshape(d, N_KV, HPG, 3)
    wg = jnp.transpose(wg, (0, 1, 3, 2)).reshape(d, N_KV, 3 * HPG)
    wg = jnp.pad(wg, ((0, 0), (0, 0), (0, LANES - 3 * HPG))).reshape(d, N_KV * LANES)
    z_a = w_in[:, c0 + n_gate:c0 + n_gate + NSA_WIDTH]
    glu0 = c0 + n_gate + NSA_WIDTH
    w_glu = w_in[:, glu0:glu0 + 2 * conv_ch].astype(BF16)
    z_b = w_in[:, glu0 + 2 * conv_ch:]
    assert conv_ch == NSA_WIDTH
    w_gate = jnp.concatenate([z_a, z_b, wg], axis=1).astype(BF16)
    return w_qkv, w_gate, w_glu


def _token_minor(cache):
    n, tokens = cache.shape[:2]
    return jnp.transpose(cache, (0, 2, 3, 1)).reshape(n, KV_WIDTH, tokens)


def kernel(x_prompt, x_sample, cache_k_cmp, cache_v_cmp, cache_k_slc, cache_v_slc, cache_k_win, cache_v_win,
           state_conv, page_table, g_pre, w_in, cmp_w1_k, cmp_w2_k, cmp_pe_k, cmp_w1_v, cmp_w2_v, cmp_pe_v,
           rel_bias, conv_w, conv_b, ln_g, ln_b, w_pw, b_pw, w_out, g_post):
    depth = g_pre.shape[0]
    assert depth == 1
    layer = 0
    b, s, d = x_prompt.shape
    db, t_new, _ = x_sample.shape
    past = page_table.shape[1] * PAGE_SIZE
    conv_ch = conv_w.shape[-1]

    w_qkv, w_gate, w_glu = _split_w_in(w_in[layer])
    wts_k = _cmp_weights(cmp_w1_k[layer], cmp_w2_k[layer], cmp_pe_k[layer])
    wts_v = _cmp_weights(cmp_w1_v[layer], cmp_w2_v[layer], cmp_pe_v[layer])
    cparams = _conv_params(conv_w[layer], conv_b[layer], ln_g[layer], ln_b[layer], w_pw[layer], b_pw[layer])
    chunk_w = CMP_STRIDE * KV_WIDTH

    (q_t, kc, vc, ks, vs, kw, vw, ks_g, kw_g, vs_t, vw_t), sa, sb, gt, c_in = _projections(
        x_prompt, g_pre[layer], w_qkv, w_gate, w_glu, tm=512, attn_layouts=True)
    n_ch = s // CMP_STRIDE
    kcb = _compress(kc[:, :n_ch * CMP_STRIDE].reshape(b, n_ch, chunk_w), wts_k, layout="group_rows")
    vcb_t = _compress(vc[:, :n_ch * CMP_STRIDE].reshape(b, n_ch, chunk_w), wts_v, layout="group_cols")
    ma = _nsa_prompt(q_t, gt, sa, kcb, vcb_t, ks_g, vs_t, kw_g, vw_t, rel_bias)
    mb = _conv_prompt(c_in, jnp.zeros((b, CONV_HALO, conv_ch), F32), cparams, sb, ts=256)
    y_prompt = _out_proj(x_prompt, ma, mb, None, w_out[layer], g_post[layer], tm=512)
    n_keep = min(WINDOW, s)
    kv5 = lambda a: a.reshape(1, a.shape[0], a.shape[1], N_KV, HEAD_DIM)
    outs_p = (kv5(kc), kv5(vc), kv5(ks), kv5(vs), kv5(kw[:, -n_keep:]), kv5(vw[:, -n_keep:]),
              c_in[None, :, -(CONV_WIDTH - 1):])

    xs = x_sample.reshape(1, db * t_new, d)
    (q_s, kc_s, vc_s, ks_s, vs_s, kw_s, vw_s), sa_s, sb_s, gt_s, c_s = _projections(
        xs, g_pre[layer], w_qkv, w_gate, w_glu, tm=db * t_new, attn_layouts=False)
    tok = lambda a: a.reshape(db, t_new, a.shape[-1])
    kcb_s = _compress(_token_minor(cache_k_cmp[layer]), wts_k, page_table=page_table, layout="cols")
    vcb_s = _compress(_token_minor(cache_v_cmp[layer]), wts_v, page_table=page_table, layout="rows")
    o_a = _nsa_sample(tok(q_s), tok(gt_s), kcb_s, vcb_s,
                      _token_minor(cache_k_slc[layer]), _token_minor(cache_v_slc[layer]), page_table,
                      tok(ks_s), tok(vs_s), _token_minor(cache_k_win[layer]), _token_minor(cache_v_win[layer]),
                      tok(kw_s), tok(vw_s), rel_bias, past=past)
    mb_s = _conv_sample(tok(c_s), state_conv[layer], cparams, sb_s[0])
    y_sample = _out_proj(xs, o_a.reshape(1, db * t_new, NSA_WIDTH), mb_s[None], sa_s, w_out[layer],
                         g_post[layer], tm=db * t_new).reshape(db, t_new, d)
    n_keep_s = min(WINDOW, past + t_new)
    kv5s = lambda a: a.reshape(1, db, t_new, N_KV, HEAD_DIM)
    win = lambda cache, new: jnp.concatenate(
        [cache[layer], new.reshape(db, t_new, N_KV, HEAD_DIM)], axis=1)[None, :, -n_keep_s:]
    conv_s = jnp.concatenate([state_conv[layer], tok(c_s)], axis=1)[None, :, -(CONV_WIDTH - 1):]
    outs_s = (kv5s(kc_s), kv5s(vc_s), kv5s(ks_s), kv5s(vs_s), win(cache_k_win, kw_s), win(cache_v_win, vw_s), conv_s)
    return (y_prompt, y_sample) + outs_p + outs_s
```

```python
import functools
import math

import numpy as np
import jax
import jax.numpy as jnp
from jax import lax
from jax.experimental import pallas as pl
from jax.experimental.pallas import tpu as pltpu

F32 = jnp.float32
BF16 = jnp.bfloat16

HEAD_DIM = 64
N_KV = 4
HPG = 4
N_HEADS = N_KV * HPG
KV_WIDTH = N_KV * HEAD_DIM
NSA_WIDTH = N_HEADS * HEAD_DIM
CMP_BLOCK = 32
CMP_STRIDE = 16
SEL_BLOCK = 64
N_SEL = 16
WINDOW = 512
CONV_WIDTH = 31
N_BUCKETS = 32
MAX_DISTANCE = 128
FORCE_SCORE = 1e6
RMS_EPS = 1e-6
LN_EPS = 1e-5
PAGE_SIZE = 128
SCALE = HEAD_DIM ** -0.5

NEG = -1e30
PICKED = -3e38
LANES = 128
VMEM_LIMIT = 56 * 1024 * 1024

LOG2E = 1.4426950408889634
TQ = 256
FAR_GROUP = 4
PIPE_AHEAD = 2
AUG_PAD = 16
K_WIN_AUG = HEAD_DIM + AUG_PAD
K_SEL_AUG = HEAD_DIM + LANES + AUG_PAD
CPT = TQ // CMP_STRIDE
CONV_HALO = 32
SAMPLE_PAGES = 32
SAMPLE_PIECE_PAGES = 8
CMP_PAGES = 32


def _cparams(sem):
    return pltpu.CompilerParams(dimension_semantics=sem, vmem_limit_bytes=VMEM_LIMIT)


def _dot(a, b):
    return jnp.dot(a, b, preferred_element_type=F32)


def _dot_nt(a, b):
    return lax.dot_general(a, b, (((1,), (1,)), ((), ())), preferred_element_type=F32)


def _sigmoid(x):
    return 1.0 / (1.0 + jnp.exp(-x))


def _split_hi_lo(x):
    hi = x.astype(BF16)
    lo = (x - hi.astype(F32)).astype(BF16)
    return hi, lo


def _normed(x_ref, g_ref):
    x = x_ref[0]
    ms = jnp.mean(x * x, axis=-1, keepdims=True)
    return (x * lax.rsqrt(ms + RMS_EPS) * g_ref[...]).astype(BF16)


def _proj_qkv_kernel(x_ref, g_ref, w_ref, q_ref, kc_ref, vc_ref, ks_ref, vs_ref, kw_ref, vw_ref,
                     *attn_refs, attn_layouts):
    h = _normed(x_ref, g_ref)
    tm = h.shape[0]
    if attn_layouts:
        tok = pl.program_id(1) * tm + lax.broadcasted_iota(jnp.int32, (tm, LANES), 0)
        lane = lax.broadcasted_iota(jnp.int32, (tm, LANES), 1)
        blk_cols = jnp.where(lane == tok // SEL_BLOCK, 1.0, 0.0).astype(BF16)
        one_cols = jnp.where(lax.broadcasted_iota(jnp.int32, (tm, AUG_PAD), 1) < 2, 1.0, 0.0).astype(BF16)
    for g in range(N_KV):
        res = _dot(h, w_ref[:, g * KV_WIDTH:(g + 1) * KV_WIDTH]) * (SCALE * LOG2E if attn_layouts else SCALE)
        if attn_layouts:
            res_t = res.T
            for r in range(HPG):
                q_ref[0, g, r] = res_t[r * HEAD_DIM:(r + 1) * HEAD_DIM, :].astype(BF16)
        else:
            q_ref[0, :, g * KV_WIDTH:(g + 1) * KV_WIDTH] = res.astype(BF16)
    for j, o_ref in enumerate((kc_ref, vc_ref, ks_ref, vs_ref, kw_ref, vw_ref)):
        c0 = NSA_WIDTH + j * KV_WIDTH
        res = _dot(h, w_ref[:, c0:c0 + KV_WIDTH])
        o_ref[0] = res
        if attn_layouts and j in (2, 4):
            extra = [blk_cols, one_cols] if j == 2 else [one_cols]
            for g in range(N_KV):
                attn_refs[j // 2 - 1][0, g] = jnp.concatenate(
                    [res[:, g * HEAD_DIM:(g + 1) * HEAD_DIM].astype(BF16)] + extra, axis=1)
        if attn_layouts and j in (3, 5):
            res_t = res.T.astype(BF16)
            for g in range(N_KV):
                for kt in range(tm // TQ):
                    attn_refs[2 + j // 2 - 1][0, g, kt] = res_t[g * HEAD_DIM:(g + 1) * HEAD_DIM, kt * TQ:(kt + 1) * TQ]


def _proj_gate_kernel(x_ref, g_ref, w_ref, sa_ref, sb_ref, gt_ref):
    h = _normed(x_ref, g_ref)
    for o_ref, base in ((sa_ref, 0), (sb_ref, NSA_WIDTH)):
        for c in range(NSA_WIDTH // 256):
            z = _dot(h, w_ref[:, base + c * 256: base + (c + 1) * 256])
            o_ref[0, :, c * 256:(c + 1) * 256] = z * _sigmoid(z)
    for c in range(2):
        z = _dot(h, w_ref[:, 2 * NSA_WIDTH + c * 256: 2 * NSA_WIDTH + (c + 1) * 256])
        gt_ref[0, :, c * 256:(c + 1) * 256] = _sigmoid(z)


def _proj_glu_kernel(x_ref, g_ref, w_ref, c_ref):
    h = _normed(x_ref, g_ref)
    n = c_ref.shape[-1]
    for c in range(n // 256):
        a = _dot(h, w_ref[:, c * 256:(c + 1) * 256])
        gg = _dot(h, w_ref[:, n + c * 256: n + (c + 1) * 256])
        c_ref[0, :, c * 256:(c + 1) * 256] = a * _sigmoid(gg)


def _projections(x, g_pre, w_qkv, w_gate, w_glu, *, tm, attn_layouts):
    b, s, d = x.shape
    grid = (b, s // tm)
    x_spec = pl.BlockSpec((1, tm, d), lambda bi, i: (bi, i, 0))
    g_spec = pl.BlockSpec((1, d), lambda bi, i: (0, 0))

    def w_spec(w):
        return pl.BlockSpec(w.shape, lambda bi, i: (0, 0))

    def row_spec(width):
        return pl.BlockSpec((1, tm, width), lambda bi, i: (bi, i, 0))

    kv_shape = jax.ShapeDtypeStruct((b, s, KV_WIDTH), F32)
    if attn_layouts:
        assert tm % TQ == 0
        q_shape = jax.ShapeDtypeStruct((b, N_KV, HPG, HEAD_DIM, s), BF16)
        q_spec = pl.BlockSpec((1, N_KV, HPG, HEAD_DIM, tm), lambda bi, i: (bi, 0, 0, 0, i))
        k_shape = lambda w: jax.ShapeDtypeStruct((b, N_KV, s, w), BF16)
        k_spec = lambda w: pl.BlockSpec((1, N_KV, tm, w), lambda bi, i: (bi, 0, i, 0))
        v_shape = jax.ShapeDtypeStruct((b, N_KV, s // TQ, HEAD_DIM, TQ), BF16)
        v_spec = pl.BlockSpec((1, N_KV, tm // TQ, HEAD_DIM, TQ), lambda bi, i: (bi, 0, i, 0, 0))
        extra_shape = [k_shape(K_SEL_AUG), k_shape(K_WIN_AUG), v_shape, v_shape]
        extra_spec = [k_spec(K_SEL_AUG), k_spec(K_WIN_AUG), v_spec, v_spec]
    else:
        q_shape = jax.ShapeDtypeStruct((b, s, NSA_WIDTH), BF16)
        q_spec = row_spec(NSA_WIDTH)
        extra_shape, extra_spec = [], []
    g2 = g_pre.reshape(1, d)
    qkv = pl.pallas_call(
        functools.partial(_proj_qkv_kernel, attn_layouts=attn_layouts),
        grid=grid,
        in_specs=[x_spec, g_spec, w_spec(w_qkv)],
        out_specs=[q_spec] + [row_spec(KV_WIDTH)] * 6 + extra_spec,
        out_shape=[q_shape] + [kv_shape] * 6 + extra_shape,
        compiler_params=_cparams(("parallel", "parallel")),
        name="proj_qkv",
    )(x, g2, w_qkv)
    sa, sb, gt = pl.pallas_call(
        _proj_gate_kernel,
        grid=grid,
        in_specs=[x_spec, g_spec, w_spec(w_gate)],
        out_specs=[row_spec(NSA_WIDTH), row_spec(NSA_WIDTH), row_spec(N_KV * LANES)],
        out_shape=[jax.ShapeDtypeStruct((b, s, NSA_WIDTH), F32)] * 2
        + [jax.ShapeDtypeStruct((b, s, N_KV * LANES), F32)],
        compiler_params=_cparams(("parallel", "parallel")),
        name="proj_gate",
    )(x, g2, w_gate)
    c_in = pl.pallas_call(
        _proj_glu_kernel,
        grid=grid,
        in_specs=[x_spec, g_spec, w_spec(w_glu)],
        out_specs=row_spec(w_glu.shape[1] // 2),
        out_shape=jax.ShapeDtypeStruct((b, s, w_glu.shape[1] // 2), F32),
        compiler_params=_cparams(("parallel", "parallel")),
        name="proj_glu",
    )(x, g2, w_glu)
    return qkv, sa, sb, gt, c_in


def _cmp1_rows_kernel(x_ref, pe_ref, w_ref, pre_ref, pepre_ref):
    w = w_ref[...].reshape(-1, w_ref.shape[-1])
    pre_ref[0] = _dot(x_ref[0].astype(BF16), w)
    pepre_ref[...] = _dot(pe_ref[...], w)


def _cmp1_paged_kernel(pt_ref, cache_ref, perm_ref, pe_ref, w_ref, pre_ref, pepre_ref, buf_ref, sem_ref,
                       *, n_in, n_steps):
    step = pl.program_id(0) * n_steps + pl.program_id(1)
    last = pl.num_programs(0) * n_steps - 1
    slot = step % 2

    def fetch(s, p, into):
        page = pt_ref[s // n_steps, (s % n_steps) * n_in + p]
        return pltpu.make_async_copy(cache_ref.at[page], buf_ref.at[into, p], sem_ref.at[into])

    def wait_slot(into):
        for p in range(n_in):
            pltpu.make_async_copy(cache_ref.at[0], buf_ref.at[into, p], sem_ref.at[into]).wait()

    @pl.when(step == 0)
    def _():
        for p in range(n_in):
            fetch(0, p, 0).start()

    wait_slot(slot)
    perm = perm_ref[...]
    nxt = jnp.minimum(step + 1, last)
    zs = []
    for p in range(n_in):
        zs.append(_dot_nt(perm, buf_ref[slot, p].astype(BF16)))
        fetch(nxt, p, 1 - slot).start()
    rpp = PAGE_SIZE // CMP_STRIDE
    acc = jnp.zeros((n_in * rpp, w_ref.shape[-1]), F32)
    for c in range(CMP_STRIDE):
        xc = jnp.concatenate([z[c * rpp:(c + 1) * rpp] for z in zs], axis=0)
        acc = acc + _dot(xc.astype(BF16), w_ref[c])
    pre_ref[0] = acc
    pepre_ref[...] = _dot(pe_ref[...], w_ref[...].reshape(-1, w_ref.shape[-1]))

    @pl.when(step == last)
    def _():
        wait_slot(1 - slot)


def _cmp2_kernel(pre_ref, pepre_ref, w2_ref, o_ref, *, layout):
    pre = pre_ref[0]
    n_ch = pre.shape[0]
    a = pre[:, :KV_WIDTH]
    b_next = pltpu.roll(pre[:, KV_WIDTH:], n_ch - 1, axis=0)
    pe_bias = pepre_ref[0:1, :KV_WIDTH] + pepre_ref[1:2, KV_WIDTH:]
    z = a + b_next + pe_bias
    hid = 0.5 * z * (1.0 + jnp.tanh(math.sqrt(2.0 / math.pi) * (z + 0.044715 * (z * z * z))))
    out = _dot(hid.astype(BF16), w2_ref[...])
    if layout == "group_rows":
        for g in range(N_KV):
            o_ref[0, g] = out[:, g * HEAD_DIM:(g + 1) * HEAD_DIM].astype(BF16)
    elif layout == "group_cols":
        out_t = out.T.astype(BF16)
        for g in range(N_KV):
            o_ref[0, g] = out_t[g * HEAD_DIM:(g + 1) * HEAD_DIM, :]
    elif layout == "rows":
        o_ref[0] = out.astype(BF16)
    else:
        o_ref[0] = out.T.astype(BF16)


def _cmp_weights(w1, w2, pe):
    eye = jnp.eye(N_KV, dtype=F32)
    halves = w1.reshape(2, CMP_STRIDE, HEAD_DIM, -1)
    hdim = halves.shape[-1]
    wbig = jnp.einsum('acdh,gk->cgdakh', halves, eye)
    wbig = wbig.reshape(CMP_STRIDE, KV_WIDTH, 2 * N_KV * hdim).astype(BF16)
    w2big = jnp.einsum('hd,gk->ghkd', w2, eye).reshape(N_KV * hdim, KV_WIDTH).astype(BF16)
    pe_rows = jnp.broadcast_to(pe.reshape(2, CMP_STRIDE, 1, HEAD_DIM), (2, CMP_STRIDE, N_KV, HEAD_DIM))
    pe8 = jnp.zeros((8, CMP_STRIDE * KV_WIDTH), F32).at[:2].set(pe_rows.reshape(2, -1)).astype(BF16)
    return wbig, w2big, pe8


def _compress(x_view, wts, *, page_table=None, layout):
    wbig, w2big, pe8 = wts
    ncol = wbig.shape[-1]
    kdim = wbig.shape[0] * wbig.shape[1]
    if page_table is None:
        b, n_ch, _ = x_view.shape
        rows = min(n_ch, 256)
        const2 = lambda bi, i: (0, 0)
        pre, pepre = pl.pallas_call(
            _cmp1_rows_kernel, grid=(b, n_ch // rows),
            in_specs=[pl.BlockSpec((1, rows, kdim), lambda bi, i: (bi, i, 0)), pl.BlockSpec(pe8.shape, const2),
                      pl.BlockSpec(wbig.shape, lambda bi, i: (0, 0, 0))],
            out_specs=[pl.BlockSpec((1, rows, ncol), lambda bi, i: (bi, i, 0)), pl.BlockSpec((8, ncol), const2)],
            out_shape=[jax.ShapeDtypeStruct((b, n_ch, ncol), F32), jax.ShapeDtypeStruct((8, ncol), F32)],
            compiler_params=_cparams(("arbitrary", "arbitrary")), name="cmp_stage1")(x_view, pe8, wbig)
    else:
        b, n_pages = page_table.shape
        n_in = CMP_PAGES
        rpp = PAGE_SIZE // CMP_STRIDE
        rows = n_in * rpp
        n_ch = n_pages * rpp
        const2 = lambda bi, i, pt: (0, 0)
        n_steps = n_pages // n_in
        tok = np.arange(PAGE_SIZE)
        perm = (tok[None, :] == (tok[:, None] % rpp) * CMP_STRIDE + tok[:, None] // rpp)
        perm = jnp.asarray(perm, BF16)
        pre, pepre = pl.pallas_call(
            functools.partial(_cmp1_paged_kernel, n_in=n_in, n_steps=n_steps),
            grid_spec=pltpu.PrefetchScalarGridSpec(
                num_scalar_prefetch=1, grid=(b, n_steps),
                in_specs=[pl.BlockSpec(memory_space=pl.ANY), pl.BlockSpec(perm.shape, const2),
                          pl.BlockSpec(pe8.shape, const2), pl.BlockSpec(wbig.shape, lambda bi, i, pt: (0, 0, 0))],
                out_specs=[pl.BlockSpec((1, rows, ncol), lambda bi, i, pt: (bi, i, 0)),
                           pl.BlockSpec((8, ncol), const2)],
                scratch_shapes=[pltpu.VMEM((2, n_in, KV_WIDTH, PAGE_SIZE), F32), pltpu.SemaphoreType.DMA((2,))]),
            out_shape=[jax.ShapeDtypeStruct((b, n_ch, ncol), F32), jax.ShapeDtypeStruct((8, ncol), F32)],
            compiler_params=_cparams(("arbitrary", "arbitrary")), name="cmp_stage1_paged",
        )(page_table, x_view, perm, pe8, wbig)
    o_dims = {"group_rows": (N_KV, n_ch, HEAD_DIM), "group_cols": (N_KV, HEAD_DIM, n_ch),
              "rows": (n_ch, KV_WIDTH), "cols": (KV_WIDTH, n_ch)}[layout]
    return pl.pallas_call(
        functools.partial(_cmp2_kernel, layout=layout),
        grid=(b,),
        in_specs=[pl.BlockSpec((1, n_ch, ncol), lambda bi: (bi, 0, 0)),
                  pl.BlockSpec((8, ncol), lambda bi: (0, 0)),
                  pl.BlockSpec(w2big.shape, lambda bi: (0, 0))],
        out_specs=pl.BlockSpec((1,) + o_dims, lambda bi: (bi,) + (0,) * len(o_dims)),
        out_shape=jax.ShapeDtypeStruct((b,) + o_dims, BF16),
        compiler_params=_cparams(("parallel",)), name="cmp_stage2",
    )(pre, pepre, w2big)


def _t5_bucket(dist):
    dist = np.maximum(np.asarray(dist, np.int64), 0)
    max_exact = N_BUCKETS // 2
    d32 = np.maximum(dist, 1).astype(np.float32)
    large = max_exact + (np.log(d32 / np.float32(max_exact)) / np.float32(math.log(MAX_DISTANCE / max_exact))
                         * np.float32(N_BUCKETS - max_exact)).astype(np.int32)
    large = np.minimum(large, N_BUCKETS - 1)
    return np.where(dist < max_exact, dist, large).astype(np.int32)


def _bias_lookup(table, dist):
    bucket = jnp.asarray(_t5_bucket(dist).reshape(-1, 1))
    onehot = (bucket == jnp.arange(N_BUCKETS, dtype=jnp.int32)[None, :]).astype(F32)
    out = jnp.dot(onehot, table, precision=lax.Precision.HIGHEST)
    return out.reshape(tuple(np.shape(dist)) + (table.shape[1],))


def _top_block_round(chosen, work, jrow):
    m = jnp.max(work, axis=0, keepdims=True)
    idx = jnp.min(jnp.where(work == m, jrow, 1 << 20), axis=0, keepdims=True)
    pick = jrow == idx
    return jnp.where(pick, 1.0, chosen), jnp.where(pick, PICKED, work)


def _select_top_blocks(score, jrow, k_sel):
    state = (jnp.zeros(score.shape, F32), score)
    for _ in range(k_sel):
        state = _top_block_round(*state, jrow)
    return state[0]


def _nsa_prompt_kernel(q_ref, gt_ref, sa_ref, kcb_ref, vcbt_ref, ks_ref, vst_ref, kw_ref, vwt_ref,
                       pcd_ref, far_ref, far16_ref, tz0_ref, tz1_ref, covt_ref, o_ref,
                       lc_ref, qs_ref, qw_ref, m_ref, l_ref, acc_ref, *, n_slc):
    i = pl.program_id(2)
    nl = HPG * TQ
    ncp = kcb_ref.shape[2]
    q_t = jnp.concatenate([q_ref[0, 0, r] for r in range(HPG)], axis=1)
    lane_t = lax.broadcasted_iota(jnp.int32, (1, nl), 1) & (TQ - 1)
    qpos = i * TQ + lane_t
    far = far_ref[0, 0:1, :]
    qs_ref[0:HEAD_DIM, :] = q_t
    qs_ref[HEAD_DIM + LANES:, :] = far16_ref[0]
    qw_ref[0:HEAD_DIM, :] = q_t
    qw_ref[HEAD_DIM:, :] = far16_ref[0]

    lc_ref[0:CPT, :] = jnp.zeros((CPT, nl), F32)
    lc_ref[CPT:CPT + ncp, :] = _dot(kcb_ref[0, 0], q_t) + far
    near = pl.ds(pl.multiple_of(i * CPT, CPT), 2 * CPT)
    lc_ref[near, :] = lc_ref[near, :] + pcd_ref[0]
    lc = lc_ref[CPT:CPT + ncp, :]
    cend = lax.broadcasted_iota(jnp.int32, (ncp, 1), 0) * CMP_STRIDE + (CMP_BLOCK - 1)
    valid_c = cend <= qpos
    lm = jnp.where(valid_c, lc, NEG)
    mc = jnp.max(lm, axis=0, keepdims=True)
    ec = jnp.where(valid_c, jnp.exp2(lm - mc), 0.0)
    den = jnp.sum(ec, axis=0, keepdims=True)
    p_c = ec / jnp.where(den > 0, den, 1.0)
    o_c = _dot(vcbt_ref[0, 0], p_c.astype(BF16))

    psum = p_c[:, 0:TQ]
    for r in range(1, HPG):
        psum = psum + p_c[:, r * TQ:(r + 1) * TQ]
    hi, lo = _split_hi_lo(psum)
    imp_t = _dot(covt_ref[...], hi) + _dot(covt_ref[...], lo)
    jrow = lax.broadcasted_iota(jnp.int32, (LANES, TQ), 0)
    qpos_t = qpos[:, 0:TQ]
    cur = qpos_t // SEL_BLOCK
    causal = (jrow * SEL_BLOCK <= qpos_t) & (jrow < n_slc)
    forced = (jrow == 0) | (jrow == cur) | (jrow == cur - 1)
    score = jnp.where(forced, FORCE_SCORE, jnp.where(causal, imp_t, NEG))
    k_sel = min(N_SEL, n_slc)
    topk = [(jnp.zeros(score.shape, F32), score), 0]

    def topk_rounds(n):
        for _ in range(min(n, k_sel - topk[1])):
            topk[0] = _top_block_round(*topk[0], jrow)
            topk[1] += 1

    key_u = lax.broadcasted_iota(jnp.int32, (TQ, 1), 0)
    causal_diag = key_u <= lane_t

    def run_pieces(specs, between=lambda: None):
        def logits(t):
            k_ref, _, qx_ref, kt, bias, mask = specs[t]
            s = _dot(k_ref[0, 0, pl.ds(pl.multiple_of(kt * TQ, TQ), TQ), :], qx_ref[...])
            if bias is not None:
                s = s + bias
            return s if mask is None else jnp.where(mask, s, NEG)

        n = len(specs)
        s = {t: logits(t) for t in range(min(PIPE_AHEAD, n))}
        between()
        pieces = []
        for t in range(n):
            m = jnp.max(s[t], axis=0, keepdims=True)
            p = jnp.exp2(s.pop(t) - m)
            if t + PIPE_AHEAD < n:
                s[t + PIPE_AHEAD] = logits(t + PIPE_AHEAD)
            between()
            _, vt_ref, _, kt, _, _ = specs[t]
            pieces.append((_dot(vt_ref[0, 0, kt], p.astype(BF16)), m, jnp.sum(p, axis=0, keepdims=True)))
            between()
        return pieces

    def merge(pieces):
        m_new = functools.reduce(jnp.maximum, [m for _, m, _ in pieces])
        scales = [jnp.exp2(m - m_new) for _, m, _ in pieces]
        acc = functools.reduce(jnp.add, [a * acc for a, (acc, _, _) in zip(scales, pieces)])
        l = functools.reduce(jnp.add, [a * l for a, (_, _, l) in zip(scales, pieces)])
        return acc, m_new, l

    prev = jnp.maximum(i - 1, 0)
    prev2 = jnp.maximum(i - 2, 0)
    tz1 = tz1_ref[0] + jnp.where(i >= 1, 0.0, NEG)
    tz0 = tz0_ref[0]
    n_gaps = 1 + 2 * 3
    acc_w, _, l_w = merge(run_pieces([(kw_ref, vwt_ref, qw_ref, prev2, None, (key_u > lane_t) & (i >= 2)),
                                      (kw_ref, vwt_ref, qw_ref, prev, tz1, None),
                                      (kw_ref, vwt_ref, qw_ref, i, tz0, causal_diag)],
                                     between=lambda: topk_rounds(-(-k_sel // n_gaps))))
    o_w = acc_w / l_w
    topk_rounds(k_sel)
    chosen = topk[0][0]
    sel_add = jnp.where(causal & (chosen > 0.5), 0.0, NEG).astype(BF16)
    qs_ref[HEAD_DIM:HEAD_DIM + LANES, :] = jnp.concatenate([sel_add] * HPG, axis=1)

    m_ref[...] = jnp.full(m_ref.shape, NEG, F32)
    l_ref[...] = jnp.zeros(l_ref.shape, F32)
    acc_ref[...] = jnp.zeros(acc_ref.shape, F32)

    def state():
        return acc_ref[...], m_ref[...], l_ref[...]

    def merge_far(kt0, n):
        acc, m, l = merge([state()] + run_pieces([(ks_ref, vst_ref, qs_ref, kt0 + t, None, None) for t in range(n)]))
        acc_ref[...] = acc
        m_ref[...] = m
        l_ref[...] = l

    n_far = jnp.maximum(i - 1, 0)

    def far_group(jg, carry):
        merge_far(FAR_GROUP * jg, FAR_GROUP)
        return carry

    lax.fori_loop(0, n_far // FAR_GROUP, far_group, 0)
    rest = n_far % FAR_GROUP
    size = FAR_GROUP // 2
    while size >= 1:
        @pl.when((rest & size) != 0)
        def _(size=size):
            merge_far(n_far - (rest & (2 * size - 1)), size)
        size //= 2

    acc_s, _, l_s = merge([state()] + run_pieces([(ks_ref, vst_ref, qs_ref, prev, tz1, None),
                                                  (ks_ref, vst_ref, qs_ref, i, tz0, causal_diag)]))
    o_s = acc_s / l_s

    g_t = gt_ref[0].T

    def gate_row(branch):
        return jnp.concatenate([g_t[branch * HPG + r:branch * HPG + r + 1, :] for r in range(HPG)], axis=1)

    o_t = gate_row(0) * o_c + gate_row(1) * o_s + gate_row(2) * o_w
    o_rd = jnp.concatenate([o_t[:, r * TQ:(r + 1) * TQ] for r in range(HPG)], axis=0)
    o_ref[0] = (o_rd.T * sa_ref[0]).astype(BF16)


def _nsa_prompt(q_t, gt, sa, kcb, vcb_t, ks, vs_t, kw, vw_t, rel_bias):
    b, _, _, _, s = q_t.shape
    assert WINDOW == 2 * TQ and s % TQ == 0 and TQ >= MAX_DISTANCE
    nq = s // TQ
    ncp = s // CMP_STRIDE
    n_cmp = ncp - 1
    n_slc = -(-s // SEL_BLOCK)
    assert n_slc <= LANES and ncp >= 2 * CPT
    nl = HPG * TQ
    table = rel_bias.astype(F32)
    uu = np.arange(TQ)[:, None]
    tt = np.arange(TQ)[None, :]

    def per_group(tab):
        rows = tab.shape[1]
        return jnp.transpose(tab.reshape(N_KV, HPG, rows, TQ), (0, 2, 1, 3)).reshape(N_KV, rows, nl)

    table2 = table * LOG2E
    heads_first = lambda dist: jnp.moveaxis(_bias_lookup(table2, dist), -1, 0)
    far_h = _bias_lookup(table2, np.array([MAX_DISTANCE]))[0]
    far_b = far_h[:, None, None]
    tz0 = per_group(heads_first(tt - uu) - far_b)
    tz1 = per_group(heads_first(TQ + tt - uu) - far_b)
    far = per_group(jnp.broadcast_to(far_b, (N_HEADS, 8, TQ)))
    far_hi = far_h.astype(BF16)
    far_lo = (far_h - far_hi.astype(F32)).astype(BF16)
    far16 = jnp.zeros((N_HEADS, AUG_PAD, TQ), BF16).at[:, 0].set(far_hi[:, None]).at[:, 1].set(far_lo[:, None])
    far16 = per_group(far16)
    e = np.arange(2 * CPT)[:, None] - CPT
    pcd = per_group(heads_first(tt - CMP_STRIDE * e - (CMP_BLOCK - 1)) - far_b)
    c = np.arange(ncp)[None, :]
    j = np.arange(LANES)[:, None]
    cov_t = ((c * CMP_STRIDE < (j + 1) * SEL_BLOCK) & (c * CMP_STRIDE + CMP_BLOCK > j * SEL_BLOCK)
             & (c < n_cmp) & (j < n_slc))
    cov_t = jnp.asarray(cov_t, BF16)

    per_g = lambda rows: pl.BlockSpec((1, rows, nl), lambda bi, g, i: (g, 0, 0))
    return pl.pallas_call(
        functools.partial(_nsa_prompt_kernel, n_slc=n_slc),
        grid=(b, N_KV, nq),
        in_specs=[
            pl.BlockSpec((1, 1, HPG, HEAD_DIM, TQ), lambda bi, g, i: (bi, g, 0, 0, i)),
            pl.BlockSpec((1, TQ, LANES), lambda bi, g, i: (bi, i, g)),
            pl.BlockSpec((1, TQ, KV_WIDTH), lambda bi, g, i: (bi, i, g)),
            pl.BlockSpec((1, 1, ncp, HEAD_DIM), lambda bi, g, i: (bi, g, 0, 0)),
            pl.BlockSpec((1, 1, HEAD_DIM, ncp), lambda bi, g, i: (bi, g, 0, 0)),
            pl.BlockSpec((1, 1, s, K_SEL_AUG), lambda bi, g, i: (bi, g, 0, 0)),
            pl.BlockSpec((1, 1, nq, HEAD_DIM, TQ), lambda bi, g, i: (bi, g, 0, 0, 0)),
            pl.BlockSpec((1, 1, s, K_WIN_AUG), lambda bi, g, i: (bi, g, 0, 0)),
            pl.BlockSpec((1, 1, nq, HEAD_DIM, TQ), lambda bi, g, i: (bi, g, 0, 0, 0)),
            per_g(2 * CPT), per_g(8), per_g(AUG_PAD), per_g(TQ), per_g(TQ),
            pl.BlockSpec(cov_t.shape, lambda bi, g, i: (0, 0)),
        ],
        out_specs=pl.BlockSpec((1, TQ, KV_WIDTH), lambda bi, g, i: (bi, i, g)),
        out_shape=jax.ShapeDtypeStruct((b, s, NSA_WIDTH), BF16),
        scratch_shapes=[pltpu.VMEM((CPT + ncp, nl), F32), pltpu.VMEM((K_SEL_AUG, nl), BF16),
                        pltpu.VMEM((K_WIN_AUG, nl), BF16),
                        pltpu.VMEM((1, nl), F32), pltpu.VMEM((1, nl), F32), pltpu.VMEM((HEAD_DIM, nl), F32)],
        compiler_params=_cparams(("parallel", "parallel", "arbitrary")),
        name="nsa_prompt",
    )(q_t, gt, sa, kcb, vcb_t, ks, vs_t, kw, vw_t, pcd, far, far16, tz0, tz1, cov_t)


def _nsa_sample_kernel(*refs, n_pg, n_chunks, n_slc, past, t_new):
    (qbd_ref, kcbt_ref, vcb_ref, bc_ref, cov_ref, rmat_ref) = refs[1:7]
    kpg = refs[7:7 + n_pg]
    vpg = refs[7 + n_pg:7 + 2 * n_pg]
    (bsl_ref, ksn_ref, vsn_ref, bsn_ref, kwc_ref, vwc_ref, kwn_ref, vwn_ref, bw_ref, gt_ref, e4_ref,
     o_ref, sel_ref, m_ref, l_ref, acc_ref, oc_ref) = refs[7 + 2 * n_pg:]
    j = pl.program_id(1)
    qbd = qbd_ref[0]
    nsp = cov_ref.shape[1]
    n_win = kwc_ref.shape[2]

    def softmax_rows(s):
        m = jnp.max(s, axis=-1, keepdims=True)
        e = jnp.where(s > 0.5 * NEG, jnp.exp(s - m), 0.0)
        den = jnp.sum(e, axis=-1, keepdims=True)
        return e / jnp.where(den > 0, den, 1.0)

    def flash_step(s, v, v_is_transposed):
        m_old = m_ref[...]
        m_new = jnp.maximum(m_old, jnp.max(s, axis=-1, keepdims=True))
        alpha = jnp.exp(m_old - m_new)
        p = jnp.exp(s - m_new)
        l_ref[...] = alpha * l_ref[...] + jnp.sum(p, axis=-1, keepdims=True)
        pv = _dot_nt(p.astype(BF16), v) if v_is_transposed else _dot(p.astype(BF16), v)
        acc_ref[...] = alpha * acc_ref[...] + pv
        m_ref[...] = m_new

    @pl.when(j == 0)
    def _():
        p_c = softmax_rows(_dot(qbd, kcbt_ref[0]) + bc_ref[...])
        oc_ref[...] = _dot(p_c.astype(BF16), vcb_ref[0])
        hi, lo = _split_hi_lo(p_c)
        psum = _dot(rmat_ref[...], hi) + _dot(rmat_ref[...], lo)
        hi, lo = _split_hi_lo(psum)
        imp_t = (_dot(hi, cov_ref[...]) + _dot(lo, cov_ref[...])).T
        jrow = lax.broadcasted_iota(jnp.int32, (nsp, LANES), 0)
        lane = lax.broadcasted_iota(jnp.int32, (1, LANES), 1)
        qpos = past + lane % t_new
        cur = qpos // SEL_BLOCK
        causal = (jrow * SEL_BLOCK <= qpos) & (jrow < n_slc)
        forced = (jrow == 0) | (jrow == cur) | (jrow == cur - 1)
        score = jnp.where(forced, FORCE_SCORE, jnp.where(causal, imp_t, NEG))
        chosen = _select_top_blocks(score, jrow, min(N_SEL, n_slc))
        sel = jnp.where(causal & (lane < N_HEADS * t_new) & (chosen > 0.5), 0.0, -1.0).T
        for k in range(nsp // LANES):
            sel_ref[k] = sel[:, k * LANES:(k + 1) * LANES]
        m_ref[...] = jnp.full(m_ref.shape, NEG, F32)
        l_ref[...] = jnp.zeros(l_ref.shape, F32)
        acc_ref[...] = jnp.zeros(acc_ref.shape, F32)

    far = bsl_ref[:, PAGE_SIZE:2 * PAGE_SIZE]
    last = jnp.where(j == n_chunks - 1, bsl_ref[:, 0:PAGE_SIZE], far)
    chunks_per_tile = LANES // (n_pg * (PAGE_SIZE // SEL_BLOCK))
    sel_tile = sel_ref[j // chunks_per_tile].astype(BF16)
    pp = SAMPLE_PIECE_PAGES
    n_pieces = n_pg // pp

    def logits(t):
        k_t = jnp.concatenate([r[0] for r in kpg[t * pp:(t + 1) * pp]], axis=1).astype(BF16)
        bias = jnp.concatenate([far] * (pp - 1) + [last if t == n_pieces - 1 else far], axis=1)
        mask_add = _dot(sel_tile, e4_ref[j % chunks_per_tile, :, t * pp * PAGE_SIZE:(t + 1) * pp * PAGE_SIZE])
        return _dot(qbd, k_t) + bias + mask_add

    s = {t: logits(t) for t in range(min(PIPE_AHEAD, n_pieces))}
    pieces = []
    for t in range(n_pieces):
        m = jnp.max(s[t], axis=-1, keepdims=True)
        p = jnp.exp(s.pop(t) - m)
        if t + PIPE_AHEAD < n_pieces:
            s[t + PIPE_AHEAD] = logits(t + PIPE_AHEAD)
        v_t = jnp.concatenate([r[0] for r in vpg[t * pp:(t + 1) * pp]], axis=1).astype(BF16)
        pieces.append((_dot_nt(p.astype(BF16), v_t), m, jnp.sum(p, axis=-1, keepdims=True)))
    m_old = m_ref[...]
    m_new = functools.reduce(jnp.maximum, [m_old] + [m for _, m, _ in pieces])
    scales = [jnp.exp(m - m_new) for _, m, _ in pieces]
    a_old = jnp.exp(m_old - m_new)
    l_ref[...] = a_old * l_ref[...] + functools.reduce(jnp.add, [a * l for a, (_, _, l) in zip(scales, pieces)])
    acc_ref[...] = a_old * acc_ref[...] + functools.reduce(jnp.add, [a * acc for a, (acc, _, _) in zip(scales, pieces)])
    m_ref[...] = m_new

    @pl.when(j == n_chunks - 1)
    def _():
        sn = _dot_nt(qbd, ksn_ref[0].astype(BF16)) + bsn_ref[...]
        blk = n_slc - 1
        seln = sel_ref[blk // LANES][:, blk % LANES:blk % LANES + 1]
        flash_step(sn + seln * (-NEG), vsn_ref[0].astype(BF16), False)
        l = l_ref[...]
        o_s = acc_ref[...] / jnp.where(l > 0, l, 1.0)
        sw = jnp.concatenate([_dot(qbd, kwc_ref[0].astype(BF16)), _dot_nt(qbd, kwn_ref[0].astype(BF16))], axis=1)
        p_w = softmax_rows(sw + bw_ref[...]).astype(BF16)
        o_w = _dot_nt(p_w[:, :n_win], vwc_ref[0].astype(BF16)) + _dot(p_w[:, n_win:], vwn_ref[0].astype(BF16))
        gt = gt_ref[0]
        o_ref[0] = gt[:, 0:1] * oc_ref[...] + gt[:, 1:2] * o_s + gt[:, 2:3] * o_w


def _nsa_sample(q, gates, kcb_t, vcb, cache_k_slc, cache_v_slc, page_table, ks_new, vs_new,
                kw_cache, vw_cache, kw_new, vw_new, rel_bias, *, past):
    db, t_new, _ = q.shape
    n_pages = page_table.shape[1]
    assert past == n_pages * PAGE_SIZE and past % SEL_BLOCK == 0 and t_new <= SEL_BLOCK
    assert PAGE_SIZE >= MAX_DISTANCE
    n_cp = kcb_t.shape[2]
    n_slc = -(-(past + t_new) // SEL_BLOCK)
    nsp = -(-n_slc // LANES) * LANES
    n_pg = SAMPLE_PAGES
    n_chunks = n_pages // n_pg
    blocks_per_chunk = n_pg * (PAGE_SIZE // SEL_BLOCK)
    assert LANES % blocks_per_chunk == 0
    n_win = kw_cache.shape[2]
    nl = N_HEADS * t_new
    assert nl <= LANES
    row = np.arange(LANES)
    row_ok = (row < nl)[:, None]
    row_g = np.where(row < nl, row // (HPG * t_new), 0)
    row_t = (row % t_new)[:, None]
    q5 = q.reshape(db, t_new, N_KV, HPG, HEAD_DIM)
    qbd = jnp.einsum('btgrd,gk->bgrtkd', q5.astype(F32), jnp.eye(N_KV, dtype=F32)).reshape(db, nl, KV_WIDTH)
    qbd = jnp.pad(qbd, ((0, 0), (0, LANES - nl), (0, 0))).astype(BF16)

    table = rel_bias.astype(F32)

    def bias_tab(dist, valid):
        vals = jnp.moveaxis(_bias_lookup(table, dist), -1, 0)
        vals = jnp.where(jnp.asarray(valid)[None], vals, NEG).reshape(nl, -1)
        return jnp.pad(vals, ((0, LANES - nl), (0, 0)), constant_values=NEG)

    tq = np.arange(t_new)[:, None]
    qpos = past + tq
    cblk = np.arange(n_cp)[None, :]
    dist_c = qpos - (cblk * CMP_STRIDE + CMP_BLOCK - 1)
    bc = bias_tab(dist_c, (dist_c >= 0) & (cblk < n_cp - 1))
    kpos = past - PAGE_SIZE + np.arange(PAGE_SIZE)[None, :]
    all_ok = np.ones((t_new, PAGE_SIZE), bool)
    bsl = jnp.concatenate([bias_tab(qpos - kpos, all_ok),
                           bias_tab(np.full((t_new, PAGE_SIZE), MAX_DISTANCE), all_ok)], axis=1)
    u = np.arange(PAGE_SIZE)[None, :]
    new_ok = (u <= tq) & (u < t_new)
    bsn = bias_tab(tq - u, new_ok)
    wpos = past - n_win + np.arange(n_win)[None, :]
    dist_w = qpos - wpos
    bw = jnp.concatenate([bias_tab(dist_w, (dist_w >= 0) & (dist_w < WINDOW) & (wpos >= 0)),
                          bias_tab(tq - u, new_ok & (tq - u < WINDOW))], axis=1)
    c = np.arange(n_cp)[:, None]
    jb = np.arange(nsp)[None, :]
    cov = ((c * CMP_STRIDE < (jb + 1) * SEL_BLOCK) & (c * CMP_STRIDE + CMP_BLOCK > jb * SEL_BLOCK)
           & (c < n_cp - 1) & (jb < n_slc))
    cov = jnp.asarray(cov, BF16)
    same = (row_g[:, None] == row_g[None, :]) & (row_t == row_t.T) & row_ok & row_ok.T
    rmat = jnp.asarray(same, BF16)
    kk = np.arange(n_pg * PAGE_SIZE)[None, None, :] // SEL_BLOCK
    e4 = np.arange(LANES)[None, :, None] == (np.arange(LANES // blocks_per_chunk)[:, None, None] * blocks_per_chunk + kk)
    e4 = jnp.asarray(np.where(e4, -NEG, 0.0), BF16)
    g5 = gates.reshape(db, t_new, N_KV, LANES)[..., :3 * HPG].reshape(db, t_new, N_KV, 3, HPG)
    gcol = jnp.transpose(g5, (0, 2, 4, 1, 3)).reshape(db, nl, 3)
    gcol = jnp.pad(gcol, ((0, 0), (0, LANES - nl), (0, 5)))

    def pad_new(a):
        return jnp.pad(a, ((0, 0), (0, PAGE_SIZE - t_new), (0, 0)))

    ksn, vsn, kwn, vwn = (pad_new(a) for a in (ks_new, vs_new, kw_new, vw_new))

    per_b = lambda shape: pl.BlockSpec((1,) + shape, lambda b, jc, pt: (b, 0, 0))
    full = lambda a: pl.BlockSpec(a.shape, lambda b, jc, pt: (0,) * a.ndim)
    page = lambda p: pl.BlockSpec((1, KV_WIDTH, PAGE_SIZE), lambda b, jc, pt: (pt[b, jc * SAMPLE_PAGES + p], 0, 0))
    in_specs = ([per_b((LANES, KV_WIDTH)), per_b((KV_WIDTH, n_cp)), per_b((n_cp, KV_WIDTH)),
                 full(bc), full(cov), full(rmat)]
                + [page(p) for p in range(n_pg)] * 2
                + [full(bsl), per_b((PAGE_SIZE, KV_WIDTH)), per_b((PAGE_SIZE, KV_WIDTH)), full(bsn),
                   per_b((KV_WIDTH, n_win)), per_b((KV_WIDTH, n_win)),
                   per_b((PAGE_SIZE, KV_WIDTH)), per_b((PAGE_SIZE, KV_WIDTH)), full(bw), per_b((LANES, 8)), full(e4)])
    o = pl.pallas_call(
        functools.partial(_nsa_sample_kernel, n_pg=n_pg, n_chunks=n_chunks, n_slc=n_slc, past=past, t_new=t_new),
        grid_spec=pltpu.PrefetchScalarGridSpec(
            num_scalar_prefetch=1, grid=(db, n_chunks), in_specs=in_specs,
            out_specs=pl.BlockSpec((1, LANES, KV_WIDTH), lambda b, jc, pt: (b, 0, 0)),
            scratch_shapes=[pltpu.VMEM((nsp // LANES, LANES, LANES), F32), pltpu.VMEM((LANES, 1), F32),
                            pltpu.VMEM((LANES, 1), F32), pltpu.VMEM((LANES, KV_WIDTH), F32),
                            pltpu.VMEM((LANES, KV_WIDTH), F32)]),
        out_shape=jax.ShapeDtypeStruct((db, LANES, KV_WIDTH), F32),
        compiler_params=_cparams(("arbitrary", "arbitrary")),
        name="nsa_sample",
    )(page_table, qbd, kcb_t, vcb, bc, cov, rmat, *([cache_k_slc] * n_pg), *([cache_v_slc] * n_pg),
      bsl, ksn, vsn, bsn, kw_cache, vw_cache, kwn, vwn, bw, gcol, e4)
    o6 = o[:, :nl].reshape(db, N_KV, HPG, t_new, N_KV, HEAD_DIM)
    o_diag = jnp.stack([o6[:, g, :, :, g] for g in range(N_KV)], axis=1)
    return jnp.transpose(o_diag, (0, 3, 1, 2, 4)).reshape(db, t_new, NSA_WIDTH)


def _conv_tail(y, cb_ref, lg_ref, lb_ref, wpw_ref, bpw_ref, sb):
    y = y + cb_ref[...]
    mu = jnp.mean(y, axis=-1, keepdims=True)
    yc = y - mu
    var = jnp.mean(yc * yc, axis=-1, keepdims=True)
    yn = yc * lax.rsqrt(var + LN_EPS) * lg_ref[...] + lb_ref[...]
    act = yn * _sigmoid(yn)
    return ((_dot(act.astype(BF16), wpw_ref[...]) + bpw_ref[...]) * sb).astype(BF16)


def _conv_prompt_kernel(c_ref, halo_ref, init_ref, cw_ref, cb_ref, lg_ref, lb_ref, wpw_ref, bpw_ref, sb_ref,
                        o_ref, full_ref, sh_ref, y_ref, *, ts):
    j = pl.program_id(1)
    full_ref[CONV_HALO:CONV_HALO + ts, :] = c_ref[0]

    @pl.when(j == 0)
    def _():
        full_ref[0:CONV_HALO, :] = init_ref[0]

    @pl.when(j > 0)
    def _():
        full_ref[0:CONV_HALO, :] = halo_ref[0]

    first = CONV_HALO - (CONV_WIDTH - 1)
    span = sh_ref.shape[1]
    for sft in range(1, 8):
        sh_ref[sft - 1] = full_ref[sft:sft + span, :]
    rb = 64
    ch = full_ref.shape[1]
    for c0 in range(0, ch, LANES):
        for r0 in range(0, ts, rb):
            acc = jnp.zeros((rb, LANES), F32)
            for w in range(CONV_WIDTH):
                sft = (first + w) % 8
                base = r0 + first + w - sft
                if sft == 0:
                    x = full_ref[base:base + rb, c0:c0 + LANES]
                else:
                    x = sh_ref[sft - 1, base:base + rb, c0:c0 + LANES]
                acc = acc + x * cw_ref[w:w + 1, c0:c0 + LANES]
            y_ref[r0:r0 + rb, c0:c0 + LANES] = acc
    o_ref[0] = _conv_tail(y_ref[...], cb_ref, lg_ref, lb_ref, wpw_ref, bpw_ref, sb_ref[0])


def _conv_sample_kernel(c_ref, st_ref, cw_ref, cb_ref, lg_ref, lb_ref, wpw_ref, bpw_ref, sb_ref,
                        o_ref, full_ref, y_ref):
    nb, t_new, _ = c_ref.shape
    n_st = st_ref.shape[1]
    full_ref[:, 0:n_st, :] = st_ref[...]
    full_ref[:, n_st:n_st + t_new, :] = c_ref[...]
    first = n_st - (CONV_WIDTH - 1)
    for b in range(nb):
        acc = jnp.zeros((t_new, full_ref.shape[2]), F32)
        for w in range(CONV_WIDTH):
            acc = acc + full_ref[b, first + w:first + w + t_new, :] * cw_ref[w:w + 1, :]
        y_ref[b * t_new:(b + 1) * t_new, :] = acc
    o_ref[...] = _conv_tail(y_ref[...], cb_ref, lg_ref, lb_ref, wpw_ref, bpw_ref, sb_ref[...])


def _conv_params(conv_w, conv_b, ln_g, ln_b, w_pw, b_pw):
    ch = conv_w.shape[1]
    cw = jnp.pad(conv_w, ((0, 32 - CONV_WIDTH), (0, 0)))
    return (cw, conv_b.reshape(1, ch), ln_g.reshape(1, ch), ln_b.reshape(1, ch), w_pw.astype(BF16),
            b_pw.reshape(1, ch))


def _conv_prompt(c_in, init, params, sb, *, ts):
    b, s, ch = c_in.shape
    hb = ts // CONV_HALO
    const = lambda a: pl.BlockSpec(a.shape, lambda bi, j: (0,) * a.ndim)
    return pl.pallas_call(
        functools.partial(_conv_prompt_kernel, ts=ts),
        grid=(b, s // ts),
        in_specs=[pl.BlockSpec((1, ts, ch), lambda bi, j: (bi, j, 0)),
                  pl.BlockSpec((1, CONV_HALO, ch), lambda bi, j: (bi, jnp.maximum(j * hb - 1, 0), 0)),
                  pl.BlockSpec((1, CONV_HALO, ch), lambda bi, j: (bi, 0, 0))]
        + [const(a) for a in params]
        + [pl.BlockSpec((1, ts, ch), lambda bi, j: (bi, j, 0))],
        out_specs=pl.BlockSpec((1, ts, ch), lambda bi, j: (bi, j, 0)),
        out_shape=jax.ShapeDtypeStruct((b, s, ch), BF16),
        scratch_shapes=[pltpu.VMEM((CONV_HALO + ts, ch), F32), pltpu.VMEM((7, CONV_HALO + ts - 8, ch), F32),
                        pltpu.VMEM((ts, ch), F32)],
        compiler_params=_cparams(("parallel", "arbitrary")),
        name="conv_prompt",
    )(c_in, c_in, init, *params, sb)


def _conv_sample(c_in, state, params, sb):
    db, t_new, ch = c_in.shape
    n_st = state.shape[1]
    rows_pad = -(-(n_st + t_new) // 8) * 8
    return pl.pallas_call(
        _conv_sample_kernel,
        out_shape=jax.ShapeDtypeStruct((db * t_new, ch), BF16),
        scratch_shapes=[pltpu.VMEM((db, rows_pad, ch), F32), pltpu.VMEM((db * t_new, ch), F32)],
        compiler_params=pltpu.CompilerParams(vmem_limit_bytes=VMEM_LIMIT),
        name="conv_sample",
    )(c_in, state, *params, sb)


def _out_kernel(x_ref, ma_ref, mb_ref, *rest, gated):
    if gated:
        sa_ref, wa_ref, wb_ref, gp_ref, y_ref = rest
        ma = (ma_ref[0] * sa_ref[0]).astype(BF16)
    else:
        wa_ref, wb_ref, gp_ref, y_ref = rest
        ma = ma_ref[0]
    z = _dot(ma, wa_ref[...]) + _dot(mb_ref[0], wb_ref[...])
    ms = jnp.mean(z * z, axis=-1, keepdims=True)
    y_ref[0] = x_ref[0] + z * lax.rsqrt(ms + RMS_EPS) * gp_ref[...]


def _out_proj(x, ma, mb, sa, w_out, g_post, *, tm):
    b, s, d = x.shape
    na = ma.shape[-1]
    wa = w_out[:na].astype(BF16)
    wb = w_out[na:].astype(BF16)
    row = lambda width: pl.BlockSpec((1, tm, width), lambda bi, i: (bi, i, 0))
    const = lambda a: pl.BlockSpec(a.shape, lambda bi, i: (0, 0))
    gp = g_post.reshape(1, d)
    gate_in, gate_spec = ([sa], [row(na)]) if sa is not None else ([], [])
    return pl.pallas_call(
        functools.partial(_out_kernel, gated=sa is not None),
        grid=(b, s // tm),
        in_specs=[row(d), row(na), row(mb.shape[-1])] + gate_spec + [const(wa), const(wb), const(gp)],
        out_specs=row(d),
        out_shape=jax.ShapeDtypeStruct((b, s, d), F32),
        compiler_params=_cparams(("parallel", "parallel")),
        name="out_proj",
    )(x, ma, mb, *gate_in, wa, wb, gp)


def _split_w_in(w_in):
    d = w_in.shape[0]
    c0 = NSA_WIDTH + 6 * KV_WIDTH
    n_gate = 3 * N_HEADS
    conv_ch = (w_in.shape[1] - c0 - n_gate - NSA_WIDTH) // 3
    w_qkv = w_in[:, :c0].astype(BF16)
    wg = w_in[:, c0:c0 + n_gate].reshape(d, N_KV, HPG, 3)
    wg = jnp.transpose(wg, (0, 1, 3, 2)).reshape(d, N_KV, 3 * HPG)
    wg = jnp.pad(wg, ((0, 0), (0, 0), (0, LANES - 3 * HPG))).reshape(d, N_KV * LANES)
    z_a = w_in[:, c0 + n_gate:c0 + n_gate + NSA_WIDTH]
    glu0 = c0 + n_gate + NSA_WIDTH
    w_glu = w_in[:, glu0:glu0 + 2 * conv_ch].astype(BF16)
    z_b = w_in[:, glu0 + 2 * conv_ch:]
    assert conv_ch == NSA_WIDTH
    w_gate = jnp.concatenate([z_a, z_b, wg], axis=1).astype(BF16)
    return w_qkv, w_gate, w_glu


def _token_minor(cache):
    n, tokens = cache.shape[:2]
    return jnp.transpose(cache, (0, 2, 3, 1)).reshape(n, KV_WIDTH, tokens)


def kernel(x_prompt, x_sample, cache_k_cmp, cache_v_cmp, cache_k_slc, cache_v_slc, cache_k_win, cache_v_win,
           state_conv, page_table, g_pre, w_in, cmp_w1_k, cmp_w2_k, cmp_pe_k, cmp_w1_v, cmp_w2_v, cmp_pe_v,
           rel_bias, conv_w, conv_b, ln_g, ln_b, w_pw, b_pw, w_out, g_post):
    depth = g_pre.shape[0]
    assert depth == 1
    layer = 0
    b, s, d = x_prompt.shape
    db, t_new, _ = x_sample.shape
    past = page_table.shape[1] * PAGE_SIZE
    conv_ch = conv_w.shape[-1]

    w_qkv, w_gate, w_glu = _split_w_in(w_in[layer])
    wts_k = _cmp_weights(cmp_w1_k[layer], cmp_w2_k[layer], cmp_pe_k[layer])
    wts_v = _cmp_weights(cmp_w1_v[layer], cmp_w2_v[layer], cmp_pe_v[layer])
    cparams = _conv_params(conv_w[layer], conv_b[layer], ln_g[layer], ln_b[layer], w_pw[layer], b_pw[layer])
    chunk_w = CMP_STRIDE * KV_WIDTH

    (q_t, kc, vc, ks, vs, kw, vw, ks_g, kw_g, vs_t, vw_t), sa, sb, gt, c_in = _projections(
        x_prompt, g_pre[layer], w_qkv, w_gate, w_glu, tm=512, attn_layouts=True)
    n_ch = s // CMP_STRIDE
    kcb = _compress(kc[:, :n_ch * CMP_STRIDE].reshape(b, n_ch, chunk_w), wts_k, layout="group_rows")
    vcb_t = _compress(vc[:, :n_ch * CMP_STRIDE].reshape(b, n_ch, chunk_w), wts_v, layout="group_cols")
    ma = _nsa_prompt(q_t, gt, sa, kcb, vcb_t, ks_g, vs_t, kw_g, vw_t, rel_bias)
    mb = _conv_prompt(c_in, jnp.zeros((b, CONV_HALO, conv_ch), F32), cparams, sb, ts=256)
    y_prompt = _out_proj(x_prompt, ma, mb, None, w_out[layer], g_post[layer], tm=512)
    n_keep = min(WINDOW, s)
    kv5 = lambda a: a.reshape(1, a.shape[0], a.shape[1], N_KV, HEAD_DIM)
    outs_p = (kv5(kc), kv5(vc), kv5(ks), kv5(vs), kv5(kw[:, -n_keep:]), kv5(vw[:, -n_keep:]),
              c_in[None, :, -(CONV_WIDTH - 1):])

    xs = x_sample.reshape(1, db * t_new, d)
    (q_s, kc_s, vc_s, ks_s, vs_s, kw_s, vw_s), sa_s, sb_s, gt_s, c_s = _projections(
        xs, g_pre[layer], w_qkv, w_gate, w_glu, tm=db * t_new, attn_layouts=False)
    tok = lambda a: a.reshape(db, t_new, a.shape[-1])
    kcb_s = _compress(_token_minor(cache_k_cmp[layer]), wts_k, page_table=page_table, layout="cols")
    vcb_s = _compress(_token_minor(cache_v_cmp[layer]), wts_v, page_table=page_table, layout="rows")
    o_a = _nsa_sample(tok(q_s), tok(gt_s), kcb_s, vcb_s,
                      _token_minor(cache_k_slc[layer]), _token_minor(cache_v_slc[layer]), page_table,
                      tok(ks_s), tok(vs_s), _token_minor(cache_k_win[layer]), _token_minor(cache_v_win[layer]),
                      tok(kw_s), tok(vw_s), rel_bias, past=past)
    mb_s = _conv_sample(tok(c_s), state_conv[layer], cparams, sb_s[0])
    y_sample = _out_proj(xs, o_a.reshape(1, db * t_new, NSA_WIDTH), mb_s[None], sa_s, w_out[layer],
                         g_post[layer], tm=db * t_new).reshape(db, t_new, d)
    n_keep_s = min(WINDOW, past + t_new)
    kv5s = lambda a: a.reshape(1, db, t_new, N_KV, HEAD_DIM)
    win = lambda cache, new: jnp.concatenate(
        [cache[layer], new.reshape(db, t_new, N_KV, HEAD_DIM)], axis=1)[None, :, -n_keep_s:]
    conv_s = jnp.concatenate([state_conv[layer], tok(c_s)], axis=1)[None, :, -(CONV_WIDTH - 1):]
    outs_s = (kv5s(kc_s), kv5s(vc_s), kv5s(ks_s), kv5s(vs_s), win(cache_k_win, kw_s), win(cache_v_win, vw_s), conv_s)
    return (y_prompt, y_sample) + outs_p + outs_s
```

```python
import functools
import math

import numpy as np
import jax
import jax.numpy as jnp
from jax import lax
from jax.experimental import pallas as pl
from jax.experimental.pallas import tpu as pltpu

F32 = jnp.float32
BF16 = jnp.bfloat16

HEAD_DIM = 64
N_KV = 4
HPG = 4
N_HEADS = N_KV * HPG
KV_WIDTH = N_KV * HEAD_DIM
NSA_WIDTH = N_HEADS * HEAD_DIM
CMP_BLOCK = 32
CMP_STRIDE = 16
SEL_BLOCK = 64
N_SEL = 16
WINDOW = 512
CONV_WIDTH = 31
N_BUCKETS = 32
MAX_DISTANCE = 128
FORCE_SCORE = 1e6
RMS_EPS = 1e-6
LN_EPS = 1e-5
PAGE_SIZE = 128
SCALE = HEAD_DIM ** -0.5

NEG = -1e30
PICKED = -3e38
LANES = 128
VMEM_LIMIT = 56 * 1024 * 1024

LOG2E = 1.4426950408889634
TQ = 256
FAR_GROUP = 4
PIPE_AHEAD = 2
AUG_PAD = 16
K_WIN_AUG = HEAD_DIM + AUG_PAD
K_SEL_AUG = HEAD_DIM + LANES + AUG_PAD
CPT = TQ // CMP_STRIDE
CONV_HALO = 32
SAMPLE_PAGES = 32
SAMPLE_PIECE_PAGES = 8
CMP_PAGES = 32


def _cparams(sem):
    return pltpu.CompilerParams(dimension_semantics=sem, vmem_limit_bytes=VMEM_LIMIT)


def _dot(a, b):
    return jnp.dot(a, b, preferred_element_type=F32)


def _dot_nt(a, b):
    return lax.dot_general(a, b, (((1,), (1,)), ((), ())), preferred_element_type=F32)


def _sigmoid(x):
    return 1.0 / (1.0 + jnp.exp(-x))


def _split_hi_lo(x):
    hi = x.astype(BF16)
    lo = (x - hi.astype(F32)).astype(BF16)
    return hi, lo


def _normed(x_ref, g_ref):
    x = x_ref[0]
    ms = jnp.mean(x * x, axis=-1, keepdims=True)
    return (x * lax.rsqrt(ms + RMS_EPS) * g_ref[...]).astype(BF16)


def _proj_qkv_kernel(x_ref, g_ref, w_ref, q_ref, kc_ref, vc_ref, ks_ref, vs_ref, kw_ref, vw_ref,
                     *attn_refs, attn_layouts):
    h = _normed(x_ref, g_ref)
    tm = h.shape[0]
    if attn_layouts:
        tok = pl.program_id(1) * tm + lax.broadcasted_iota(jnp.int32, (tm, LANES), 0)
        lane = lax.broadcasted_iota(jnp.int32, (tm, LANES), 1)
        blk_cols = jnp.where(lane == tok // SEL_BLOCK, 1.0, 0.0).astype(BF16)
        one_cols = jnp.where(lax.broadcasted_iota(jnp.int32, (tm, AUG_PAD), 1) < 2, 1.0, 0.0).astype(BF16)
    for g in range(N_KV):
        res = _dot(h, w_ref[:, g * KV_WIDTH:(g + 1) * KV_WIDTH]) * (SCALE * LOG2E if attn_layouts else SCALE)
        if attn_layouts:
            res_t = res.T
            for r in range(HPG):
                q_ref[0, g, r] = res_t[r * HEAD_DIM:(r + 1) * HEAD_DIM, :].astype(BF16)
        else:
            q_ref[0, :, g * KV_WIDTH:(g + 1) * KV_WIDTH] = res.astype(BF16)
    for j, o_ref in enumerate((kc_ref, vc_ref, ks_ref, vs_ref, kw_ref, vw_ref)):
        c0 = NSA_WIDTH + j * KV_WIDTH
        res = _dot(h, w_ref[:, c0:c0 + KV_WIDTH])
        o_ref[0] = res
        if attn_layouts and j in (2, 4):
            extra = [blk_cols, one_cols] if j == 2 else [one_cols]
            for g in range(N_KV):
                attn_refs[j // 2 - 1][0, g] = jnp.concatenate(
                    [res[:, g * HEAD_DIM:(g + 1) * HEAD_DIM].astype(BF16)] + extra, axis=1)
        if attn_layouts and j in (3, 5):
            res_t = res.T.astype(BF16)
            for g in range(N_KV):
                for kt in range(tm // TQ):
                    attn_refs[2 + j // 2 - 1][0, g, kt] = res_t[g * HEAD_DIM:(g + 1) * HEAD_DIM, kt * TQ:(kt + 1) * TQ]


def _proj_gate_kernel(x_ref, g_ref, w_ref, sa_ref, sb_ref, gt_ref):
    h = _normed(x_ref, g_ref)
    for o_ref, base in ((sa_ref, 0), (sb_ref, NSA_WIDTH)):
        for c in range(NSA_WIDTH // 256):
            z = _dot(h, w_ref[:, base + c * 256: base + (c + 1) * 256])
            o_ref[0, :, c * 256:(c + 1) * 256] = z * _sigmoid(z)
    for c in range(2):
        z = _dot(h, w_ref[:, 2 * NSA_WIDTH + c * 256: 2 * NSA_WIDTH + (c + 1) * 256])
        gt_ref[0, :, c * 256:(c + 1) * 256] = _sigmoid(z)


def _proj_glu_kernel(x_ref, g_ref, w_ref, c_ref):
    h = _normed(x_ref, g_ref)
    n = c_ref.shape[-1]
    for c in range(n // 256):
        a = _dot(h, w_ref[:, c * 256:(c + 1) * 256])
        gg = _dot(h, w_ref[:, n + c * 256: n + (c + 1) * 256])
        c_ref[0, :, c * 256:(c + 1) * 256] = a * _sigmoid(gg)


def _projections(x, g_pre, w_qkv, w_gate, w_glu, *, tm, attn_layouts):
    b, s, d = x.shape
    grid = (b, s // tm)
    x_spec = pl.BlockSpec((1, tm, d), lambda bi, i: (bi, i, 0))
    g_spec = pl.BlockSpec((1, d), lambda bi, i: (0, 0))

    def w_spec(w):
        return pl.BlockSpec(w.shape, lambda bi, i: (0, 0))

    def row_spec(width):
        return pl.BlockSpec((1, tm, width), lambda bi, i: (bi, i, 0))

    kv_shape = jax.ShapeDtypeStruct((b, s, KV_WIDTH), F32)
    if attn_layouts:
        assert tm % TQ == 0
        q_shape = jax.ShapeDtypeStruct((b, N_KV, HPG, HEAD_DIM, s), BF16)
        q_spec = pl.BlockSpec((1, N_KV, HPG, HEAD_DIM, tm), lambda bi, i: (bi, 0, 0, 0, i))
        k_shape = lambda w: jax.ShapeDtypeStruct((b, N_KV, s, w), BF16)
        k_spec = lambda w: pl.BlockSpec((1, N_KV, tm, w), lambda bi, i: (bi, 0, i, 0))
        v_shape = jax.ShapeDtypeStruct((b, N_KV, s // TQ, HEAD_DIM, TQ), BF16)
        v_spec = pl.BlockSpec((1, N_KV, tm // TQ, HEAD_DIM, TQ), lambda bi, i: (bi, 0, i, 0, 0))
        extra_shape = [k_shape(K_SEL_AUG), k_shape(K_WIN_AUG), v_shape, v_shape]
        extra_spec = [k_spec(K_SEL_AUG), k_spec(K_WIN_AUG), v_spec, v_spec]
    else:
        q_shape = jax.ShapeDtypeStruct((b, s, NSA_WIDTH), BF16)
        q_spec = row_spec(NSA_WIDTH)
        extra_shape, extra_spec = [], []
    g2 = g_pre.reshape(1, d)
    qkv = pl.pallas_call(
        functools.partial(_proj_qkv_kernel, attn_layouts=attn_layouts),
        grid=grid,
        in_specs=[x_spec, g_spec, w_spec(w_qkv)],
        out_specs=[q_spec] + [row_spec(KV_WIDTH)] * 6 + extra_spec,
        out_shape=[q_shape] + [kv_shape] * 6 + extra_shape,
        compiler_params=_cparams(("parallel", "parallel")),
        name="proj_qkv",
    )(x, g2, w_qkv)
    sa, sb, gt = pl.pallas_call(
        _proj_gate_kernel,
        grid=grid,
        in_specs=[x_spec, g_spec, w_spec(w_gate)],
        out_specs=[row_spec(NSA_WIDTH), row_spec(NSA_WIDTH), row_spec(N_KV * LANES)],
        out_shape=[jax.ShapeDtypeStruct((b, s, NSA_WIDTH), F32)] * 2
        + [jax.ShapeDtypeStruct((b, s, N_KV * LANES), F32)],
        compiler_params=_cparams(("parallel", "parallel")),
        name="proj_gate",
    )(x, g2, w_gate)
    c_in = pl.pallas_call(
        _proj_glu_kernel,
        grid=grid,
        in_specs=[x_spec, g_spec, w_spec(w_glu)],
        out_specs=row_spec(w_glu.shape[1] // 2),
        out_shape=jax.ShapeDtypeStruct((b, s, w_glu.shape[1] // 2), F32),
        compiler_params=_cparams(("parallel", "parallel")),
        name="proj_glu",
    )(x, g2, w_glu)
    return qkv, sa, sb, gt, c_in


def _cmp1_rows_kernel(x_ref, pe_ref, w_ref, pre_ref, pepre_ref):
    w = w_ref[...].reshape(-1, w_ref.shape[-1])
    pre_ref[0] = _dot(x_ref[0].astype(BF16), w)
    pepre_ref[...] = _dot(pe_ref[...], w)


def _cmp1_paged_kernel(pt_ref, cache_ref, perm_ref, pe_ref, w_ref, pre_ref, pepre_ref, buf_ref, sem_ref,
                       *, n_in, n_steps):
    step = pl.program_id(0) * n_steps + pl.program_id(1)
    last = pl.num_programs(0) * n_steps - 1
    slot = step % 2

    def fetch(s, p, into):
        page = pt_ref[s // n_steps, (s % n_steps) * n_in + p]
        return pltpu.make_async_copy(cache_ref.at[page], buf_ref.at[into, p], sem_ref.at[into])

    def wait_slot(into):
        for p in range(n_in):
            pltpu.make_async_copy(cache_ref.at[0], buf_ref.at[into, p], sem_ref.at[into]).wait()

    @pl.when(step == 0)
    def _():
        for p in range(n_in):
            fetch(0, p, 0).start()

    wait_slot(slot)
    perm = perm_ref[...]
    nxt = jnp.minimum(step + 1, last)
    zs = []
    for p in range(n_in):
        zs.append(_dot_nt(perm, buf_ref[slot, p].astype(BF16)))
        fetch(nxt, p, 1 - slot).start()
    rpp = PAGE_SIZE // CMP_STRIDE
    acc = jnp.zeros((n_in * rpp, w_ref.shape[-1]), F32)
    for c in range(CMP_STRIDE):
        xc = jnp.concatenate([z[c * rpp:(c + 1) * rpp] for z in zs], axis=0)
        acc = acc + _dot(xc.astype(BF16), w_ref[c])
    pre_ref[0] = acc
    pepre_ref[...] = _dot(pe_ref[...], w_ref[...].reshape(-1, w_ref.shape[-1]))

    @pl.when(step == last)
    def _():
        wait_slot(1 - slot)


def _cmp2_kernel(pre_ref, pepre_ref, w2_ref, o_ref, *, layout):
    pre = pre_ref[0]
    n_ch = pre.shape[0]
    a = pre[:, :KV_WIDTH]
    b_next = pltpu.roll(pre[:, KV_WIDTH:], n_ch - 1, axis=0)
    pe_bias = pepre_ref[0:1, :KV_WIDTH] + pepre_ref[1:2, KV_WIDTH:]
    z = a + b_next + pe_bias
    hid = 0.5 * z * (1.0 + jnp.tanh(math.sqrt(2.0 / math.pi) * (z + 0.044715 * (z * z * z))))
    out = _dot(hid.astype(BF16), w2_ref[...])
    if layout == "group_rows":
        for g in range(N_KV):
            o_ref[0, g] = out[:, g * HEAD_DIM:(g + 1) * HEAD_DIM].astype(BF16)
    elif layout == "group_cols":
        out_t = out.T.astype(BF16)
        for g in range(N_KV):
            o_ref[0, g] = out_t[g * HEAD_DIM:(g + 1) * HEAD_DIM, :]
    elif layout == "rows":
        o_ref[0] = out.astype(BF16)
    else:
        o_ref[0] = out.T.astype(BF16)


def _cmp_weights(w1, w2, pe):
    eye = jnp.eye(N_KV, dtype=F32)
    halves = w1.reshape(2, CMP_STRIDE, HEAD_DIM, -1)
    hdim = halves.shape[-1]
    wbig = jnp.einsum('acdh,gk->cgdakh', halves, eye)
    wbig = wbig.reshape(CMP_STRIDE, KV_WIDTH, 2 * N_KV * hdim).astype(BF16)
    w2big = jnp.einsum('hd,gk->ghkd', w2, eye).reshape(N_KV * hdim, KV_WIDTH).astype(BF16)
    pe_rows = jnp.broadcast_to(pe.reshape(2, CMP_STRIDE, 1, HEAD_DIM), (2, CMP_STRIDE, N_KV, HEAD_DIM))
    pe8 = jnp.zeros((8, CMP_STRIDE * KV_WIDTH), F32).at[:2].set(pe_rows.reshape(2, -1)).astype(BF16)
    return wbig, w2big, pe8


def _compress(x_view, wts, *, page_table=None, layout):
    wbig, w2big, pe8 = wts
    ncol = wbig.shape[-1]
    kdim = wbig.shape[0] * wbig.shape[1]
    if page_table is None:
        b, n_ch, _ = x_view.shape
        rows = min(n_ch, 256)
        const2 = lambda bi, i: (0, 0)
        pre, pepre = pl.pallas_call(
            _cmp1_rows_kernel, grid=(b, n_ch // rows),
            in_specs=[pl.BlockSpec((1, rows, kdim), lambda bi, i: (bi, i, 0)), pl.BlockSpec(pe8.shape, const2),
                      pl.BlockSpec(wbig.shape, lambda bi, i: (0, 0, 0))],
            out_specs=[pl.BlockSpec((1, rows, ncol), lambda bi, i: (bi, i, 0)), pl.BlockSpec((8, ncol), const2)],
            out_shape=[jax.ShapeDtypeStruct((b, n_ch, ncol), F32), jax.ShapeDtypeStruct((8, ncol), F32)],
            compiler_params=_cparams(("arbitrary", "arbitrary")), name="cmp_stage1")(x_view, pe8, wbig)
    else:
        b, n_pages = page_table.shape
        n_in = CMP_PAGES
        rpp = PAGE_SIZE // CMP_STRIDE
        rows = n_in * rpp
        n_ch = n_pages * rpp
        const2 = lambda bi, i, pt: (0, 0)
        n_steps = n_pages // n_in
        tok = np.arange(PAGE_SIZE)
        perm = (tok[None, :] == (tok[:, None] % rpp) * CMP_STRIDE + tok[:, None] // rpp)
        perm = jnp.asarray(perm, BF16)
        pre, pepre = pl.pallas_call(
            functools.partial(_cmp1_paged_kernel, n_in=n_in, n_steps=n_steps),
            grid_spec=pltpu.PrefetchScalarGridSpec(
                num_scalar_prefetch=1, grid=(b, n_steps),
                in_specs=[pl.BlockSpec(memory_space=pl.ANY), pl.BlockSpec(perm.shape, const2),
                          pl.BlockSpec(pe8.shape, const2), pl.BlockSpec(wbig.shape, lambda bi, i, pt: (0, 0, 0))],
                out_specs=[pl.BlockSpec((1, rows, ncol), lambda bi, i, pt: (bi, i, 0)),
                           pl.BlockSpec((8, ncol), const2)],
                scratch_shapes=[pltpu.VMEM((2, n_in, KV_WIDTH, PAGE_SIZE), F32), pltpu.SemaphoreType.DMA((2,))]),
            out_shape=[jax.ShapeDtypeStruct((b, n_ch, ncol), F32), jax.ShapeDtypeStruct((8, ncol), F32)],
            compiler_params=_cparams(("arbitrary", "arbitrary")), name="cmp_stage1_paged",
        )(page_table, x_view, perm, pe8, wbig)
    o_dims = {"group_rows": (N_KV, n_ch, HEAD_DIM), "group_cols": (N_KV, HEAD_DIM, n_ch),
              "rows": (n_ch, KV_WIDTH), "cols": (KV_WIDTH, n_ch)}[layout]
    return pl.pallas_call(
        functools.partial(_cmp2_kernel, layout=layout),
        grid=(b,),
        in_specs=[pl.BlockSpec((1, n_ch, ncol), lambda bi: (bi, 0, 0)),
                  pl.BlockSpec((8, ncol), lambda bi: (0, 0)),
                  pl.BlockSpec(w2big.shape, lambda bi: (0, 0))],
        out_specs=pl.BlockSpec((1,) + o_dims, lambda bi: (bi,) + (0,) * len(o_dims)),
        out_shape=jax.ShapeDtypeStruct((b,) + o_dims, BF16),
        compiler_params=_cparams(("parallel",)), name="cmp_stage2",
    )(pre, pepre, w2big)


def _t5_bucket(dist):
    dist = np.maximum(np.asarray(dist, np.int64), 0)
    max_exact = N_BUCKETS // 2
    d32 = np.maximum(dist, 1).astype(np.float32)
    large = max_exact + (np.log(d32 / np.float32(max_exact)) / np.float32(math.log(MAX_DISTANCE / max_exact))
                         * np.float32(N_BUCKETS - max_exact)).astype(np.int32)
    large = np.minimum(large, N_BUCKETS - 1)
    return np.where(dist < max_exact, dist, large).astype(np.int32)


def _bias_lookup(table, dist):
    bucket = jnp.asarray(_t5_bucket(dist).reshape(-1, 1))
    onehot = (bucket == jnp.arange(N_BUCKETS, dtype=jnp.int32)[None, :]).astype(F32)
    out = jnp.dot(onehot, table, precision=lax.Precision.HIGHEST)
    return out.reshape(tuple(np.shape(dist)) + (table.shape[1],))


def _top_block_round(chosen, work, jrow):
    m = jnp.max(work, axis=0, keepdims=True)
    idx = jnp.min(jnp.where(work == m, jrow, 1 << 20), axis=0, keepdims=True)
    pick = jrow == idx
    return jnp.where(pick, 1.0, chosen), jnp.where(pick, PICKED, work)


def _select_top_blocks(score, jrow, k_sel):
    state = (jnp.zeros(score.shape, F32), score)
    for _ in range(k_sel):
        state = _top_block_round(*state, jrow)
    return state[0]


def _nsa_prompt_kernel(q_ref, gt_ref, sa_ref, kcb_ref, vcbt_ref, ks_ref, vst_ref, kw_ref, vwt_ref,
                       pcd_ref, far_ref, far16_ref, tz0_ref, tz1_ref, covt_ref, o_ref,
                       lc_ref, qs_ref, qw_ref, m_ref, l_ref, acc_ref, *, n_slc):
    i = pl.program_id(2)
    nl = HPG * TQ
    ncp = kcb_ref.shape[2]
    q_t = jnp.concatenate([q_ref[0, 0, r] for r in range(HPG)], axis=1)
    lane_t = lax.broadcasted_iota(jnp.int32, (1, nl), 1) & (TQ - 1)
    qpos = i * TQ + lane_t
    far = far_ref[0, 0:1, :]
    qs_ref[0:HEAD_DIM, :] = q_t
    qs_ref[HEAD_DIM + LANES:, :] = far16_ref[0]
    qw_ref[0:HEAD_DIM, :] = q_t
    qw_ref[HEAD_DIM:, :] = far16_ref[0]

    lc_ref[0:CPT, :] = jnp.zeros((CPT, nl), F32)
    lc_ref[CPT:CPT + ncp, :] = _dot(kcb_ref[0, 0], q_t) + far
    near = pl.ds(pl.multiple_of(i * CPT, CPT), 2 * CPT)
    lc_ref[near, :] = lc_ref[near, :] + pcd_ref[0]
    lc = lc_ref[CPT:CPT + ncp, :]
    cend = lax.broadcasted_iota(jnp.int32, (ncp, 1), 0) * CMP_STRIDE + (CMP_BLOCK - 1)
    valid_c = cend <= qpos
    lm = jnp.where(valid_c, lc, NEG)
    mc = jnp.max(lm, axis=0, keepdims=True)
    ec = jnp.where(valid_c, jnp.exp2(lm - mc), 0.0)
    den = jnp.sum(ec, axis=0, keepdims=True)
    p_c = ec / jnp.where(den > 0, den, 1.0)
    o_c = _dot(vcbt_ref[0, 0], p_c.astype(BF16))

    psum = p_c[:, 0:TQ]
    for r in range(1, HPG):
        psum = psum + p_c[:, r * TQ:(r + 1) * TQ]
    hi, lo = _split_hi_lo(psum)
    imp_t = _dot(covt_ref[...], hi) + _dot(covt_ref[...], lo)
    jrow = lax.broadcasted_iota(jnp.int32, (LANES, TQ), 0)
    qpos_t = qpos[:, 0:TQ]
    cur = qpos_t // SEL_BLOCK
    causal = (jrow * SEL_BLOCK <= qpos_t) & (jrow < n_slc)
    forced = (jrow == 0) | (jrow == cur) | (jrow == cur - 1)
    score = jnp.where(forced, FORCE_SCORE, jnp.where(causal, imp_t, NEG))
    k_sel = min(N_SEL, n_slc)
    topk = [(jnp.zeros(score.shape, F32), score), 0]

    def topk_rounds(n):
        for _ in range(min(n, k_sel - topk[1])):
            topk[0] = _top_block_round(*topk[0], jrow)
            topk[1] += 1

    key_u = lax.broadcasted_iota(jnp.int32, (TQ, 1), 0)
    causal_diag = key_u <= lane_t

    def run_pieces(specs, between=lambda: None):
        def logits(t):
            k_ref, _, qx_ref, kt, bias, mask = specs[t]
            s = _dot(k_ref[0, 0, pl.ds(pl.multiple_of(kt * TQ, TQ), TQ), :], qx_ref[...])
            if bias is not None:
                s = s + bias
            return s if mask is None else jnp.where(mask, s, NEG)

        n = len(specs)
        s = {t: logits(t) for t in range(min(PIPE_AHEAD, n))}
        between()
        pieces = []
        for t in range(n):
            m = jnp.max(s[t], axis=0, keepdims=True)
            p = jnp.exp2(s.pop(t) - m)
            if t + PIPE_AHEAD < n:
                s[t + PIPE_AHEAD] = logits(t + PIPE_AHEAD)
            between()
            _, vt_ref, _, kt, _, _ = specs[t]
            pieces.append((_dot(vt_ref[0, 0, kt], p.astype(BF16)), m, jnp.sum(p, axis=0, keepdims=True)))
            between()
        return pieces

    def merge(pieces):
        m_new = functools.reduce(jnp.maximum, [m for _, m, _ in pieces])
        scales = [jnp.exp2(m - m_new) for _, m, _ in pieces]
        acc = functools.reduce(jnp.add, [a * acc for a, (acc, _, _) in zip(scales, pieces)])
        l = functools.reduce(jnp.add, [a * l for a, (_, _, l) in zip(scales, pieces)])
        return acc, m_new, l

    prev = jnp.maximum(i - 1, 0)
    prev2 = jnp.maximum(i - 2, 0)
    tz1 = tz1_ref[0] + jnp.where(i >= 1, 0.0, NEG)
    tz0 = tz0_ref[0]
    n_gaps = 1 + 2 * 3
    acc_w, _, l_w = merge(run_pieces([(kw_ref, vwt_ref, qw_ref, prev2, None, (key_u > lane_t) & (i >= 2)),
                                      (kw_ref, vwt_ref, qw_ref, prev, tz1, None),
                                      (kw_ref, vwt_ref, qw_ref, i, tz0, causal_diag)],
                                     between=lambda: topk_rounds(-(-k_sel // n_gaps))))
    o_w = acc_w / l_w
    topk_rounds(k_sel)
    chosen = topk[0][0]
    sel_add = jnp.where(causal & (chosen > 0.5), 0.0, NEG).astype(BF16)
    qs_ref[HEAD_DIM:HEAD_DIM + LANES, :] = jnp.concatenate([sel_add] * HPG, axis=1)

    m_ref[...] = jnp.full(m_ref.shape, NEG, F32)
    l_ref[...] = jnp.zeros(l_ref.shape, F32)
    acc_ref[...] = jnp.zeros(acc_ref.shape, F32)

    def state():
        return acc_ref[...], m_ref[...], l_ref[...]

    def merge_far(kt0, n):
        acc, m, l = merge([state()] + run_pieces([(ks_ref, vst_ref, qs_ref, kt0 + t, None, None) for t in range(n)]))
        acc_ref[...] = acc
        m_ref[...] = m
        l_ref[...] = l

    n_far = jnp.maximum(i - 1, 0)

    def far_group(jg, carry):
        merge_far(FAR_GROUP * jg, FAR_GROUP)
        return carry

    lax.fori_loop(0, n_far // FAR_GROUP, far_group, 0)
    rest = n_far % FAR_GROUP
    size = FAR_GROUP // 2
    while size >= 1:
        @pl.when((rest & size) != 0)
        def _(size=size):
            merge_far(n_far - (rest & (2 * size - 1)), size)
        size //= 2

    acc_s, _, l_s = merge([state()] + run_pieces([(ks_ref, vst_ref, qs_ref, prev, tz1, None),
                                                  (ks_ref, vst_ref, qs_ref, i, tz0, causal_diag)]))
    o_s = acc_s / l_s

    g_t = gt_ref[0].T

    def gate_row(branch):
        return jnp.concatenate([g_t[branch * HPG + r:branch * HPG + r + 1, :] for r in range(HPG)], axis=1)

    o_t = gate_row(0) * o_c + gate_row(1) * o_s + gate_row(2) * o_w
    o_rd = jnp.concatenate([o_t[:, r * TQ:(r + 1) * TQ] for r in range(HPG)], axis=0)
    o_ref[0] = (o_rd.T * sa_ref[0]).astype(BF16)


def _nsa_prompt(q_t, gt, sa, kcb, vcb_t, ks, vs_t, kw, vw_t, rel_bias):
    b, _, _, _, s = q_t.shape
    assert WINDOW == 2 * TQ and s % TQ == 0 and TQ >= MAX_DISTANCE
    nq = s // TQ
    ncp = s // CMP_STRIDE
    n_cmp = ncp - 1
    n_slc = -(-s // SEL_BLOCK)
    assert n_slc <= LANES and ncp >= 2 * CPT
    nl = HPG * TQ
    table = rel_bias.astype(F32)
    uu = np.arange(TQ)[:, None]
    tt = np.arange(TQ)[None, :]

    def per_group(tab):
        rows = tab.shape[1]
        return jnp.transpose(tab.reshape(N_KV, HPG, rows, TQ), (0, 2, 1, 3)).reshape(N_KV, rows, nl)

    table2 = table * LOG2E
    heads_first = lambda dist: jnp.moveaxis(_bias_lookup(table2, dist), -1, 0)
    far_h = _bias_lookup(table2, np.array([MAX_DISTANCE]))[0]
    far_b = far_h[:, None, None]
    tz0 = per_group(heads_first(tt - uu) - far_b)
    tz1 = per_group(heads_first(TQ + tt - uu) - far_b)
    far = per_group(jnp.broadcast_to(far_b, (N_HEADS, 8, TQ)))
    far_hi = far_h.astype(BF16)
    far_lo = (far_h - far_hi.astype(F32)).astype(BF16)
    far16 = jnp.zeros((N_HEADS, AUG_PAD, TQ), BF16).at[:, 0].set(far_hi[:, None]).at[:, 1].set(far_lo[:, None])
    far16 = per_group(far16)
    e = np.arange(2 * CPT)[:, None] - CPT
    pcd = per_group(heads_first(tt - CMP_STRIDE * e - (CMP_BLOCK - 1)) - far_b)
    c = np.arange(ncp)[None, :]
    j = np.arange(LANES)[:, None]
    cov_t = ((c * CMP_STRIDE < (j + 1) * SEL_BLOCK) & (c * CMP_STRIDE + CMP_BLOCK > j * SEL_BLOCK)
             & (c < n_cmp) & (j < n_slc))
    cov_t = jnp.asarray(cov_t, BF16)

    per_g = lambda rows: pl.BlockSpec((1, rows, nl), lambda bi, g, i: (g, 0, 0))
    return pl.pallas_call(
        functools.partial(_nsa_prompt_kernel, n_slc=n_slc),
        grid=(b, N_KV, nq),
        in_specs=[
            pl.BlockSpec((1, 1, HPG, HEAD_DIM, TQ), lambda bi, g, i: (bi, g, 0, 0, i)),
            pl.BlockSpec((1, TQ, LANES), lambda bi, g, i: (bi, i, g)),
            pl.BlockSpec((1, TQ, KV_WIDTH), lambda bi, g, i: (bi, i, g)),
            pl.BlockSpec((1, 1, ncp, HEAD_DIM), lambda bi, g, i: (bi, g, 0, 0)),
            pl.BlockSpec((1, 1, HEAD_DIM, ncp), lambda bi, g, i: (bi, g, 0, 0)),
            pl.BlockSpec((1, 1, s, K_SEL_AUG), lambda bi, g, i: (bi, g, 0, 0)),
            pl.BlockSpec((1, 1, nq, HEAD_DIM, TQ), lambda bi, g, i: (bi, g, 0, 0, 0)),
            pl.BlockSpec((1, 1, s, K_WIN_AUG), lambda bi, g, i: (bi, g, 0, 0)),
            pl.BlockSpec((1, 1, nq, HEAD_DIM, TQ), lambda bi, g, i: (bi, g, 0, 0, 0)),
            per_g(2 * CPT), per_g(8), per_g(AUG_PAD), per_g(TQ), per_g(TQ),
            pl.BlockSpec(cov_t.shape, lambda bi, g, i: (0, 0)),
        ],
        out_specs=pl.BlockSpec((1, TQ, KV_WIDTH), lambda bi, g, i: (bi, i, g)),
        out_shape=jax.ShapeDtypeStruct((b, s, NSA_WIDTH), BF16),
        scratch_shapes=[pltpu.VMEM((CPT + ncp, nl), F32), pltpu.VMEM((K_SEL_AUG, nl), BF16),
                        pltpu.VMEM((K_WIN_AUG, nl), BF16),
                        pltpu.VMEM((1, nl), F32), pltpu.VMEM((1, nl), F32), pltpu.VMEM((HEAD_DIM, nl), F32)],
        compiler_params=_cparams(("parallel", "parallel", "arbitrary")),
        name="nsa_prompt",
    )(q_t, gt, sa, kcb, vcb_t, ks, vs_t, kw, vw_t, pcd, far, far16, tz0, tz1, cov_t)


def _nsa_sample_kernel(*refs, n_pg, n_chunks, n_slc, past, t_new):
    pt_ref = refs[0]
    (qbd_ref, kcbt_ref, vcb_ref, bc_ref, cov_ref, rmat_ref, ck_ref, cv_ref) = refs[1:9]
    (bsl_ref, ksn_ref, vsn_ref, bsn_ref, kwc_ref, vwc_ref, kwn_ref, vwn_ref, bw_ref, gt_ref, e4_ref,
     o_ref, sel_ref, m_ref, l_ref, acc_ref, oc_ref, kbuf_ref, vbuf_ref, sem_ref) = refs[9:]
    j = pl.program_id(1)
    step = pl.program_id(0) * n_chunks + j
    last = pl.num_programs(0) * n_chunks - 1
    slot = step % 2
    caches = ((ck_ref, kbuf_ref), (cv_ref, vbuf_ref))

    def fetch(kv, s, p, into):
        page = pt_ref[s // n_chunks, (s % n_chunks) * n_pg + p]
        cache_ref, buf_ref = caches[kv]
        return pltpu.make_async_copy(cache_ref.at[page], buf_ref.at[into, p], sem_ref.at[kv, into])

    def wait_slot(into):
        for kv, (cache_ref, buf_ref) in enumerate(caches):
            for p in range(n_pg):
                pltpu.make_async_copy(cache_ref.at[0], buf_ref.at[into, p], sem_ref.at[kv, into]).wait()

    @pl.when(step == 0)
    def _():
        for kv in range(2):
            for p in range(n_pg):
                fetch(kv, 0, p, 0).start()

    wait_slot(slot)
    nxt = jnp.minimum(step + 1, last)
    qbd = qbd_ref[0]
    nsp = cov_ref.shape[1]
    n_win = kwc_ref.shape[2]

    def softmax_rows(s):
        m = jnp.max(s, axis=-1, keepdims=True)
        e = jnp.where(s > 0.5 * NEG, jnp.exp(s - m), 0.0)
        den = jnp.sum(e, axis=-1, keepdims=True)
        return e / jnp.where(den > 0, den, 1.0)

    def flash_step(s, v, v_is_transposed):
        m_old = m_ref[...]
        m_new = jnp.maximum(m_old, jnp.max(s, axis=-1, keepdims=True))
        alpha = jnp.exp(m_old - m_new)
        p = jnp.exp(s - m_new)
        l_ref[...] = alpha * l_ref[...] + jnp.sum(p, axis=-1, keepdims=True)
        pv = _dot_nt(p.astype(BF16), v) if v_is_transposed else _dot(p.astype(BF16), v)
        acc_ref[...] = alpha * acc_ref[...] + pv
        m_ref[...] = m_new

    @pl.when(j == 0)
    def _():
        p_c = softmax_rows(_dot(qbd, kcbt_ref[0]) + bc_ref[...])
        oc_ref[...] = _dot(p_c.astype(BF16), vcb_ref[0])
        hi, lo = _split_hi_lo(p_c)
        psum = _dot(rmat_ref[...], hi) + _dot(rmat_ref[...], lo)
        hi, lo = _split_hi_lo(psum)
        imp_t = (_dot(hi, cov_ref[...]) + _dot(lo, cov_ref[...])).T
        jrow = lax.broadcasted_iota(jnp.int32, (nsp, LANES), 0)
        lane = lax.broadcasted_iota(jnp.int32, (1, LANES), 1)
        qpos = past + lane % t_new
        cur = qpos // SEL_BLOCK
        causal = (jrow * SEL_BLOCK <= qpos) & (jrow < n_slc)
        forced = (jrow == 0) | (jrow == cur) | (jrow == cur - 1)
        score = jnp.where(forced, FORCE_SCORE, jnp.where(causal, imp_t, NEG))
        chosen = _select_top_blocks(score, jrow, min(N_SEL, n_slc))
        sel = jnp.where(causal & (lane < N_HEADS * t_new) & (chosen > 0.5), 0.0, -1.0).T
        for k in range(nsp // LANES):
            sel_ref[k] = sel[:, k * LANES:(k + 1) * LANES]
        m_ref[...] = jnp.full(m_ref.shape, NEG, F32)
        l_ref[...] = jnp.zeros(l_ref.shape, F32)
        acc_ref[...] = jnp.zeros(acc_ref.shape, F32)

    far = bsl_ref[:, PAGE_SIZE:2 * PAGE_SIZE]
    last_page = jnp.where(j == n_chunks - 1, bsl_ref[:, 0:PAGE_SIZE], far)
    chunks_per_tile = LANES // (n_pg * (PAGE_SIZE // SEL_BLOCK))
    sel_tile = sel_ref[j // chunks_per_tile].astype(BF16)
    pp = SAMPLE_PIECE_PAGES
    n_pieces = n_pg // pp

    def logits(t):
        pages = range(t * pp, (t + 1) * pp)
        k_t = jnp.concatenate([kbuf_ref[slot, p] for p in pages], axis=1).astype(BF16)
        for p in pages:
            fetch(0, nxt, p, 1 - slot).start()
        bias = jnp.concatenate([far] * (pp - 1) + [last_page if t == n_pieces - 1 else far], axis=1)
        mask_add = _dot(sel_tile, e4_ref[j % chunks_per_tile, :, t * pp * PAGE_SIZE:(t + 1) * pp * PAGE_SIZE])
        return _dot(qbd, k_t) + bias + mask_add

    s = {t: logits(t) for t in range(min(PIPE_AHEAD, n_pieces))}
    pieces = []
    for t in range(n_pieces):
        m = jnp.max(s[t], axis=-1, keepdims=True)
        p = jnp.exp(s.pop(t) - m)
        if t + PIPE_AHEAD < n_pieces:
            s[t + PIPE_AHEAD] = logits(t + PIPE_AHEAD)
        pages = range(t * pp, (t + 1) * pp)
        v_t = jnp.concatenate([vbuf_ref[slot, pg] for pg in pages], axis=1).astype(BF16)
        for pg in pages:
            fetch(1, nxt, pg, 1 - slot).start()
        pieces.append((_dot_nt(p.astype(BF16), v_t), m, jnp.sum(p, axis=-1, keepdims=True)))
    m_old = m_ref[...]
    m_new = functools.reduce(jnp.maximum, [m_old] + [m for _, m, _ in pieces])
    scales = [jnp.exp(m - m_new) for _, m, _ in pieces]
    a_old = jnp.exp(m_old - m_new)
    l_ref[...] = a_old * l_ref[...] + functools.reduce(jnp.add, [a * l for a, (_, _, l) in zip(scales, pieces)])
    acc_ref[...] = a_old * acc_ref[...] + functools.reduce(jnp.add, [a * acc for a, (acc, _, _) in zip(scales, pieces)])
    m_ref[...] = m_new

    @pl.when(j == n_chunks - 1)
    def _():
        sn = _dot_nt(qbd, ksn_ref[0].astype(BF16)) + bsn_ref[...]
        blk = n_slc - 1
        seln = sel_ref[blk // LANES][:, blk % LANES:blk % LANES + 1]
        flash_step(sn + seln * (-NEG), vsn_ref[0].astype(BF16), False)
        l = l_ref[...]
        o_s = acc_ref[...] / jnp.where(l > 0, l, 1.0)
        sw = jnp.concatenate([_dot(qbd, kwc_ref[0].astype(BF16)), _dot_nt(qbd, kwn_ref[0].astype(BF16))], axis=1)
        p_w = softmax_rows(sw + bw_ref[...]).astype(BF16)
        o_w = _dot_nt(p_w[:, :n_win], vwc_ref[0].astype(BF16)) + _dot(p_w[:, n_win:], vwn_ref[0].astype(BF16))
        gt = gt_ref[0]
        o_ref[0] = gt[:, 0:1] * oc_ref[...] + gt[:, 1:2] * o_s + gt[:, 2:3] * o_w

    @pl.when(step == last)
    def _():
        wait_slot(1 - slot)


def _nsa_sample(q, gates, kcb_t, vcb, cache_k_slc, cache_v_slc, page_table, ks_new, vs_new,
                kw_cache, vw_cache, kw_new, vw_new, rel_bias, *, past):
    db, t_new, _ = q.shape
    n_pages = page_table.shape[1]
    assert past == n_pages * PAGE_SIZE and past % SEL_BLOCK == 0 and t_new <= SEL_BLOCK
    assert PAGE_SIZE >= MAX_DISTANCE
    n_cp = kcb_t.shape[2]
    n_slc = -(-(past + t_new) // SEL_BLOCK)
    nsp = -(-n_slc // LANES) * LANES
    n_pg = SAMPLE_PAGES
    n_chunks = n_pages // n_pg
    blocks_per_chunk = n_pg * (PAGE_SIZE // SEL_BLOCK)
    assert LANES % blocks_per_chunk == 0
    n_win = kw_cache.shape[2]
    nl = N_HEADS * t_new
    assert nl <= LANES
    row = np.arange(LANES)
    row_ok = (row < nl)[:, None]
    row_g = np.where(row < nl, row // (HPG * t_new), 0)
    row_t = (row % t_new)[:, None]
    q5 = q.reshape(db, t_new, N_KV, HPG, HEAD_DIM)
    qbd = jnp.einsum('btgrd,gk->bgrtkd', q5.astype(F32), jnp.eye(N_KV, dtype=F32)).reshape(db, nl, KV_WIDTH)
    qbd = jnp.pad(qbd, ((0, 0), (0, LANES - nl), (0, 0))).astype(BF16)

    table = rel_bias.astype(F32)

    def bias_tab(dist, valid):
        vals = jnp.moveaxis(_bias_lookup(table, dist), -1, 0)
        vals = jnp.where(jnp.asarray(valid)[None], vals, NEG).reshape(nl, -1)
        return jnp.pad(vals, ((0, LANES - nl), (0, 0)), constant_values=NEG)

    tq = np.arange(t_new)[:, None]
    qpos = past + tq
    cblk = np.arange(n_cp)[None, :]
    dist_c = qpos - (cblk * CMP_STRIDE + CMP_BLOCK - 1)
    bc = bias_tab(dist_c, (dist_c >= 0) & (cblk < n_cp - 1))
    kpos = past - PAGE_SIZE + np.arange(PAGE_SIZE)[None, :]
    all_ok = np.ones((t_new, PAGE_SIZE), bool)
    bsl = jnp.concatenate([bias_tab(qpos - kpos, all_ok),
                           bias_tab(np.full((t_new, PAGE_SIZE), MAX_DISTANCE), all_ok)], axis=1)
    u = np.arange(PAGE_SIZE)[None, :]
    new_ok = (u <= tq) & (u < t_new)
    bsn = bias_tab(tq - u, new_ok)
    wpos = past - n_win + np.arange(n_win)[None, :]
    dist_w = qpos - wpos
    bw = jnp.concatenate([bias_tab(dist_w, (dist_w >= 0) & (dist_w < WINDOW) & (wpos >= 0)),
                          bias_tab(tq - u, new_ok & (tq - u < WINDOW))], axis=1)
    c = np.arange(n_cp)[:, None]
    jb = np.arange(nsp)[None, :]
    cov = ((c * CMP_STRIDE < (jb + 1) * SEL_BLOCK) & (c * CMP_STRIDE + CMP_BLOCK > jb * SEL_BLOCK)
           & (c < n_cp - 1) & (jb < n_slc))
    cov = jnp.asarray(cov, BF16)
    same = (row_g[:, None] == row_g[None, :]) & (row_t == row_t.T) & row_ok & row_ok.T
    rmat = jnp.asarray(same, BF16)
    kk = np.arange(n_pg * PAGE_SIZE)[None, None, :] // SEL_BLOCK
    e4 = np.arange(LANES)[None, :, None] == (np.arange(LANES // blocks_per_chunk)[:, None, None] * blocks_per_chunk + kk)
    e4 = jnp.asarray(np.where(e4, -NEG, 0.0), BF16)
    g5 = gates.reshape(db, t_new, N_KV, LANES)[..., :3 * HPG].reshape(db, t_new, N_KV, 3, HPG)
    gcol = jnp.transpose(g5, (0, 2, 4, 1, 3)).reshape(db, nl, 3)
    gcol = jnp.pad(gcol, ((0, 0), (0, LANES - nl), (0, 5)))

    def pad_new(a):
        return jnp.pad(a, ((0, 0), (0, PAGE_SIZE - t_new), (0, 0)))

    ksn, vsn, kwn, vwn = (pad_new(a) for a in (ks_new, vs_new, kw_new, vw_new))

    per_b = lambda shape: pl.BlockSpec((1,) + shape, lambda b, jc, pt: (b, 0, 0))
    full = lambda a: pl.BlockSpec(a.shape, lambda b, jc, pt: (0,) * a.ndim)
    in_specs = ([per_b((LANES, KV_WIDTH)), per_b((KV_WIDTH, n_cp)), per_b((n_cp, KV_WIDTH)),
                 full(bc), full(cov), full(rmat)]
                + [pl.BlockSpec(memory_space=pl.ANY)] * 2
                + [full(bsl), per_b((PAGE_SIZE, KV_WIDTH)), per_b((PAGE_SIZE, KV_WIDTH)), full(bsn),
                   per_b((KV_WIDTH, n_win)), per_b((KV_WIDTH, n_win)),
                   per_b((PAGE_SIZE, KV_WIDTH)), per_b((PAGE_SIZE, KV_WIDTH)), full(bw), per_b((LANES, 8)), full(e4)])
    o = pl.pallas_call(
        functools.partial(_nsa_sample_kernel, n_pg=n_pg, n_chunks=n_chunks, n_slc=n_slc, past=past, t_new=t_new),
        grid_spec=pltpu.PrefetchScalarGridSpec(
            num_scalar_prefetch=1, grid=(db, n_chunks), in_specs=in_specs,
            out_specs=pl.BlockSpec((1, LANES, KV_WIDTH), lambda b, jc, pt: (b, 0, 0)),
            scratch_shapes=[pltpu.VMEM((nsp // LANES, LANES, LANES), F32), pltpu.VMEM((LANES, 1), F32),
                            pltpu.VMEM((LANES, 1), F32), pltpu.VMEM((LANES, KV_WIDTH), F32),
                            pltpu.VMEM((LANES, KV_WIDTH), F32),
                            pltpu.VMEM((2, n_pg, KV_WIDTH, PAGE_SIZE), F32),
                            pltpu.VMEM((2, n_pg, KV_WIDTH, PAGE_SIZE), F32), pltpu.SemaphoreType.DMA((2, 2))]),
        out_shape=jax.ShapeDtypeStruct((db, LANES, KV_WIDTH), F32),
        compiler_params=_cparams(("arbitrary", "arbitrary")),
        name="nsa_sample",
    )(page_table, qbd, kcb_t, vcb, bc, cov, rmat, cache_k_slc, cache_v_slc,
      bsl, ksn, vsn, bsn, kw_cache, vw_cache, kwn, vwn, bw, gcol, e4)
    o6 = o[:, :nl].reshape(db, N_KV, HPG, t_new, N_KV, HEAD_DIM)
    o_diag = jnp.stack([o6[:, g, :, :, g] for g in range(N_KV)], axis=1)
    return jnp.transpose(o_diag, (0, 3, 1, 2, 4)).reshape(db, t_new, NSA_WIDTH)


def _conv_tail(y, cb_ref, lg_ref, lb_ref, wpw_ref, bpw_ref, sb):
    y = y + cb_ref[...]
    mu = jnp.mean(y, axis=-1, keepdims=True)
    yc = y - mu
    var = jnp.mean(yc * yc, axis=-1, keepdims=True)
    yn = yc * lax.rsqrt(var + LN_EPS) * lg_ref[...] + lb_ref[...]
    act = yn * _sigmoid(yn)
    return ((_dot(act.astype(BF16), wpw_ref[...]) + bpw_ref[...]) * sb).astype(BF16)


def _conv_prompt_kernel(c_ref, halo_ref, init_ref, cw_ref, cb_ref, lg_ref, lb_ref, wpw_ref, bpw_ref, sb_ref,
                        o_ref, full_ref, sh_ref, y_ref, *, ts):
    j = pl.program_id(1)
    full_ref[CONV_HALO:CONV_HALO + ts, :] = c_ref[0]

    @pl.when(j == 0)
    def _():
        full_ref[0:CONV_HALO, :] = init_ref[0]

    @pl.when(j > 0)
    def _():
        full_ref[0:CONV_HALO, :] = halo_ref[0]

    first = CONV_HALO - (CONV_WIDTH - 1)
    span = sh_ref.shape[1]
    for sft in range(1, 8):
        sh_ref[sft - 1] = full_ref[sft:sft + span, :]
    rb = 64
    ch = full_ref.shape[1]
    for c0 in range(0, ch, LANES):
        for r0 in range(0, ts, rb):
            acc = jnp.zeros((rb, LANES), F32)
            for w in range(CONV_WIDTH):
                sft = (first + w) % 8
                base = r0 + first + w - sft
                if sft == 0:
                    x = full_ref[base:base + rb, c0:c0 + LANES]
                else:
                    x = sh_ref[sft - 1, base:base + rb, c0:c0 + LANES]
                acc = acc + x * cw_ref[w:w + 1, c0:c0 + LANES]
            y_ref[r0:r0 + rb, c0:c0 + LANES] = acc
    o_ref[0] = _conv_tail(y_ref[...], cb_ref, lg_ref, lb_ref, wpw_ref, bpw_ref, sb_ref[0])


def _conv_sample_kernel(c_ref, st_ref, cw_ref, cb_ref, lg_ref, lb_ref, wpw_ref, bpw_ref, sb_ref,
                        o_ref, full_ref, y_ref):
    nb, t_new, _ = c_ref.shape
    n_st = st_ref.shape[1]
    full_ref[:, 0:n_st, :] = st_ref[...]
    full_ref[:, n_st:n_st + t_new, :] = c_ref[...]
    first = n_st - (CONV_WIDTH - 1)
    for b in range(nb):
        acc = jnp.zeros((t_new, full_ref.shape[2]), F32)
        for w in range(CONV_WIDTH):
            acc = acc + full_ref[b, first + w:first + w + t_new, :] * cw_ref[w:w + 1, :]
        y_ref[b * t_new:(b + 1) * t_new, :] = acc
    o_ref[...] = _conv_tail(y_ref[...], cb_ref, lg_ref, lb_ref, wpw_ref, bpw_ref, sb_ref[...])


def _conv_params(conv_w, conv_b, ln_g, ln_b, w_pw, b_pw):
    ch = conv_w.shape[1]
    cw = jnp.pad(conv_w, ((0, 32 - CONV_WIDTH), (0, 0)))
    return (cw, conv_b.reshape(1, ch), ln_g.reshape(1, ch), ln_b.reshape(1, ch), w_pw.astype(BF16),
            b_pw.reshape(1, ch))


def _conv_prompt(c_in, init, params, sb, *, ts):
    b, s, ch = c_in.shape
    hb = ts // CONV_HALO
    const = lambda a: pl.BlockSpec(a.shape, lambda bi, j: (0,) * a.ndim)
    return pl.pallas_call(
        functools.partial(_conv_prompt_kernel, ts=ts),
        grid=(b, s // ts),
        in_specs=[pl.BlockSpec((1, ts, ch), lambda bi, j: (bi, j, 0)),
                  pl.BlockSpec((1, CONV_HALO, ch), lambda bi, j: (bi, jnp.maximum(j * hb - 1, 0), 0)),
                  pl.BlockSpec((1, CONV_HALO, ch), lambda bi, j: (bi, 0, 0))]
        + [const(a) for a in params]
        + [pl.BlockSpec((1, ts, ch), lambda bi, j: (bi, j, 0))],
        out_specs=pl.BlockSpec((1, ts, ch), lambda bi, j: (bi, j, 0)),
        out_shape=jax.ShapeDtypeStruct((b, s, ch), BF16),
        scratch_shapes=[pltpu.VMEM((CONV_HALO + ts, ch), F32), pltpu.VMEM((7, CONV_HALO + ts - 8, ch), F32),
                        pltpu.VMEM((ts, ch), F32)],
        compiler_params=_cparams(("parallel", "arbitrary")),
        name="conv_prompt",
    )(c_in, c_in, init, *params, sb)


def _conv_sample(c_in, state, params, sb):
    db, t_new, ch = c_in.shape
    n_st = state.shape[1]
    rows_pad = -(-(n_st + t_new) // 8) * 8
    return pl.pallas_call(
        _conv_sample_kernel,
        out_shape=jax.ShapeDtypeStruct((db * t_new, ch), BF16),
        scratch_shapes=[pltpu.VMEM((db, rows_pad, ch), F32), pltpu.VMEM((db * t_new, ch), F32)],
        compiler_params=pltpu.CompilerParams(vmem_limit_bytes=VMEM_LIMIT),
        name="conv_sample",
    )(c_in, state, *params, sb)


def _out_kernel(x_ref, ma_ref, mb_ref, *rest, gated):
    if gated:
        sa_ref, wa_ref, wb_ref, gp_ref, y_ref = rest
        ma = (ma_ref[0] * sa_ref[0]).astype(BF16)
    else:
        wa_ref, wb_ref, gp_ref, y_ref = rest
        ma = ma_ref[0]
    z = _dot(ma, wa_ref[...]) + _dot(mb_ref[0], wb_ref[...])
    ms = jnp.mean(z * z, axis=-1, keepdims=True)
    y_ref[0] = x_ref[0] + z * lax.rsqrt(ms + RMS_EPS) * gp_ref[...]


def _out_proj(x, ma, mb, sa, w_out, g_post, *, tm):
    b, s, d = x.shape
    na = ma.shape[-1]
    wa = w_out[:na].astype(BF16)
    wb = w_out[na:].astype(BF16)
    row = lambda width: pl.BlockSpec((1, tm, width), lambda bi, i: (bi, i, 0))
    const = lambda a: pl.BlockSpec(a.shape, lambda bi, i: (0, 0))
    gp = g_post.reshape(1, d)
    gate_in, gate_spec = ([sa], [row(na)]) if sa is not None else ([], [])
    return pl.pallas_call(
        functools.partial(_out_kernel, gated=sa is not None),
        grid=(b, s // tm),
        in_specs=[row(d), row(na), row(mb.shape[-1])] + gate_spec + [const(wa), const(wb), const(gp)],
        out_specs=row(d),
        out_shape=jax.ShapeDtypeStruct((b, s, d), F32),
        compiler_params=_cparams(("parallel", "parallel")),
        name="out_proj",
    )(x, ma, mb, *gate_in, wa, wb, gp)


def _split_w_in(w_in):
    d = w_in.shape[0]
    c0 = NSA_WIDTH + 6 * KV_WIDTH
    n_gate = 3 * N_HEADS
    conv_ch = (w_in.shape[1] - c0 - n_gate - NSA_WIDTH) // 3
    w_qkv = w_in[:, :c0].astype(BF16)
    wg = w_in[:, c0:c0 + n_gate].reshape(d, N_KV, HPG, 3)
    wg = jnp.transpose(wg, (0, 1, 3, 2)).reshape(d, N_KV, 3 * HPG)
    wg = jnp.pad(wg, ((0, 0), (0, 0), (0, LANES - 3 * HPG))).reshape(d, N_KV * LANES)
    z_a = w_in[:, c0 + n_gate:c0 + n_gate + NSA_WIDTH]
    glu0 = c0 + n_gate + NSA_WIDTH
    w_glu = w_in[:, glu0:glu0 + 2 * conv_ch].astype(BF16)
    z_b = w_in[:, glu0 + 2 * conv_ch:]
    assert conv_ch == NSA_WIDTH
    w_gate = jnp.concatenate([z_a, z_b, wg], axis=1).astype(BF16)
    return w_qkv, w_gate, w_glu


def _token_minor(cache):
    n, tokens = cache.shape[:2]
    return jnp.transpose(cache, (0, 2, 3, 1)).reshape(n, KV_WIDTH, tokens)


def kernel(x_prompt, x_sample, cache_k_cmp, cache_v_cmp, cache_k_slc, cache_v_slc, cache_k_win, cache_v_win,
           state_conv, page_table, g_pre, w_in, cmp_w1_k, cmp_w2_k, cmp_pe_k, cmp_w1_v, cmp_w2_v, cmp_pe_v,
           rel_bias, conv_w, conv_b, ln_g, ln_b, w_pw, b_pw, w_out, g_post):
    depth = g_pre.shape[0]
    assert depth == 1
    layer = 0
    b, s, d = x_prompt.shape
    db, t_new, _ = x_sample.shape
    past = page_table.shape[1] * PAGE_SIZE
    conv_ch = conv_w.shape[-1]

    w_qkv, w_gate, w_glu = _split_w_in(w_in[layer])
    wts_k = _cmp_weights(cmp_w1_k[layer], cmp_w2_k[layer], cmp_pe_k[layer])
    wts_v = _cmp_weights(cmp_w1_v[layer], cmp_w2_v[layer], cmp_pe_v[layer])
    cparams = _conv_params(conv_w[layer], conv_b[layer], ln_g[layer], ln_b[layer], w_pw[layer], b_pw[layer])
    chunk_w = CMP_STRIDE * KV_WIDTH

    (q_t, kc, vc, ks, vs, kw, vw, ks_g, kw_g, vs_t, vw_t), sa, sb, gt, c_in = _projections(
        x_prompt, g_pre[layer], w_qkv, w_gate, w_glu, tm=512, attn_layouts=True)
    n_ch = s // CMP_STRIDE
    kcb = _compress(kc[:, :n_ch * CMP_STRIDE].reshape(b, n_ch, chunk_w), wts_k, layout="group_rows")
    vcb_t = _compress(vc[:, :n_ch * CMP_STRIDE].reshape(b, n_ch, chunk_w), wts_v, layout="group_cols")
    ma = _nsa_prompt(q_t, gt, sa, kcb, vcb_t, ks_g, vs_t, kw_g, vw_t, rel_bias)
    mb = _conv_prompt(c_in, jnp.zeros((b, CONV_HALO, conv_ch), F32), cparams, sb, ts=256)
    y_prompt = _out_proj(x_prompt, ma, mb, None, w_out[layer], g_post[layer], tm=512)
    n_keep = min(WINDOW, s)
    kv5 = lambda a: a.reshape(1, a.shape[0], a.shape[1], N_KV, HEAD_DIM)
    outs_p = (kv5(kc), kv5(vc), kv5(ks), kv5(vs), kv5(kw[:, -n_keep:]), kv5(vw[:, -n_keep:]),
              c_in[None, :, -(CONV_WIDTH - 1):])

    xs = x_sample.reshape(1, db * t_new, d)
    (q_s, kc_s, vc_s, ks_s, vs_s, kw_s, vw_s), sa_s, sb_s, gt_s, c_s = _projections(
        xs, g_pre[layer], w_qkv, w_gate, w_glu, tm=db * t_new, attn_layouts=False)
    tok = lambda a: a.reshape(db, t_new, a.shape[-1])
    kcb_s = _compress(_token_minor(cache_k_cmp[layer]), wts_k, page_table=page_table, layout="cols")
    vcb_s = _compress(_token_minor(cache_v_cmp[layer]), wts_v, page_table=page_table, layout="rows")
    o_a = _nsa_sample(tok(q_s), tok(gt_s), kcb_s, vcb_s,
                      _token_minor(cache_k_slc[layer]), _token_minor(cache_v_slc[layer]), page_table,
                      tok(ks_s), tok(vs_s), _token_minor(cache_k_win[layer]), _token_minor(cache_v_win[layer]),
                      tok(kw_s), tok(vw_s), rel_bias, past=past)
    mb_s = _conv_sample(tok(c_s), state_conv[layer], cparams, sb_s[0])
    y_sample = _out_proj(xs, o_a.reshape(1, db * t_new, NSA_WIDTH), mb_s[None], sa_s, w_out[layer],
                         g_post[layer], tm=db * t_new).reshape(db, t_new, d)
    n_keep_s = min(WINDOW, past + t_new)
    kv5s = lambda a: a.reshape(1, db, t_new, N_KV, HEAD_DIM)
    win = lambda cache, new: jnp.concatenate(
        [cache[layer], new.reshape(db, t_new, N_KV, HEAD_DIM)], axis=1)[None, :, -n_keep_s:]
    conv_s = jnp.concatenate([state_conv[layer], tok(c_s)], axis=1)[None, :, -(CONV_WIDTH - 1):]
    outs_s = (kv5s(kc_s), kv5s(vc_s), kv5s(ks_s), kv5s(vs_s), win(cache_k_win, kw_s), win(cache_v_win, vw_s), conv_s)
    return (y_prompt, y_sample) + outs_p + outs_s
```

```python
import functools
import math

import numpy as np
import jax
import jax.numpy as jnp
from jax import lax
from jax.experimental import pallas as pl
from jax.experimental.pallas import tpu as pltpu

F32 = jnp.float32
BF16 = jnp.bfloat16

HEAD_DIM = 64
N_KV = 4
HPG = 4
N_HEADS = N_KV * HPG
KV_WIDTH = N_KV * HEAD_DIM
NSA_WIDTH = N_HEADS * HEAD_DIM
CMP_BLOCK = 32
CMP_STRIDE = 16
SEL_BLOCK = 64
N_SEL = 16
WINDOW = 512
CONV_WIDTH = 31
N_BUCKETS = 32
MAX_DISTANCE = 128
FORCE_SCORE = 1e6
RMS_EPS = 1e-6
LN_EPS = 1e-5
PAGE_SIZE = 128
SCALE = HEAD_DIM ** -0.5

NEG = -1e30
PICKED = -3e38
LANES = 128
VMEM_LIMIT = 56 * 1024 * 1024

LOG2E = 1.4426950408889634
TQ = 256
FAR_GROUP = 4
PIPE_AHEAD = 2
AUG_PAD = 16
K_WIN_AUG = HEAD_DIM + AUG_PAD
K_SEL_AUG = HEAD_DIM + LANES + AUG_PAD
CPT = TQ // CMP_STRIDE
CONV_HALO = 32
SAMPLE_PAGES = 32
SAMPLE_PIECE_PAGES = 8
CMP_PAGES = 32


def _cparams(sem):
    return pltpu.CompilerParams(dimension_semantics=sem, vmem_limit_bytes=VMEM_LIMIT)


def _dot(a, b):
    return jnp.dot(a, b, preferred_element_type=F32)


def _dot_nt(a, b):
    return lax.dot_general(a, b, (((1,), (1,)), ((), ())), preferred_element_type=F32)


def _sigmoid(x):
    return 1.0 / (1.0 + jnp.exp(-x))


def _split_hi_lo(x):
    hi = x.astype(BF16)
    lo = (x - hi.astype(F32)).astype(BF16)
    return hi, lo


def _normed(x_ref, g_ref):
    x = x_ref[0]
    ms = jnp.mean(x * x, axis=-1, keepdims=True)
    return (x * lax.rsqrt(ms + RMS_EPS) * g_ref[...]).astype(BF16)


def _proj_qkv_kernel(x_ref, g_ref, w_ref, q_ref, kc_ref, vc_ref, ks_ref, vs_ref, kw_ref, vw_ref,
                     *attn_refs, attn_layouts):
    h = _normed(x_ref, g_ref)
    tm = h.shape[0]
    if attn_layouts:
        tok = pl.program_id(1) * tm + lax.broadcasted_iota(jnp.int32, (tm, LANES), 0)
        lane = lax.broadcasted_iota(jnp.int32, (tm, LANES), 1)
        blk_cols = jnp.where(lane == tok // SEL_BLOCK, 1.0, 0.0).astype(BF16)
        one_cols = jnp.where(lax.broadcasted_iota(jnp.int32, (tm, AUG_PAD), 1) < 2, 1.0, 0.0).astype(BF16)
    for g in range(N_KV):
        res = _dot(h, w_ref[:, g * KV_WIDTH:(g + 1) * KV_WIDTH]) * (SCALE * LOG2E if attn_layouts else SCALE)
        if attn_layouts:
            res_t = res.T
            for r in range(HPG):
                q_ref[0, g, r] = res_t[r * HEAD_DIM:(r + 1) * HEAD_DIM, :].astype(BF16)
        else:
            q_ref[0, :, g * KV_WIDTH:(g + 1) * KV_WIDTH] = res.astype(BF16)
    for j, o_ref in enumerate((kc_ref, vc_ref, ks_ref, vs_ref, kw_ref, vw_ref)):
        c0 = NSA_WIDTH + j * KV_WIDTH
        res = _dot(h, w_ref[:, c0:c0 + KV_WIDTH])
        o_ref[0] = res
        if attn_layouts and j in (2, 4):
            extra = [blk_cols, one_cols] if j == 2 else [one_cols]
            for g in range(N_KV):
                attn_refs[j // 2 - 1][0, g] = jnp.concatenate(
                    [res[:, g * HEAD_DIM:(g + 1) * HEAD_DIM].astype(BF16)] + extra, axis=1)
        if attn_layouts and j in (3, 5):
            res_t = res.T.astype(BF16)
            for g in range(N_KV):
                for kt in range(tm // TQ):
                    attn_refs[2 + j // 2 - 1][0, g, kt] = res_t[g * HEAD_DIM:(g + 1) * HEAD_DIM, kt * TQ:(kt + 1) * TQ]


def _proj_gate_kernel(x_ref, g_ref, w_ref, sa_ref, sb_ref, gt_ref):
    h = _normed(x_ref, g_ref)
    for o_ref, base in ((sa_ref, 0), (sb_ref, NSA_WIDTH)):
        for c in range(NSA_WIDTH // 256):
            z = _dot(h, w_ref[:, base + c * 256: base + (c + 1) * 256])
            o_ref[0, :, c * 256:(c + 1) * 256] = z * _sigmoid(z)
    for c in range(2):
        z = _dot(h, w_ref[:, 2 * NSA_WIDTH + c * 256: 2 * NSA_WIDTH + (c + 1) * 256])
        gt_ref[0, :, c * 256:(c + 1) * 256] = _sigmoid(z)


def _proj_glu_kernel(x_ref, g_ref, w_ref, c_ref):
    h = _normed(x_ref, g_ref)
    n = c_ref.shape[-1]
    for c in range(n // 256):
        a = _dot(h, w_ref[:, c * 256:(c + 1) * 256])
        gg = _dot(h, w_ref[:, n + c * 256: n + (c + 1) * 256])
        c_ref[0, :, c * 256:(c + 1) * 256] = a * _sigmoid(gg)


def _projections(x, g_pre, w_qkv, w_gate, w_glu, *, tm, attn_layouts):
    b, s, d = x.shape
    grid = (b, s // tm)
    x_spec = pl.BlockSpec((1, tm, d), lambda bi, i: (bi, i, 0))
    g_spec = pl.BlockSpec((1, d), lambda bi, i: (0, 0))

    def w_spec(w):
        return pl.BlockSpec(w.shape, lambda bi, i: (0, 0))

    def row_spec(width):
        return pl.BlockSpec((1, tm, width), lambda bi, i: (bi, i, 0))

    kv_shape = jax.ShapeDtypeStruct((b, s, KV_WIDTH), F32)
    if attn_layouts:
        assert tm % TQ == 0
        q_shape = jax.ShapeDtypeStruct((b, N_KV, HPG, HEAD_DIM, s), BF16)
        q_spec = pl.BlockSpec((1, N_KV, HPG, HEAD_DIM, tm), lambda bi, i: (bi, 0, 0, 0, i))
        k_shape = lambda w: jax.ShapeDtypeStruct((b, N_KV, s, w), BF16)
        k_spec = lambda w: pl.BlockSpec((1, N_KV, tm, w), lambda bi, i: (bi, 0, i, 0))
        v_shape = jax.ShapeDtypeStruct((b, N_KV, s // TQ, HEAD_DIM, TQ), BF16)
        v_spec = pl.BlockSpec((1, N_KV, tm // TQ, HEAD_DIM, TQ), lambda bi, i: (bi, 0, i, 0, 0))
        extra_shape = [k_shape(K_SEL_AUG), k_shape(K_WIN_AUG), v_shape, v_shape]
        extra_spec = [k_spec(K_SEL_AUG), k_spec(K_WIN_AUG), v_spec, v_spec]
    else:
        q_shape = jax.ShapeDtypeStruct((b, s, NSA_WIDTH), BF16)
        q_spec = row_spec(NSA_WIDTH)
        extra_shape, extra_spec = [], []
    g2 = g_pre.reshape(1, d)
    qkv = pl.pallas_call(
        functools.partial(_proj_qkv_kernel, attn_layouts=attn_layouts),
        grid=grid,
        in_specs=[x_spec, g_spec, w_spec(w_qkv)],
        out_specs=[q_spec] + [row_spec(KV_WIDTH)] * 6 + extra_spec,
        out_shape=[q_shape] + [kv_shape] * 6 + extra_shape,
        compiler_params=_cparams(("parallel", "parallel")),
        name="proj_qkv",
    )(x, g2, w_qkv)
    sa, sb, gt = pl.pallas_call(
        _proj_gate_kernel,
        grid=grid,
        in_specs=[x_spec, g_spec, w_spec(w_gate)],
        out_specs=[row_spec(NSA_WIDTH), row_spec(NSA_WIDTH), row_spec(N_KV * LANES)],
        out_shape=[jax.ShapeDtypeStruct((b, s, NSA_WIDTH), F32)] * 2
        + [jax.ShapeDtypeStruct((b, s, N_KV * LANES), F32)],
        compiler_params=_cparams(("parallel", "parallel")),
        name="proj_gate",
    )(x, g2, w_gate)
    c_in = pl.pallas_call(
        _proj_glu_kernel,
        grid=grid,
        in_specs=[x_spec, g_spec, w_spec(w_glu)],
        out_specs=row_spec(w_glu.shape[1] // 2),
        out_shape=jax.ShapeDtypeStruct((b, s, w_glu.shape[1] // 2), F32),
        compiler_params=_cparams(("parallel", "parallel")),
        name="proj_glu",
    )(x, g2, w_glu)
    return qkv, sa, sb, gt, c_in


def _cmp1_rows_kernel(x_ref, pe_ref, w_ref, pre_ref, pepre_ref):
    w = w_ref[...].reshape(-1, w_ref.shape[-1])
    pre_ref[0] = _dot(x_ref[0].astype(BF16), w)
    pepre_ref[...] = _dot(pe_ref[...], w)


def _cmp1_paged_kernel(pt_ref, cache_ref, perm_ref, pe_ref, w_ref, pre_ref, pepre_ref, buf_ref, sem_ref,
                       *, n_in, n_steps):
    step = pl.program_id(0) * n_steps + pl.program_id(1)
    last = pl.num_programs(0) * n_steps - 1
    slot = step % 2

    def fetch(s, p, into):
        page = pt_ref[s // n_steps, (s % n_steps) * n_in + p]
        return pltpu.make_async_copy(cache_ref.at[page], buf_ref.at[into, p], sem_ref.at[into])

    def wait_slot(into):
        for p in range(n_in):
            pltpu.make_async_copy(cache_ref.at[0], buf_ref.at[into, p], sem_ref.at[into]).wait()

    @pl.when(step == 0)
    def _():
        for p in range(n_in):
            fetch(0, p, 0).start()

    wait_slot(slot)
    perm = perm_ref[...]
    nxt = jnp.minimum(step + 1, last)
    zs = []
    for p in range(n_in):
        zs.append(_dot_nt(perm, buf_ref[slot, p].astype(BF16)))
        fetch(nxt, p, 1 - slot).start()
    rpp = PAGE_SIZE // CMP_STRIDE
    acc = jnp.zeros((n_in * rpp, w_ref.shape[-1]), F32)
    for c in range(CMP_STRIDE):
        xc = jnp.concatenate([z[c * rpp:(c + 1) * rpp] for z in zs], axis=0)
        acc = acc + _dot(xc.astype(BF16), w_ref[c])
    pre_ref[0] = acc
    pepre_ref[...] = _dot(pe_ref[...], w_ref[...].reshape(-1, w_ref.shape[-1]))

    @pl.when(step == last)
    def _():
        wait_slot(1 - slot)


def _cmp2_kernel(pre_ref, pepre_ref, w2_ref, o_ref, *, layout):
    pre = pre_ref[0]
    n_ch = pre.shape[0]
    a = pre[:, :KV_WIDTH]
    b_next = pltpu.roll(pre[:, KV_WIDTH:], n_ch - 1, axis=0)
    pe_bias = pepre_ref[0:1, :KV_WIDTH] + pepre_ref[1:2, KV_WIDTH:]
    z = a + b_next + pe_bias
    hid = 0.5 * z * (1.0 + jnp.tanh(math.sqrt(2.0 / math.pi) * (z + 0.044715 * (z * z * z))))
    out = _dot(hid.astype(BF16), w2_ref[...])
    if layout == "group_rows":
        for g in range(N_KV):
            o_ref[0, g] = out[:, g * HEAD_DIM:(g + 1) * HEAD_DIM].astype(BF16)
    elif layout == "group_cols":
        out_t = out.T.astype(BF16)
        for g in range(N_KV):
            o_ref[0, g] = out_t[g * HEAD_DIM:(g + 1) * HEAD_DIM, :]
    elif layout == "rows":
        o_ref[0] = out.astype(BF16)
    else:
        o_ref[0] = out.T.astype(BF16)


def _cmp_weights(w1, w2, pe):
    eye = jnp.eye(N_KV, dtype=F32)
    halves = w1.reshape(2, CMP_STRIDE, HEAD_DIM, -1)
    hdim = halves.shape[-1]
    wbig = jnp.einsum('acdh,gk->cgdakh', halves, eye)
    wbig = wbig.reshape(CMP_STRIDE, KV_WIDTH, 2 * N_KV * hdim).astype(BF16)
    w2big = jnp.einsum('hd,gk->ghkd', w2, eye).reshape(N_KV * hdim, KV_WIDTH).astype(BF16)
    pe_rows = jnp.broadcast_to(pe.reshape(2, CMP_STRIDE, 1, HEAD_DIM), (2, CMP_STRIDE, N_KV, HEAD_DIM))
    pe8 = jnp.zeros((8, CMP_STRIDE * KV_WIDTH), F32).at[:2].set(pe_rows.reshape(2, -1)).astype(BF16)
    return wbig, w2big, pe8


def _compress(x_view, wts, *, page_table=None, layout):
    wbig, w2big, pe8 = wts
    ncol = wbig.shape[-1]
    kdim = wbig.shape[0] * wbig.shape[1]
    if page_table is None:
        b, n_ch, _ = x_view.shape
        rows = min(n_ch, 256)
        const2 = lambda bi, i: (0, 0)
        pre, pepre = pl.pallas_call(
            _cmp1_rows_kernel, grid=(b, n_ch // rows),
            in_specs=[pl.BlockSpec((1, rows, kdim), lambda bi, i: (bi, i, 0)), pl.BlockSpec(pe8.shape, const2),
                      pl.BlockSpec(wbig.shape, lambda bi, i: (0, 0, 0))],
            out_specs=[pl.BlockSpec((1, rows, ncol), lambda bi, i: (bi, i, 0)), pl.BlockSpec((8, ncol), const2)],
            out_shape=[jax.ShapeDtypeStruct((b, n_ch, ncol), F32), jax.ShapeDtypeStruct((8, ncol), F32)],
            compiler_params=_cparams(("arbitrary", "arbitrary")), name="cmp_stage1")(x_view, pe8, wbig)
    else:
        b, n_pages = page_table.shape
        n_in = CMP_PAGES
        rpp = PAGE_SIZE // CMP_STRIDE
        rows = n_in * rpp
        n_ch = n_pages * rpp
        const2 = lambda bi, i, pt: (0, 0)
        n_steps = n_pages // n_in
        tok = np.arange(PAGE_SIZE)
        perm = (tok[None, :] == (tok[:, None] % rpp) * CMP_STRIDE + tok[:, None] // rpp)
        perm = jnp.asarray(perm, BF16)
        pre, pepre = pl.pallas_call(
            functools.partial(_cmp1_paged_kernel, n_in=n_in, n_steps=n_steps),
            grid_spec=pltpu.PrefetchScalarGridSpec(
                num_scalar_prefetch=1, grid=(b, n_steps),
                in_specs=[pl.BlockSpec(memory_space=pl.ANY), pl.BlockSpec(perm.shape, const2),
                          pl.BlockSpec(pe8.shape, const2), pl.BlockSpec(wbig.shape, lambda bi, i, pt: (0, 0, 0))],
                out_specs=[pl.BlockSpec((1, rows, ncol), lambda bi, i, pt: (bi, i, 0)),
                           pl.BlockSpec((8, ncol), const2)],
                scratch_shapes=[pltpu.VMEM((2, n_in, KV_WIDTH, PAGE_SIZE), F32), pltpu.SemaphoreType.DMA((2,))]),
            out_shape=[jax.ShapeDtypeStruct((b, n_ch, ncol), F32), jax.ShapeDtypeStruct((8, ncol), F32)],
            compiler_params=_cparams(("arbitrary", "arbitrary")), name="cmp_stage1_paged",
        )(page_table, x_view, perm, pe8, wbig)
    o_dims = {"group_rows": (N_KV, n_ch, HEAD_DIM), "group_cols": (N_KV, HEAD_DIM, n_ch),
              "rows": (n_ch, KV_WIDTH), "cols": (KV_WIDTH, n_ch)}[layout]
    return pl.pallas_call(
        functools.partial(_cmp2_kernel, layout=layout),
        grid=(b,),
        in_specs=[pl.BlockSpec((1, n_ch, ncol), lambda bi: (bi, 0, 0)),
                  pl.BlockSpec((8, ncol), lambda bi: (0, 0)),
                  pl.BlockSpec(w2big.shape, lambda bi: (0, 0))],
        out_specs=pl.BlockSpec((1,) + o_dims, lambda bi: (bi,) + (0,) * len(o_dims)),
        out_shape=jax.ShapeDtypeStruct((b,) + o_dims, BF16),
        compiler_params=_cparams(("parallel",)), name="cmp_stage2",
    )(pre, pepre, w2big)


def _t5_bucket(dist):
    dist = np.maximum(np.asarray(dist, np.int64), 0)
    max_exact = N_BUCKETS // 2
    d32 = np.maximum(dist, 1).astype(np.float32)
    large = max_exact + (np.log(d32 / np.float32(max_exact)) / np.float32(math.log(MAX_DISTANCE / max_exact))
                         * np.float32(N_BUCKETS - max_exact)).astype(np.int32)
    large = np.minimum(large, N_BUCKETS - 1)
    return np.where(dist < max_exact, dist, large).astype(np.int32)


def _bias_lookup(table, dist):
    bucket = jnp.asarray(_t5_bucket(dist).reshape(-1, 1))
    onehot = (bucket == jnp.arange(N_BUCKETS, dtype=jnp.int32)[None, :]).astype(F32)
    out = jnp.dot(onehot, table, precision=lax.Precision.HIGHEST)
    return out.reshape(tuple(np.shape(dist)) + (table.shape[1],))


def _top_block_round(chosen, work, jrow):
    m = jnp.max(work, axis=0, keepdims=True)
    idx = jnp.min(jnp.where(work == m, jrow, 1 << 20), axis=0, keepdims=True)
    pick = jrow == idx
    return jnp.where(pick, 1.0, chosen), jnp.where(pick, PICKED, work)


def _select_top_blocks(score, jrow, k_sel):
    state = (jnp.zeros(score.shape, F32), score)
    for _ in range(k_sel):
        state = _top_block_round(*state, jrow)
    return state[0]


def _nsa_prompt_kernel(q_ref, gt_ref, sa_ref, kcb_ref, vcbt_ref, ks_ref, vst_ref, kw_ref, vwt_ref,
                       pcd_ref, far_ref, far16_ref, tz0_ref, tz1_ref, covt_ref, o_ref,
                       lc_ref, qs_ref, qw_ref, m_ref, l_ref, acc_ref, *, n_slc):
    i = pl.program_id(2)
    nl = HPG * TQ
    ncp = kcb_ref.shape[2]
    q_t = jnp.concatenate([q_ref[0, 0, r] for r in range(HPG)], axis=1)
    lane_t = lax.broadcasted_iota(jnp.int32, (1, nl), 1) & (TQ - 1)
    qpos = i * TQ + lane_t
    far = far_ref[0, 0:1, :]
    qs_ref[0:HEAD_DIM, :] = q_t
    qs_ref[HEAD_DIM + LANES:, :] = far16_ref[0]
    qw_ref[0:HEAD_DIM, :] = q_t
    qw_ref[HEAD_DIM:, :] = far16_ref[0]

    lc_ref[0:CPT, :] = jnp.zeros((CPT, nl), F32)
    lc_ref[CPT:CPT + ncp, :] = _dot(kcb_ref[0, 0], q_t) + far
    near = pl.ds(pl.multiple_of(i * CPT, CPT), 2 * CPT)
    lc_ref[near, :] = lc_ref[near, :] + pcd_ref[0]
    lc = lc_ref[CPT:CPT + ncp, :]
    cend = lax.broadcasted_iota(jnp.int32, (ncp, 1), 0) * CMP_STRIDE + (CMP_BLOCK - 1)
    valid_c = cend <= qpos
    lm = jnp.where(valid_c, lc, NEG)
    mc = jnp.max(lm, axis=0, keepdims=True)
    ec = jnp.where(valid_c, jnp.exp2(lm - mc), 0.0)
    den = jnp.sum(ec, axis=0, keepdims=True)
    p_c = ec / jnp.where(den > 0, den, 1.0)
    o_c = _dot(vcbt_ref[0, 0], p_c.astype(BF16))

    psum = p_c[:, 0:TQ]
    for r in range(1, HPG):
        psum = psum + p_c[:, r * TQ:(r + 1) * TQ]
    hi, lo = _split_hi_lo(psum)
    imp_t = _dot(covt_ref[...], hi) + _dot(covt_ref[...], lo)
    jrow = lax.broadcasted_iota(jnp.int32, (LANES, TQ), 0)
    qpos_t = qpos[:, 0:TQ]
    cur = qpos_t // SEL_BLOCK
    causal = (jrow * SEL_BLOCK <= qpos_t) & (jrow < n_slc)
    forced = (jrow == 0) | (jrow == cur) | (jrow == cur - 1)
    score = jnp.where(forced, FORCE_SCORE, jnp.where(causal, imp_t, NEG))
    k_sel = min(N_SEL, n_slc)
    topk = [(jnp.zeros(score.shape, F32), score), 0]

    def topk_rounds(n):
        for _ in range(min(n, k_sel - topk[1])):
            topk[0] = _top_block_round(*topk[0], jrow)
            topk[1] += 1

    key_u = lax.broadcasted_iota(jnp.int32, (TQ, 1), 0)
    causal_diag = key_u <= lane_t

    def run_pieces(specs, between=lambda: None):
        def logits(t):
            k_ref, _, qx_ref, kt, bias, mask = specs[t]
            s = _dot(k_ref[0, 0, pl.ds(pl.multiple_of(kt * TQ, TQ), TQ), :], qx_ref[...])
            if bias is not None:
                s = s + bias
            return s if mask is None else jnp.where(mask, s, NEG)

        n = len(specs)
        s = {t: logits(t) for t in range(min(PIPE_AHEAD, n))}
        between()
        pieces = []
        for t in range(n):
            m = jnp.max(s[t], axis=0, keepdims=True)
            p = jnp.exp2(s.pop(t) - m)
            if t + PIPE_AHEAD < n:
                s[t + PIPE_AHEAD] = logits(t + PIPE_AHEAD)
            between()
            _, vt_ref, _, kt, _, _ = specs[t]
            pieces.append((_dot(vt_ref[0, 0, kt], p.astype(BF16)), m, jnp.sum(p, axis=0, keepdims=True)))
            between()
        return pieces

    def merge(pieces):
        m_new = functools.reduce(jnp.maximum, [m for _, m, _ in pieces])
        scales = [jnp.exp2(m - m_new) for _, m, _ in pieces]
        acc = functools.reduce(jnp.add, [a * acc for a, (acc, _, _) in zip(scales, pieces)])
        l = functools.reduce(jnp.add, [a * l for a, (_, _, l) in zip(scales, pieces)])
        return acc, m_new, l

    prev = jnp.maximum(i - 1, 0)
    prev2 = jnp.maximum(i - 2, 0)
    tz1 = tz1_ref[0] + jnp.where(i >= 1, 0.0, NEG)
    tz0 = tz0_ref[0]
    n_gaps = 1 + 2 * 3
    acc_w, _, l_w = merge(run_pieces([(kw_ref, vwt_ref, qw_ref, prev2, None, (key_u > lane_t) & (i >= 2)),
                                      (kw_ref, vwt_ref, qw_ref, prev, tz1, None),
                                      (kw_ref, vwt_ref, qw_ref, i, tz0, causal_diag)],
                                     between=lambda: topk_rounds(-(-k_sel // n_gaps))))
    o_w = acc_w / l_w
    topk_rounds(k_sel)
    chosen = topk[0][0]
    sel_add = jnp.where(causal & (chosen > 0.5), 0.0, NEG).astype(BF16)
    qs_ref[HEAD_DIM:HEAD_DIM + LANES, :] = jnp.concatenate([sel_add] * HPG, axis=1)

    m_ref[...] = jnp.full(m_ref.shape, NEG, F32)
    l_ref[...] = jnp.zeros(l_ref.shape, F32)
    acc_ref[...] = jnp.zeros(acc_ref.shape, F32)

    def state():
        return acc_ref[...], m_ref[...], l_ref[...]

    def merge_far(kt0, n):
        acc, m, l = merge([state()] + run_pieces([(ks_ref, vst_ref, qs_ref, kt0 + t, None, None) for t in range(n)]))
        acc_ref[...] = acc
        m_ref[...] = m
        l_ref[...] = l

    n_far = jnp.maximum(i - 1, 0)

    def far_group(jg, carry):
        merge_far(FAR_GROUP * jg, FAR_GROUP)
        return carry

    lax.fori_loop(0, n_far // FAR_GROUP, far_group, 0)
    rest = n_far % FAR_GROUP
    size = FAR_GROUP // 2
    while size >= 1:
        @pl.when((rest & size) != 0)
        def _(size=size):
            merge_far(n_far - (rest & (2 * size - 1)), size)
        size //= 2

    acc_s, _, l_s = merge([state()] + run_pieces([(ks_ref, vst_ref, qs_ref, prev, tz1, None),
                                                  (ks_ref, vst_ref, qs_ref, i, tz0, causal_diag)]))
    o_s = acc_s / l_s

    g_t = gt_ref[0].T

    def gate_row(branch):
        return jnp.concatenate([g_t[branch * HPG + r:branch * HPG + r + 1, :] for r in range(HPG)], axis=1)

    o_t = gate_row(0) * o_c + gate_row(1) * o_s + gate_row(2) * o_w
    o_rd = jnp.concatenate([o_t[:, r * TQ:(r + 1) * TQ] for r in range(HPG)], axis=0)
    o_ref[0] = (o_rd.T * sa_ref[0]).astype(BF16)


def _nsa_prompt(q_t, gt, sa, kcb, vcb_t, ks, vs_t, kw, vw_t, rel_bias):
    b, _, _, _, s = q_t.shape
    assert WINDOW == 2 * TQ and s % TQ == 0 and TQ >= MAX_DISTANCE
    nq = s // TQ
    ncp = s // CMP_STRIDE
    n_cmp = ncp - 1
    n_slc = -(-s // SEL_BLOCK)
    assert n_slc <= LANES and ncp >= 2 * CPT
    nl = HPG * TQ
    table = rel_bias.astype(F32)
    uu = np.arange(TQ)[:, None]
    tt = np.arange(TQ)[None, :]

    def per_group(tab):
        rows = tab.shape[1]
        return jnp.transpose(tab.reshape(N_KV, HPG, rows, TQ), (0, 2, 1, 3)).reshape(N_KV, rows, nl)

    table2 = table * LOG2E
    heads_first = lambda dist: jnp.moveaxis(_bias_lookup(table2, dist), -1, 0)
    far_h = _bias_lookup(table2, np.array([MAX_DISTANCE]))[0]
    far_b = far_h[:, None, None]
    tz0 = per_group(heads_first(tt - uu) - far_b)
    tz1 = per_group(heads_first(TQ + tt - uu) - far_b)
    far = per_group(jnp.broadcast_to(far_b, (N_HEADS, 8, TQ)))
    far_hi = far_h.astype(BF16)
    far_lo = (far_h - far_hi.astype(F32)).astype(BF16)
    far16 = jnp.zeros((N_HEADS, AUG_PAD, TQ), BF16).at[:, 0].set(far_hi[:, None]).at[:, 1].set(far_lo[:, None])
    far16 = per_group(far16)
    e = np.arange(2 * CPT)[:, None] - CPT
    pcd = per_group(heads_first(tt - CMP_STRIDE * e - (CMP_BLOCK - 1)) - far_b)
    c = np.arange(ncp)[None, :]
    j = np.arange(LANES)[:, None]
    cov_t = ((c * CMP_STRIDE < (j + 1) * SEL_BLOCK) & (c * CMP_STRIDE + CMP_BLOCK > j * SEL_BLOCK)
             & (c < n_cmp) & (j < n_slc))
    cov_t = jnp.asarray(cov_t, BF16)

    per_g = lambda rows: pl.BlockSpec((1, rows, nl), lambda bi, g, i: (g, 0, 0))
    return pl.pallas_call(
        functools.partial(_nsa_prompt_kernel, n_slc=n_slc),
        grid=(b, N_KV, nq),
        in_specs=[
            pl.BlockSpec((1, 1, HPG, HEAD_DIM, TQ), lambda bi, g, i: (bi, g, 0, 0, i)),
            pl.BlockSpec((1, TQ, LANES), lambda bi, g, i: (bi, i, g)),
            pl.BlockSpec((1, TQ, KV_WIDTH), lambda bi, g, i: (bi, i, g)),
            pl.BlockSpec((1, 1, ncp, HEAD_DIM), lambda bi, g, i: (bi, g, 0, 0)),
            pl.BlockSpec((1, 1, HEAD_DIM, ncp), lambda bi, g, i: (bi, g, 0, 0)),
            pl.BlockSpec((1, 1, s, K_SEL_AUG), lambda bi, g, i: (bi, g, 0, 0)),
            pl.BlockSpec((1, 1, nq, HEAD_DIM, TQ), lambda bi, g, i: (bi, g, 0, 0, 0)),
            pl.BlockSpec((1, 1, s, K_WIN_AUG), lambda bi, g, i: (bi, g, 0, 0)),
            pl.BlockSpec((1, 1, nq, HEAD_DIM, TQ), lambda bi, g, i: (bi, g, 0, 0, 0)),
            per_g(2 * CPT), per_g(8), per_g(AUG_PAD), per_g(TQ), per_g(TQ),
            pl.BlockSpec(cov_t.shape, lambda bi, g, i: (0, 0)),
        ],
        out_specs=pl.BlockSpec((1, TQ, KV_WIDTH), lambda bi, g, i: (bi, i, g)),
        out_shape=jax.ShapeDtypeStruct((b, s, NSA_WIDTH), BF16),
        scratch_shapes=[pltpu.VMEM((CPT + ncp, nl), F32), pltpu.VMEM((K_SEL_AUG, nl), BF16),
                        pltpu.VMEM((K_WIN_AUG, nl), BF16),
                        pltpu.VMEM((1, nl), F32), pltpu.VMEM((1, nl), F32), pltpu.VMEM((HEAD_DIM, nl), F32)],
        compiler_params=_cparams(("parallel", "parallel", "arbitrary")),
        name="nsa_prompt",
    )(q_t, gt, sa, kcb, vcb_t, ks, vs_t, kw, vw_t, pcd, far, far16, tz0, tz1, cov_t)


def _nsa_sample_kernel(*refs, n_pg, n_chunks, n_slc, past, t_new):
    pt_ref = refs[0]
    (qbd_ref, kcbt_ref, vcb_ref, bc_ref, cov_ref, rmat_ref, ck_ref, cv_ref) = refs[1:9]
    (bsl_ref, ksn_ref, vsn_ref, bsn_ref, kwc_ref, vwc_ref, kwn_ref, vwn_ref, bw_ref, gt_ref, e4_ref,
     o_ref, sel_ref, m_ref, l_ref, acc_ref, oc_ref, kbuf_ref, vbuf_ref, sem_ref) = refs[9:]
    j = pl.program_id(1)
    step = pl.program_id(0) * n_chunks + j
    last = pl.num_programs(0) * n_chunks - 1
    slot = step % 2
    caches = ((ck_ref, kbuf_ref), (cv_ref, vbuf_ref))

    def fetch(kv, s, p, into):
        page = pt_ref[s // n_chunks, (s % n_chunks) * n_pg + p]
        cache_ref, buf_ref = caches[kv]
        return pltpu.make_async_copy(cache_ref.at[page], buf_ref.at[into, p], sem_ref.at[kv, into])

    def wait_slot(into):
        for kv, (cache_ref, buf_ref) in enumerate(caches):
            for p in range(n_pg):
                pltpu.make_async_copy(cache_ref.at[0], buf_ref.at[into, p], sem_ref.at[kv, into]).wait()

    @pl.when(step == 0)
    def _():
        for kv in range(2):
            for p in range(n_pg):
                fetch(kv, 0, p, 0).start()

    wait_slot(slot)
    nxt = jnp.minimum(step + 1, last)
    qbd = qbd_ref[0]
    nsp = cov_ref.shape[1]
    n_win = kwc_ref.shape[2]

    def softmax_rows(s):
        m = jnp.max(s, axis=-1, keepdims=True)
        e = jnp.where(s > 0.5 * NEG, jnp.exp(s - m), 0.0)
        den = jnp.sum(e, axis=-1, keepdims=True)
        return e / jnp.where(den > 0, den, 1.0)

    def flash_step(s, v, v_is_transposed):
        m_old = m_ref[...]
        m_new = jnp.maximum(m_old, jnp.max(s, axis=-1, keepdims=True))
        alpha = jnp.exp(m_old - m_new)
        p = jnp.exp(s - m_new)
        l_ref[...] = alpha * l_ref[...] + jnp.sum(p, axis=-1, keepdims=True)
        pv = _dot_nt(p.astype(BF16), v) if v_is_transposed else _dot(p.astype(BF16), v)
        acc_ref[...] = alpha * acc_ref[...] + pv
        m_ref[...] = m_new

    @pl.when(j == 0)
    def _():
        p_c = softmax_rows(_dot(qbd, kcbt_ref[0]) + bc_ref[...])
        oc_ref[...] = _dot(p_c.astype(BF16), vcb_ref[0])
        hi, lo = _split_hi_lo(p_c)
        psum = _dot(rmat_ref[...], hi) + _dot(rmat_ref[...], lo)
        hi, lo = _split_hi_lo(psum)
        imp_t = (_dot(hi, cov_ref[...]) + _dot(lo, cov_ref[...])).T
        jrow = lax.broadcasted_iota(jnp.int32, (nsp, LANES), 0)
        lane = lax.broadcasted_iota(jnp.int32, (1, LANES), 1)
        qpos = past + lane % t_new
        cur = qpos // SEL_BLOCK
        causal = (jrow * SEL_BLOCK <= qpos) & (jrow < n_slc)
        forced = (jrow == 0) | (jrow == cur) | (jrow == cur - 1)
        score = jnp.where(forced, FORCE_SCORE, jnp.where(causal, imp_t, NEG))
        chosen = _select_top_blocks(score, jrow, min(N_SEL, n_slc))
        sel = jnp.where(causal & (lane < N_HEADS * t_new) & (chosen > 0.5), 0.0, -1.0).T
        for k in range(nsp // LANES):
            sel_ref[k] = sel[:, k * LANES:(k + 1) * LANES]
        m_ref[...] = jnp.full(m_ref.shape, NEG, F32)
        l_ref[...] = jnp.zeros(l_ref.shape, F32)
        acc_ref[...] = jnp.zeros(acc_ref.shape, F32)

    far = bsl_ref[:, PAGE_SIZE:2 * PAGE_SIZE]
    last_page = jnp.where(j == n_chunks - 1, bsl_ref[:, 0:PAGE_SIZE], far)
    chunks_per_tile = LANES // (n_pg * (PAGE_SIZE // SEL_BLOCK))
    sel_tile = sel_ref[j // chunks_per_tile].astype(BF16)
    pp = SAMPLE_PIECE_PAGES
    n_pieces = n_pg // pp

    def logits(t):
        pages = range(t * pp, (t + 1) * pp)
        k_t = jnp.concatenate([kbuf_ref[slot, p] for p in pages], axis=1).astype(BF16)
        if t < 2:
            for p in range(n_pg):
                fetch(t, nxt, p, 1 - slot).start()
        bias = jnp.concatenate([far] * (pp - 1) + [last_page if t == n_pieces - 1 else far], axis=1)
        mask_add = _dot(sel_tile, e4_ref[j % chunks_per_tile, :, t * pp * PAGE_SIZE:(t + 1) * pp * PAGE_SIZE])
        return _dot(qbd, k_t) + bias + mask_add

    s = {t: logits(t) for t in range(min(PIPE_AHEAD, n_pieces))}
    pieces = []
    for t in range(n_pieces):
        m = jnp.max(s[t], axis=-1, keepdims=True)
        p = jnp.exp(s.pop(t) - m)
        if t + PIPE_AHEAD < n_pieces:
            s[t + PIPE_AHEAD] = logits(t + PIPE_AHEAD)
        pages = range(t * pp, (t + 1) * pp)
        v_t = jnp.concatenate([vbuf_ref[slot, pg] for pg in pages], axis=1).astype(BF16)
        pieces.append((_dot_nt(p.astype(BF16), v_t), m, jnp.sum(p, axis=-1, keepdims=True)))
    m_old = m_ref[...]
    m_new = functools.reduce(jnp.maximum, [m_old] + [m for _, m, _ in pieces])
    scales = [jnp.exp(m - m_new) for _, m, _ in pieces]
    a_old = jnp.exp(m_old - m_new)
    l_ref[...] = a_old * l_ref[...] + functools.reduce(jnp.add, [a * l for a, (_, _, l) in zip(scales, pieces)])
    acc_ref[...] = a_old * acc_ref[...] + functools.reduce(jnp.add, [a * acc for a, (acc, _, _) in zip(scales, pieces)])
    m_ref[...] = m_new

    @pl.when(j == n_chunks - 1)
    def _():
        sn = _dot_nt(qbd, ksn_ref[0].astype(BF16)) + bsn_ref[...]
        blk = n_slc - 1
        seln = sel_ref[blk // LANES][:, blk % LANES:blk % LANES + 1]
        flash_step(sn + seln * (-NEG), vsn_ref[0].astype(BF16), False)
        l = l_ref[...]
        o_s = acc_ref[...] / jnp.where(l > 0, l, 1.0)
        sw = jnp.concatenate([_dot(qbd, kwc_ref[0].astype(BF16)), _dot_nt(qbd, kwn_ref[0].astype(BF16))], axis=1)
        p_w = softmax_rows(sw + bw_ref[...]).astype(BF16)
        o_w = _dot_nt(p_w[:, :n_win], vwc_ref[0].astype(BF16)) + _dot(p_w[:, n_win:], vwn_ref[0].astype(BF16))
        gt = gt_ref[0]
        o_ref[0] = gt[:, 0:1] * oc_ref[...] + gt[:, 1:2] * o_s + gt[:, 2:3] * o_w

    @pl.when(step == last)
    def _():
        wait_slot(1 - slot)


def _nsa_sample(q, gates, kcb_t, vcb, cache_k_slc, cache_v_slc, page_table, ks_new, vs_new,
                kw_cache, vw_cache, kw_new, vw_new, rel_bias, *, past):
    db, t_new, _ = q.shape
    n_pages = page_table.shape[1]
    assert past == n_pages * PAGE_SIZE and past % SEL_BLOCK == 0 and t_new <= SEL_BLOCK
    assert PAGE_SIZE >= MAX_DISTANCE
    n_cp = kcb_t.shape[2]
    n_slc = -(-(past + t_new) // SEL_BLOCK)
    nsp = -(-n_slc // LANES) * LANES
    n_pg = SAMPLE_PAGES
    n_chunks = n_pages // n_pg
    blocks_per_chunk = n_pg * (PAGE_SIZE // SEL_BLOCK)
    assert LANES % blocks_per_chunk == 0 and n_pg // SAMPLE_PIECE_PAGES >= 2
    n_win = kw_cache.shape[2]
    nl = N_HEADS * t_new
    assert nl <= LANES
    row = np.arange(LANES)
    row_ok = (row < nl)[:, None]
    row_g = np.where(row < nl, row // (HPG * t_new), 0)
    row_t = (row % t_new)[:, None]
    q5 = q.reshape(db, t_new, N_KV, HPG, HEAD_DIM)
    qbd = jnp.einsum('btgrd,gk->bgrtkd', q5.astype(F32), jnp.eye(N_KV, dtype=F32)).reshape(db, nl, KV_WIDTH)
    qbd = jnp.pad(qbd, ((0, 0), (0, LANES - nl), (0, 0))).astype(BF16)

    table = rel_bias.astype(F32)

    def bias_tab(dist, valid):
        vals = jnp.moveaxis(_bias_lookup(table, dist), -1, 0)
        vals = jnp.where(jnp.asarray(valid)[None], vals, NEG).reshape(nl, -1)
        return jnp.pad(vals, ((0, LANES - nl), (0, 0)), constant_values=NEG)

    tq = np.arange(t_new)[:, None]
    qpos = past + tq
    cblk = np.arange(n_cp)[None, :]
    dist_c = qpos - (cblk * CMP_STRIDE + CMP_BLOCK - 1)
    bc = bias_tab(dist_c, (dist_c >= 0) & (cblk < n_cp - 1))
    kpos = past - PAGE_SIZE + np.arange(PAGE_SIZE)[None, :]
    all_ok = np.ones((t_new, PAGE_SIZE), bool)
    bsl = jnp.concatenate([bias_tab(qpos - kpos, all_ok),
                           bias_tab(np.full((t_new, PAGE_SIZE), MAX_DISTANCE), all_ok)], axis=1)
    u = np.arange(PAGE_SIZE)[None, :]
    new_ok = (u <= tq) & (u < t_new)
    bsn = bias_tab(tq - u, new_ok)
    wpos = past - n_win + np.arange(n_win)[None, :]
    dist_w = qpos - wpos
    bw = jnp.concatenate([bias_tab(dist_w, (dist_w >= 0) & (dist_w < WINDOW) & (wpos >= 0)),
                          bias_tab(tq - u, new_ok & (tq - u < WINDOW))], axis=1)
    c = np.arange(n_cp)[:, None]
    jb = np.arange(nsp)[None, :]
    cov = ((c * CMP_STRIDE < (jb + 1) * SEL_BLOCK) & (c * CMP_STRIDE + CMP_BLOCK > jb * SEL_BLOCK)
           & (c < n_cp - 1) & (jb < n_slc))
    cov = jnp.asarray(cov, BF16)
    same = (row_g[:, None] == row_g[None, :]) & (row_t == row_t.T) & row_ok & row_ok.T
    rmat = jnp.asarray(same, BF16)
    kk = np.arange(n_pg * PAGE_SIZE)[None, None, :] // SEL_BLOCK
    e4 = np.arange(LANES)[None, :, None] == (np.arange(LANES // blocks_per_chunk)[:, None, None] * blocks_per_chunk + kk)
    e4 = jnp.asarray(np.where(e4, -NEG, 0.0), BF16)
    g5 = gates.reshape(db, t_new, N_KV, LANES)[..., :3 * HPG].reshape(db, t_new, N_KV, 3, HPG)
    gcol = jnp.transpose(g5, (0, 2, 4, 1, 3)).reshape(db, nl, 3)
    gcol = jnp.pad(gcol, ((0, 0), (0, LANES - nl), (0, 5)))

    def pad_new(a):
        return jnp.pad(a, ((0, 0), (0, PAGE_SIZE - t_new), (0, 0)))

    ksn, vsn, kwn, vwn = (pad_new(a) for a in (ks_new, vs_new, kw_new, vw_new))

    per_b = lambda shape: pl.BlockSpec((1,) + shape, lambda b, jc, pt: (b, 0, 0))
    full = lambda a: pl.BlockSpec(a.shape, lambda b, jc, pt: (0,) * a.ndim)
    in_specs = ([per_b((LANES, KV_WIDTH)), per_b((KV_WIDTH, n_cp)), per_b((n_cp, KV_WIDTH)),
                 full(bc), full(cov), full(rmat)]
                + [pl.BlockSpec(memory_space=pl.ANY)] * 2
                + [full(bsl), per_b((PAGE_SIZE, KV_WIDTH)), per_b((PAGE_SIZE, KV_WIDTH)), full(bsn),
                   per_b((KV_WIDTH, n_win)), per_b((KV_WIDTH, n_win)),
                   per_b((PAGE_SIZE, KV_WIDTH)), per_b((PAGE_SIZE, KV_WIDTH)), full(bw), per_b((LANES, 8)), full(e4)])
    o = pl.pallas_call(
        functools.partial(_nsa_sample_kernel, n_pg=n_pg, n_chunks=n_chunks, n_slc=n_slc, past=past, t_new=t_new),
        grid_spec=pltpu.PrefetchScalarGridSpec(
            num_scalar_prefetch=1, grid=(db, n_chunks), in_specs=in_specs,
            out_specs=pl.BlockSpec((1, LANES, KV_WIDTH), lambda b, jc, pt: (b, 0, 0)),
            scratch_shapes=[pltpu.VMEM((nsp // LANES, LANES, LANES), F32), pltpu.VMEM((LANES, 1), F32),
                            pltpu.VMEM((LANES, 1), F32), pltpu.VMEM((LANES, KV_WIDTH), F32),
                            pltpu.VMEM((LANES, KV_WIDTH), F32),
                            pltpu.VMEM((2, n_pg, KV_WIDTH, PAGE_SIZE), F32),
                            pltpu.VMEM((2, n_pg, KV_WIDTH, PAGE_SIZE), F32), pltpu.SemaphoreType.DMA((2, 2))]),
        out_shape=jax.ShapeDtypeStruct((db, LANES, KV_WIDTH), F32),
        compiler_params=_cparams(("arbitrary", "arbitrary")),
        name="nsa_sample",
    )(page_table, qbd, kcb_t, vcb, bc, cov, rmat, cache_k_slc, cache_v_slc,
      bsl, ksn, vsn, bsn, kw_cache, vw_cache, kwn, vwn, bw, gcol, e4)
    o6 = o[:, :nl].reshape(db, N_KV, HPG, t_new, N_KV, HEAD_DIM)
    o_diag = jnp.stack([o6[:, g, :, :, g] for g in range(N_KV)], axis=1)
    return jnp.transpose(o_diag, (0, 3, 1, 2, 4)).reshape(db, t_new, NSA_WIDTH)


def _conv_tail(y, cb_ref, lg_ref, lb_ref, wpw_ref, bpw_ref, sb):
    y = y + cb_ref[...]
    mu = jnp.mean(y, axis=-1, keepdims=True)
    yc = y - mu
    var = jnp.mean(yc * yc, axis=-1, keepdims=True)
    yn = yc * lax.rsqrt(var + LN_EPS) * lg_ref[...] + lb_ref[...]
    act = yn * _sigmoid(yn)
    return ((_dot(act.astype(BF16), wpw_ref[...]) + bpw_ref[...]) * sb).astype(BF16)


def _conv_prompt_kernel(c_ref, halo_ref, init_ref, cw_ref, cb_ref, lg_ref, lb_ref, wpw_ref, bpw_ref, sb_ref,
                        o_ref, full_ref, sh_ref, y_ref, *, ts):
    j = pl.program_id(1)
    full_ref[CONV_HALO:CONV_HALO + ts, :] = c_ref[0]

    @pl.when(j == 0)
    def _():
        full_ref[0:CONV_HALO, :] = init_ref[0]

    @pl.when(j > 0)
    def _():
        full_ref[0:CONV_HALO, :] = halo_ref[0]

    first = CONV_HALO - (CONV_WIDTH - 1)
    span = sh_ref.shape[1]
    for sft in range(1, 8):
        sh_ref[sft - 1] = full_ref[sft:sft + span, :]
    rb = 64
    ch = full_ref.shape[1]
    for c0 in range(0, ch, LANES):
        for r0 in range(0, ts, rb):
            acc = jnp.zeros((rb, LANES), F32)
            for w in range(CONV_WIDTH):
                sft = (first + w) % 8
                base = r0 + first + w - sft
                if sft == 0:
                    x = full_ref[base:base + rb, c0:c0 + LANES]
                else:
                    x = sh_ref[sft - 1, base:base + rb, c0:c0 + LANES]
                acc = acc + x * cw_ref[w:w + 1, c0:c0 + LANES]
            y_ref[r0:r0 + rb, c0:c0 + LANES] = acc
    o_ref[0] = _conv_tail(y_ref[...], cb_ref, lg_ref, lb_ref, wpw_ref, bpw_ref, sb_ref[0])


def _conv_sample_kernel(c_ref, st_ref, cw_ref, cb_ref, lg_ref, lb_ref, wpw_ref, bpw_ref, sb_ref,
                        o_ref, full_ref, y_ref):
    nb, t_new, _ = c_ref.shape
    n_st = st_ref.shape[1]
    full_ref[:, 0:n_st, :] = st_ref[...]
    full_ref[:, n_st:n_st + t_new, :] = c_ref[...]
    first = n_st - (CONV_WIDTH - 1)
    for b in range(nb):
        acc = jnp.zeros((t_new, full_ref.shape[2]), F32)
        for w in range(CONV_WIDTH):
            acc = acc + full_ref[b, first + w:first + w + t_new, :] * cw_ref[w:w + 1, :]
        y_ref[b * t_new:(b + 1) * t_new, :] = acc
    o_ref[...] = _conv_tail(y_ref[...], cb_ref, lg_ref, lb_ref, wpw_ref, bpw_ref, sb_ref[...])


def _conv_params(conv_w, conv_b, ln_g, ln_b, w_pw, b_pw):
    ch = conv_w.shape[1]
    cw = jnp.pad(conv_w, ((0, 32 - CONV_WIDTH), (0, 0)))
    return (cw, conv_b.reshape(1, ch), ln_g.reshape(1, ch), ln_b.reshape(1, ch), w_pw.astype(BF16),
            b_pw.reshape(1, ch))


def _conv_prompt(c_in, init, params, sb, *, ts):
    b, s, ch = c_in.shape
    hb = ts // CONV_HALO
    const = lambda a: pl.BlockSpec(a.shape, lambda bi, j: (0,) * a.ndim)
    return pl.pallas_call(
        functools.partial(_conv_prompt_kernel, ts=ts),
        grid=(b, s // ts),
        in_specs=[pl.BlockSpec((1, ts, ch), lambda bi, j: (bi, j, 0)),
                  pl.BlockSpec((1, CONV_HALO, ch), lambda bi, j: (bi, jnp.maximum(j * hb - 1, 0), 0)),
                  pl.BlockSpec((1, CONV_HALO, ch), lambda bi, j: (bi, 0, 0))]
        + [const(a) for a in params]
        + [pl.BlockSpec((1, ts, ch), lambda bi, j: (bi, j, 0))],
        out_specs=pl.BlockSpec((1, ts, ch), lambda bi, j: (bi, j, 0)),
        out_shape=jax.ShapeDtypeStruct((b, s, ch), BF16),
        scratch_shapes=[pltpu.VMEM((CONV_HALO + ts, ch), F32), pltpu.VMEM((7, CONV_HALO + ts - 8, ch), F32),
                        pltpu.VMEM((ts, ch), F32)],
        compiler_params=_cparams(("parallel", "arbitrary")),
        name="conv_prompt",
    )(c_in, c_in, init, *params, sb)


def _conv_sample(c_in, state, params, sb):
    db, t_new, ch = c_in.shape
    n_st = state.shape[1]
    rows_pad = -(-(n_st + t_new) // 8) * 8
    return pl.pallas_call(
        _conv_sample_kernel,
        out_shape=jax.ShapeDtypeStruct((db * t_new, ch), BF16),
        scratch_shapes=[pltpu.VMEM((db, rows_pad, ch), F32), pltpu.VMEM((db * t_new, ch), F32)],
        compiler_params=pltpu.CompilerParams(vmem_limit_bytes=VMEM_LIMIT),
        name="conv_sample",
    )(c_in, state, *params, sb)


def _out_kernel(x_ref, ma_ref, mb_ref, *rest, gated):
    if gated:
        sa_ref, wa_ref, wb_ref, gp_ref, y_ref = rest
        ma = (ma_ref[0] * sa_ref[0]).astype(BF16)
    else:
        wa_ref, wb_ref, gp_ref, y_ref = rest
        ma = ma_ref[0]
    z = _dot(ma, wa_ref[...]) + _dot(mb_ref[0], wb_ref[...])
    ms = jnp.mean(z * z, axis=-1, keepdims=True)
    y_ref[0] = x_ref[0] + z * lax.rsqrt(ms + RMS_EPS) * gp_ref[...]


def _out_proj(x, ma, mb, sa, w_out, g_post, *, tm):
    b, s, d = x.shape
    na = ma.shape[-1]
    wa = w_out[:na].astype(BF16)
    wb = w_out[na:].astype(BF16)
    row = lambda width: pl.BlockSpec((1, tm, width), lambda bi, i: (bi, i, 0))
    const = lambda a: pl.BlockSpec(a.shape, lambda bi, i: (0, 0))
    gp = g_post.reshape(1, d)
    gate_in, gate_spec = ([sa], [row(na)]) if sa is not None else ([], [])
    return pl.pallas_call(
        functools.partial(_out_kernel, gated=sa is not None),
        grid=(b, s // tm),
        in_specs=[row(d), row(na), row(mb.shape[-1])] + gate_spec + [const(wa), const(wb), const(gp)],
        out_specs=row(d),
        out_shape=jax.ShapeDtypeStruct((b, s, d), F32),
        compiler_params=_cparams(("parallel", "parallel")),
        name="out_proj",
    )(x, ma, mb, *gate_in, wa, wb, gp)


def _split_w_in(w_in):
    d = w_in.shape[0]
    c0 = NSA_WIDTH + 6 * KV_WIDTH
    n_gate = 3 * N_HEADS
    conv_ch = (w_in.shape[1] - c0 - n_gate - NSA_WIDTH) // 3
    w_qkv = w_in[:, :c0].astype(BF16)
    wg = w_in[:, c0:c0 + n_gate].reshape(d, N_KV, HPG, 3)
    wg = jnp.transpose(wg, (0, 1, 3, 2)).reshape(d, N_KV, 3 * HPG)
    wg = jnp.pad(wg, ((0, 0), (0, 0), (0, LANES - 3 * HPG))).reshape(d, N_KV * LANES)
    z_a = w_in[:, c0 + n_gate:c0 + n_gate + NSA_WIDTH]
    glu0 = c0 + n_gate + NSA_WIDTH
    w_glu = w_in[:, glu0:glu0 + 2 * conv_ch].astype(BF16)
    z_b = w_in[:, glu0 + 2 * conv_ch:]
    assert conv_ch == NSA_WIDTH
    w_gate = jnp.concatenate([z_a, z_b, wg], axis=1).astype(BF16)
    return w_qkv, w_gate, w_glu


def _token_minor(cache):
    n, tokens = cache.shape[:2]
    return jnp.transpose(cache, (0, 2, 3, 1)).reshape(n, KV_WIDTH, tokens)


def kernel(x_prompt, x_sample, cache_k_cmp, cache_v_cmp, cache_k_slc, cache_v_slc, cache_k_win, cache_v_win,
           state_conv, page_table, g_pre, w_in, cmp_w1_k, cmp_w2_k, cmp_pe_k, cmp_w1_v, cmp_w2_v, cmp_pe_v,
           rel_bias, conv_w, conv_b, ln_g, ln_b, w_pw, b_pw, w_out, g_post):
    depth = g_pre.shape[0]
    assert depth == 1
    layer = 0
    b, s, d = x_prompt.shape
    db, t_new, _ = x_sample.shape
    past = page_table.shape[1] * PAGE_SIZE
    conv_ch = conv_w.shape[-1]

    w_qkv, w_gate, w_glu = _split_w_in(w_in[layer])
    wts_k = _cmp_weights(cmp_w1_k[layer], cmp_w2_k[layer], cmp_pe_k[layer])
    wts_v = _cmp_weights(cmp_w1_v[layer], cmp_w2_v[layer], cmp_pe_v[layer])
    cparams = _conv_params(conv_w[layer], conv_b[layer], ln_g[layer], ln_b[layer], w_pw[layer], b_pw[layer])
    chunk_w = CMP_STRIDE * KV_WIDTH

    (q_t, kc, vc, ks, vs, kw, vw, ks_g, kw_g, vs_t, vw_t), sa, sb, gt, c_in = _projections(
        x_prompt, g_pre[layer], w_qkv, w_gate, w_glu, tm=512, attn_layouts=True)
    n_ch = s // CMP_STRIDE
    kcb = _compress(kc[:, :n_ch * CMP_STRIDE].reshape(b, n_ch, chunk_w), wts_k, layout="group_rows")
    vcb_t = _compress(vc[:, :n_ch * CMP_STRIDE].reshape(b, n_ch, chunk_w), wts_v, layout="group_cols")
    ma = _nsa_prompt(q_t, gt, sa, kcb, vcb_t, ks_g, vs_t, kw_g, vw_t, rel_bias)
    mb = _conv_prompt(c_in, jnp.zeros((b, CONV_HALO, conv_ch), F32), cparams, sb, ts=256)
    y_prompt = _out_proj(x_prompt, ma, mb, None, w_out[layer], g_post[layer], tm=512)
    n_keep = min(WINDOW, s)
    kv5 = lambda a: a.reshape(1, a.shape[0], a.shape[1], N_KV, HEAD_DIM)
    outs_p = (kv5(kc), kv5(vc), kv5(ks), kv5(vs), kv5(kw[:, -n_keep:]), kv5(vw[:, -n_keep:]),
              c_in[None, :, -(CONV_WIDTH - 1):])

    xs = x_sample.reshape(1, db * t_new, d)
    (q_s, kc_s, vc_s, ks_s, vs_s, kw_s, vw_s), sa_s, sb_s, gt_s, c_s = _projections(
        xs, g_pre[layer], w_qkv, w_gate, w_glu, tm=db * t_new, attn_layouts=False)
    tok = lambda a: a.reshape(db, t_new, a.shape[-1])
    kcb_s = _compress(_token_minor(cache_k_cmp[layer]), wts_k, page_table=page_table, layout="cols")
    vcb_s = _compress(_token_minor(cache_v_cmp[layer]), wts_v, page_table=page_table, layout="rows")
    o_a = _nsa_sample(tok(q_s), tok(gt_s), kcb_s, vcb_s,
                      _token_minor(cache_k_slc[layer]), _token_minor(cache_v_slc[layer]), page_table,
                      tok(ks_s), tok(vs_s), _token_minor(cache_k_win[layer]), _token_minor(cache_v_win[layer]),
                      tok(kw_s), tok(vw_s), rel_bias, past=past)
    mb_s = _conv_sample(tok(c_s), state_conv[layer], cparams, sb_s[0])
    y_sample = _out_proj(xs, o_a.reshape(1, db * t_new, NSA_WIDTH), mb_s[None], sa_s, w_out[layer],
                         g_post[layer], tm=db * t_new).reshape(db, t_new, d)
    n_keep_s = min(WINDOW, past + t_new)
    kv5s = lambda a: a.reshape(1, db, t_new, N_KV, HEAD_DIM)
    win = lambda cache, new: jnp.concatenate(
        [cache[layer], new.reshape(db, t_new, N_KV, HEAD_DIM)], axis=1)[None, :, -n_keep_s:]
    conv_s = jnp.concatenate([state_conv[layer], tok(c_s)], axis=1)[None, :, -(CONV_WIDTH - 1):]
    outs_s = (kv5s(kc_s), kv5s(vc_s), kv5s(ks_s), kv5s(vs_s), win(cache_k_win, kw_s), win(cache_v_win, vw_s), conv_s)
    return (y_prompt, y_sample) + outs_p + outs_s
```
